```python
import jax, jax.numpy as jnp
from jax import lax
import numpy as np

D_MODEL = 1024
BATCH = 8
SEQ = 16384
DEPTH = 4

CHUNK = 64
RET_HEADS = 4
RET_QK_DIM = 256
RET_V_DIM = 512
RET_QK_W = RET_HEADS * RET_QK_DIM
RET_V_W = RET_HEADS * RET_V_DIM
ROPE_THETA = 10000.0
CONV_CH = D_MODEL
CONV_WIDTH = 31
MEM_LEN = 256
X_HEADS = 4
X_HEAD_DIM = D_MODEL // X_HEADS
FFN_DIM = 2816
FFN_CONV_WIDTH = 3
IN_SPLITS = (RET_QK_W, RET_QK_W, RET_V_W, RET_V_W, 2 * CONV_CH, 2 * D_MODEL)
IN_W = sum(IN_SPLITS)
RMS_EPS = 1e-6
LN_EPS = 1e-5

kernel_name = "hybrid_retention_conformer_stream_block"


def rms_norm(x, g):
    xf = x.astype(jnp.float32)
    y = xf * lax.rsqrt(jnp.mean(xf * xf, axis=-1, keepdims=True) + RMS_EPS)
    return (y * g.astype(jnp.float32)).astype(x.dtype)


def layer_norm(x, g, b):
    xf = x.astype(jnp.float32)
    mu = jnp.mean(xf, axis=-1, keepdims=True)
    var = jnp.mean(jnp.square(xf - mu), axis=-1, keepdims=True)
    y = (xf - mu) * lax.rsqrt(var + LN_EPS)
    return (y * g.astype(jnp.float32) + b.astype(jnp.float32)).astype(x.dtype)


def causal_dwconv(x, w, b):
    width = w.shape[0]
    y = lax.conv_general_dilated(
        x, w[:, None, :].astype(x.dtype), window_strides=(1,),
        padding=[(width - 1, 0)], dimension_numbers=("NWC", "WIO", "NWC"),
        feature_group_count=x.shape[-1])
    return y + b.astype(x.dtype)


def rotary(x, cos, sin):
    half = x.shape[-1] // 2
    x1, x2 = x[..., :half], x[..., half:]
    return jnp.concatenate([x1 * cos - x2 * sin, x2 * cos + x1 * sin], axis=-1)


def chunk_retention(q, k, v):
    bsz, seq, heads, dk = q.shape
    dv = v.shape[-1]
    n_chunks = seq // CHUNK
    dt = q.dtype
    qc = q.reshape(bsz, n_chunks, CHUNK, heads, dk)
    kc = k.reshape(bsz, n_chunks, CHUNK, heads, dk)
    vc = v.reshape(bsz, n_chunks, CHUNK, heads, dv)

    log_gamma = jnp.log(1.0 - jnp.power(2.0, -5.0 - jnp.arange(heads, dtype=jnp.float32)))
    idx = jnp.arange(CHUNK, dtype=jnp.float32)
    dist = jnp.abs(idx[:, None] - idx[None, :])
    d_inner = jnp.exp(log_gamma[:, None, None] * dist).astype(dt)
    decay_q = jnp.exp(log_gamma[None, :] * (idx[:, None] + 1.0)).astype(dt)
    decay_k = jnp.exp(log_gamma[None, :] * (CHUNK - 1.0 - idx[:, None])).astype(dt)
    decay_chunk = jnp.exp(log_gamma * CHUNK).astype(dt)

    scores = jnp.einsum("bnchd,bnkhd->bnhck", qc, kc) * d_inner
    o_inner = jnp.einsum("bnhck,bnkhv->bnchv", scores, vc)

    kc_dec = kc * decay_k[:, :, None]
    xs = (jnp.moveaxis(qc, 1, 0), jnp.moveaxis(kc_dec, 1, 0), jnp.moveaxis(vc, 1, 0))

    def step(state, inp):
        q_i, k_i, v_i = inp
        cross = jnp.einsum("bchk,bhkv->bchv", q_i, state)
        state = state * decay_chunk[None, :, None, None] + jnp.einsum("bchk,bchv->bhkv", k_i, v_i)
        return state, cross

    state0 = jnp.zeros((bsz, heads, dk, dv), dt)
    _, cross = lax.scan(step, state0, xs)
    o_cross = jnp.moveaxis(cross, 0, 1) * decay_q[:, :, None]
    return (o_inner + o_cross).reshape(bsz, seq, heads, dv)


def head_group_norm(o, g):
    of = o.astype(jnp.float32)
    mu = jnp.mean(of, axis=-1, keepdims=True)
    var = jnp.mean(jnp.square(of - mu), axis=-1, keepdims=True)
    y = (of - mu) * lax.rsqrt(var + LN_EPS)
    bsz, seq, heads, dv = o.shape
    return (y.reshape(bsz, seq, heads * dv) * g.astype(jnp.float32)).astype(o.dtype)


def mixer_sublayer(u, cos, sin, w_in, b_gate, ret_gn_g, w_ret_out, conv_dw_w, conv_dw_b,
                   conv_ln_g, conv_ln_b, w_conv_out, b_conv_out, w_mix_out):
    bsz, seq, _ = u.shape
    p = u @ w_in
    cuts = np.cumsum(IN_SPLITS)[:-1].tolist()
    q, k, v, g_ret, c_in, gates = jnp.split(p, cuts, axis=-1)

    q = rotary(q.reshape(bsz, seq, RET_HEADS, RET_QK_DIM), cos, sin) * (RET_QK_DIM ** -0.5)
    k = rotary(k.reshape(bsz, seq, RET_HEADS, RET_QK_DIM), cos, sin)
    v = v.reshape(bsz, seq, RET_HEADS, RET_V_DIM)
    o = head_group_norm(chunk_retention(q, k, v), ret_gn_g)
    y_a = (jax.nn.silu(g_ret) * o) @ w_ret_out

    a, b = jnp.split(c_in, 2, axis=-1)
    c = a * jax.nn.sigmoid(b)
    c = causal_dwconv(c, conv_dw_w, conv_dw_b)
    c = jax.nn.silu(layer_norm(c, conv_ln_g, conv_ln_b))
    y_b = c @ w_conv_out + b_conv_out

    g_a, g_b = jnp.split(jax.nn.sigmoid(gates + b_gate), 2, axis=-1)
    return (g_a * y_a + g_b * y_b) @ w_mix_out


def memory_cross_attention(h, mem_n, w_xq, w_xkv, w_xo):
    bsz, seq, _ = h.shape
    q = (h @ w_xq).reshape(bsz, seq, X_HEADS, X_HEAD_DIM)
    k, v = jnp.split(mem_n @ w_xkv, 2, axis=-1)
    k = k.reshape(bsz, MEM_LEN, X_HEADS, X_HEAD_DIM)
    v = v.reshape(bsz, MEM_LEN, X_HEADS, X_HEAD_DIM)
    s = jnp.einsum("bshd,bmhd->bhsm", q, k).astype(jnp.float32) * (X_HEAD_DIM ** -0.5)
    pr = jax.nn.softmax(s, axis=-1).astype(h.dtype)
    o = jnp.einsum("bhsm,bmhd->bshd", pr, v).reshape(bsz, seq, D_MODEL)
    return o @ w_xo


def conv_ffn(h, w_up, ffn_dw_w, ffn_dw_b, w_down):
    val, gate = jnp.split(h @ w_up, 2, axis=-1)
    gate = causal_dwconv(gate, ffn_dw_w, ffn_dw_b)
    return (jax.nn.silu(gate) * val) @ w_down


def _fwd_setup_inputs(seed: int = 0) -> dict:
    key = jax.random.key(seed)
    ks = iter(jax.random.split(key, 32))
    f32 = jnp.float32
    L, D = DEPTH, D_MODEL

    def w(shape, fan_in):
        return jax.random.normal(next(ks), shape, f32) * (fan_in ** -0.5)

    def gain(shape):
        return 1.0 + 0.02 * jax.random.normal(next(ks), shape, f32)

    def bias(shape):
        return 0.02 * jax.random.normal(next(ks), shape, f32)

    x = jax.random.normal(next(ks), (BATCH, SEQ, D), f32)
    mem = jax.random.normal(next(ks), (BATCH, MEM_LEN, D), f32)
    positions = jnp.broadcast_to(jnp.arange(SEQ, dtype=jnp.int32)[None, :], (BATCH, SEQ))
    return {
        "x": x,
        "mem": mem,
        "positions": positions,
        "norm_mix_g": gain((L, D)),
        "w_in": w((L, D, IN_W), D),
        "b_gate": bias((L, 2 * D)),
        "ret_gn_g": gain((L, RET_V_W)),
        "w_ret_out": w((L, RET_V_W, D), RET_V_W),
        "conv_dw_w": w((L, CONV_WIDTH, CONV_CH), CONV_WIDTH),
        "conv_dw_b": bias((L, CONV_CH)),
        "conv_ln_g": gain((L, CONV_CH)),
        "conv_ln_b": bias((L, CONV_CH)),
        "w_conv_out": w((L, CONV_CH, D), CONV_CH),
        "b_conv_out": bias((L, D)),
        "w_mix_out": w((L, D, D), D),
        "norm_xattn_g": gain((L, D)),
        "norm_mem_g": gain((L, D)),
        "w_xq": w((L, D, D), D),
        "w_xkv": w((L, D, 2 * D), D),
        "w_xo": w((L, D, D), D),
        "norm_ffn_g": gain((L, D)),
        "w_up": w((L, D, 2 * FFN_DIM), D),
        "ffn_dw_w": w((L, FFN_CONV_WIDTH, FFN_DIM), FFN_CONV_WIDTH),
        "ffn_dw_b": bias((L, FFN_DIM)),
        "w_down": w((L, FFN_DIM, D), FFN_DIM),
        "norm_final_g": gain((D,)),
    }


def _fwd_reference(x, mem, positions, norm_mix_g, w_in, b_gate, ret_gn_g, w_ret_out,
              conv_dw_w, conv_dw_b, conv_ln_g, conv_ln_b, w_conv_out, b_conv_out,
              w_mix_out, norm_xattn_g, norm_mem_g, w_xq, w_xkv, w_xo, norm_ffn_g,
              w_up, ffn_dw_w, ffn_dw_b, w_down, norm_final_g):
    inv_freq = 1.0 / (ROPE_THETA ** (jnp.arange(0, RET_QK_DIM, 2, dtype=jnp.float32) / RET_QK_DIM))
    ang = positions.astype(jnp.float32)[..., None] * inv_freq
    cos = jnp.cos(ang)[:, :, None, :].astype(x.dtype)
    sin = jnp.sin(ang)[:, :, None, :].astype(x.dtype)

    h = x
    for l in range(DEPTH):
        u = rms_norm(h, norm_mix_g[l])
        h = h + mixer_sublayer(u, cos, sin, w_in[l], b_gate[l], ret_gn_g[l], w_ret_out[l],
                               conv_dw_w[l], conv_dw_b[l], conv_ln_g[l], conv_ln_b[l],
                               w_conv_out[l], b_conv_out[l], w_mix_out[l])
        mem_n = rms_norm(mem, norm_mem_g[l])
        h = h + memory_cross_attention(rms_norm(h, norm_xattn_g[l]), mem_n,
                                       w_xq[l], w_xkv[l], w_xo[l])
        h = h + conv_ffn(rms_norm(h, norm_ffn_g[l]), w_up[l], ffn_dw_w[l], ffn_dw_b[l], w_down[l])
    return rms_norm(h, norm_final_g)


import jax as _jax
import jax.numpy as _jnp

TWIN_FORMAT = 'train_step'
FWD_PARAMS = ['x', 'mem', 'positions', 'norm_mix_g', 'w_in', 'b_gate', 'ret_gn_g', 'w_ret_out', 'conv_dw_w', 'conv_dw_b', 'conv_ln_g', 'conv_ln_b', 'w_conv_out', 'b_conv_out', 'w_mix_out', 'norm_xattn_g', 'norm_mem_g', 'w_xq', 'w_xkv', 'w_xo', 'norm_ffn_g', 'w_up', 'ffn_dw_w', 'ffn_dw_b', 'w_down', 'norm_final_g']
TWIN_WEIGHTS = ['norm_mix_g', 'w_in', 'b_gate', 'ret_gn_g', 'w_ret_out', 'conv_dw_w', 'conv_dw_b', 'conv_ln_g', 'conv_ln_b', 'w_conv_out', 'b_conv_out', 'w_mix_out', 'norm_xattn_g', 'norm_mem_g', 'w_xq', 'w_xkv', 'w_xo', 'norm_ffn_g', 'w_up', 'ffn_dw_w', 'ffn_dw_b', 'w_down', 'norm_final_g']
TWIN_DIFF_INPUT = 'x'
TWIN_INPUTS = ['x', 'mem', 'positions', 'norm_mix_g', 'w_in', 'b_gate', 'ret_gn_g', 'w_ret_out', 'conv_dw_w', 'conv_dw_b', 'conv_ln_g', 'conv_ln_b', 'w_conv_out', 'b_conv_out', 'w_mix_out', 'norm_xattn_g', 'norm_mem_g', 'w_xq', 'w_xkv', 'w_xo', 'norm_ffn_g', 'w_up', 'ffn_dw_w', 'ffn_dw_b', 'w_down', 'norm_final_g', 'loss_target', 'm_norm_mix_g', 'm_w_in', 'm_b_gate', 'm_ret_gn_g', 'm_w_ret_out', 'm_conv_dw_w', 'm_conv_dw_b', 'm_conv_ln_g', 'm_conv_ln_b', 'm_w_conv_out', 'm_b_conv_out', 'm_w_mix_out', 'm_norm_xattn_g', 'm_norm_mem_g', 'm_w_xq', 'm_w_xkv', 'm_w_xo', 'm_norm_ffn_g', 'm_w_up', 'm_ffn_dw_w', 'm_ffn_dw_b', 'm_w_down', 'm_norm_final_g', 'v_norm_mix_g', 'v_w_in', 'v_b_gate', 'v_ret_gn_g', 'v_w_ret_out', 'v_conv_dw_w', 'v_conv_dw_b', 'v_conv_ln_g', 'v_conv_ln_b', 'v_w_conv_out', 'v_b_conv_out', 'v_w_mix_out', 'v_norm_xattn_g', 'v_norm_mem_g', 'v_w_xq', 'v_w_xkv', 'v_w_xo', 'v_norm_ffn_g', 'v_w_up', 'v_ffn_dw_w', 'v_ffn_dw_b', 'v_w_down', 'v_norm_final_g']
TWIN_OUTPUTS = ['loss', 'grad_x', 'grad_norm_mix_g', 'grad_w_in', 'grad_b_gate', 'grad_ret_gn_g', 'grad_w_ret_out', 'grad_conv_dw_w', 'grad_conv_dw_b', 'grad_conv_ln_g', 'grad_conv_ln_b', 'grad_w_conv_out', 'grad_b_conv_out', 'grad_w_mix_out', 'grad_norm_xattn_g', 'grad_norm_mem_g', 'grad_w_xq', 'grad_w_xkv', 'grad_w_xo', 'grad_norm_ffn_g', 'grad_w_up', 'grad_ffn_dw_w', 'grad_ffn_dw_b', 'grad_w_down', 'grad_norm_final_g', 'delta_norm_mix_g', 'delta_w_in', 'delta_b_gate', 'delta_ret_gn_g', 'delta_w_ret_out', 'delta_conv_dw_w', 'delta_conv_dw_b', 'delta_conv_ln_g', 'delta_conv_ln_b', 'delta_w_conv_out', 'delta_b_conv_out', 'delta_w_mix_out', 'delta_norm_xattn_g', 'delta_norm_mem_g', 'delta_w_xq', 'delta_w_xkv', 'delta_w_xo', 'delta_norm_ffn_g', 'delta_w_up', 'delta_ffn_dw_w', 'delta_ffn_dw_b', 'delta_w_down', 'delta_norm_final_g', 'new_m_norm_mix_g', 'new_m_w_in', 'new_m_b_gate', 'new_m_ret_gn_g', 'new_m_w_ret_out', 'new_m_conv_dw_w', 'new_m_conv_dw_b', 'new_m_conv_ln_g', 'new_m_conv_ln_b', 'new_m_w_conv_out', 'new_m_b_conv_out', 'new_m_w_mix_out', 'new_m_norm_xattn_g', 'new_m_norm_mem_g', 'new_m_w_xq', 'new_m_w_xkv', 'new_m_w_xo', 'new_m_norm_ffn_g', 'new_m_w_up', 'new_m_ffn_dw_w', 'new_m_ffn_dw_b', 'new_m_w_down', 'new_m_norm_final_g', 'new_v_norm_mix_g', 'new_v_w_in', 'new_v_b_gate', 'new_v_ret_gn_g', 'new_v_w_ret_out', 'new_v_conv_dw_w', 'new_v_conv_dw_b', 'new_v_conv_ln_g', 'new_v_conv_ln_b', 'new_v_w_conv_out', 'new_v_b_conv_out', 'new_v_w_mix_out', 'new_v_norm_xattn_g', 'new_v_norm_mem_g', 'new_v_w_xq', 'new_v_w_xkv', 'new_v_w_xo', 'new_v_norm_ffn_g', 'new_v_w_up', 'new_v_ffn_dw_w', 'new_v_ffn_dw_b', 'new_v_w_down', 'new_v_norm_final_g']
TWIN_LEAF_KINDS = {'loss': 'loss', 'grad_x': 'grad_x', 'grad_norm_mix_g': 'grad_w', 'grad_w_in': 'grad_w', 'grad_b_gate': 'grad_w', 'grad_ret_gn_g': 'grad_w', 'grad_w_ret_out': 'grad_w', 'grad_conv_dw_w': 'grad_w', 'grad_conv_dw_b': 'grad_w', 'grad_conv_ln_g': 'grad_w', 'grad_conv_ln_b': 'grad_w', 'grad_w_conv_out': 'grad_w', 'grad_b_conv_out': 'grad_w', 'grad_w_mix_out': 'grad_w', 'grad_norm_xattn_g': 'grad_w', 'grad_norm_mem_g': 'grad_w', 'grad_w_xq': 'grad_w', 'grad_w_xkv': 'grad_w', 'grad_w_xo': 'grad_w', 'grad_norm_ffn_g': 'grad_w', 'grad_w_up': 'grad_w', 'grad_ffn_dw_w': 'grad_w', 'grad_ffn_dw_b': 'grad_w', 'grad_w_down': 'grad_w', 'grad_norm_final_g': 'grad_w', 'delta_norm_mix_g': 'delta_w', 'delta_w_in': 'delta_w', 'delta_b_gate': 'delta_w', 'delta_ret_gn_g': 'delta_w', 'delta_w_ret_out': 'delta_w', 'delta_conv_dw_w': 'delta_w', 'delta_conv_dw_b': 'delta_w', 'delta_conv_ln_g': 'delta_w', 'delta_conv_ln_b': 'delta_w', 'delta_w_conv_out': 'delta_w', 'delta_b_conv_out': 'delta_w', 'delta_w_mix_out': 'delta_w', 'delta_norm_xattn_g': 'delta_w', 'delta_norm_mem_g': 'delta_w', 'delta_w_xq': 'delta_w', 'delta_w_xkv': 'delta_w', 'delta_w_xo': 'delta_w', 'delta_norm_ffn_g': 'delta_w', 'delta_w_up': 'delta_w', 'delta_ffn_dw_w': 'delta_w', 'delta_ffn_dw_b': 'delta_w', 'delta_w_down': 'delta_w', 'delta_norm_final_g': 'delta_w', 'new_m_norm_mix_g': 'new_m', 'new_m_w_in': 'new_m', 'new_m_b_gate': 'new_m', 'new_m_ret_gn_g': 'new_m', 'new_m_w_ret_out': 'new_m', 'new_m_conv_dw_w': 'new_m', 'new_m_conv_dw_b': 'new_m', 'new_m_conv_ln_g': 'new_m', 'new_m_conv_ln_b': 'new_m', 'new_m_w_conv_out': 'new_m', 'new_m_b_conv_out': 'new_m', 'new_m_w_mix_out': 'new_m', 'new_m_norm_xattn_g': 'new_m', 'new_m_norm_mem_g': 'new_m', 'new_m_w_xq': 'new_m', 'new_m_w_xkv': 'new_m', 'new_m_w_xo': 'new_m', 'new_m_norm_ffn_g': 'new_m', 'new_m_w_up': 'new_m', 'new_m_ffn_dw_w': 'new_m', 'new_m_ffn_dw_b': 'new_m', 'new_m_w_down': 'new_m', 'new_m_norm_final_g': 'new_m', 'new_v_norm_mix_g': 'new_v', 'new_v_w_in': 'new_v', 'new_v_b_gate': 'new_v', 'new_v_ret_gn_g': 'new_v', 'new_v_w_ret_out': 'new_v', 'new_v_conv_dw_w': 'new_v', 'new_v_conv_dw_b': 'new_v', 'new_v_conv_ln_g': 'new_v', 'new_v_conv_ln_b': 'new_v', 'new_v_w_conv_out': 'new_v', 'new_v_b_conv_out': 'new_v', 'new_v_w_mix_out': 'new_v', 'new_v_norm_xattn_g': 'new_v', 'new_v_norm_mem_g': 'new_v', 'new_v_w_xq': 'new_v', 'new_v_w_xkv': 'new_v', 'new_v_w_xo': 'new_v', 'new_v_norm_ffn_g': 'new_v', 'new_v_w_up': 'new_v', 'new_v_ffn_dw_w': 'new_v', 'new_v_ffn_dw_b': 'new_v', 'new_v_w_down': 'new_v', 'new_v_norm_final_g': 'new_v'}


def _forward(args):
    return _fwd_reference(*[args[k] for k in FWD_PARAMS])


def _output_shape():
    def fwd():
        inp = _fwd_setup_inputs(0)
        return _fwd_reference(*[inp[k] for k in FWD_PARAMS])
    out = _jax.eval_shape(fwd)
    return out.shape, out.dtype

N_MICROBATCH = 1
ADAM_LR = 0.001
ADAM_B1 = 0.9
ADAM_B2 = 0.999
ADAM_EPS = 1e-08
ADAM_WD = 0.01
ADAM_STEP = 10
PER_EXAMPLE_BATCH_AXIS = {'x': 0, 'mem': 0, 'positions': 0, 'loss_target': 0}
SHARED_INPUTS = []
_WEIGHT_DTYPES = {'norm_mix_g': _jnp.float32, 'w_in': _jnp.float32, 'b_gate': _jnp.float32, 'ret_gn_g': _jnp.float32, 'w_ret_out': _jnp.float32, 'conv_dw_w': _jnp.float32, 'conv_dw_b': _jnp.float32, 'conv_ln_g': _jnp.float32, 'conv_ln_b': _jnp.float32, 'w_conv_out': _jnp.float32, 'b_conv_out': _jnp.float32, 'w_mix_out': _jnp.float32, 'norm_xattn_g': _jnp.float32, 'norm_mem_g': _jnp.float32, 'w_xq': _jnp.float32, 'w_xkv': _jnp.float32, 'w_xo': _jnp.float32, 'norm_ffn_g': _jnp.float32, 'w_up': _jnp.float32, 'ffn_dw_w': _jnp.float32, 'ffn_dw_b': _jnp.float32, 'w_down': _jnp.float32, 'norm_final_g': _jnp.float32}
MOMENT_SCALE = {'norm_mix_g': 3.085045e-01, 'w_in': 9.061493e-02, 'b_gate': 4.790710e-02, 'ret_gn_g': 8.494851e-02, 'w_ret_out': 1.203430e-01, 'conv_dw_w': 1.257054e-01, 'conv_dw_b': 2.774539e-01, 'conv_ln_g': 1.552118e-01, 'conv_ln_b': 1.411145e-01, 'w_conv_out': 1.243098e-01, 'b_conv_out': 2.382404e-01, 'w_mix_out': 1.731330e-01, 'norm_xattn_g': 3.903955e-02, 'norm_mem_g': 6.047027e-02, 'w_xq': 3.884171e-02, 'w_xkv': 3.966528e-02, 'w_xo': 4.044902e-02, 'norm_ffn_g': 2.700682e-01, 'w_up': 1.154002e-01, 'ffn_dw_w': 1.174513e-01, 'ffn_dw_b': 1.098258e-01, 'w_down': 1.884956e-01, 'norm_final_g': 1.281406e+02}


def _to_microbatches(a, axis):
    t = _jnp.moveaxis(a, axis, 0)
    t = t.reshape((N_MICROBATCH, t.shape[0] // N_MICROBATCH) + t.shape[1:])
    return _jnp.moveaxis(t, 1, axis + 1)


def setup_inputs(seed: int = 0) -> dict:
    inp = _fwd_setup_inputs(seed)
    key = _jax.random.fold_in(_jax.random.key(seed), 7919)
    shape, _ = _output_shape()
    out = dict(inp)
    out["loss_target"] = _jax.random.normal(_jax.random.fold_in(key, 0), shape, _jnp.float32)
    for i, name in enumerate(TWIN_WEIGHTS):
        w = inp[name].astype(_jnp.float32)
        if MOMENT_SCALE is None:
            s = _jnp.sqrt(_jnp.mean(_jnp.square(w)) + 1e-30)
        else:
            s = MOMENT_SCALE[name]
        km, kv = _jax.random.split(_jax.random.fold_in(key, i + 1))
        out[name] = w
        out["m_" + name] = s * _jax.random.normal(km, w.shape, _jnp.float32)
        out["v_" + name] = (s * s) * _jax.random.uniform(kv, w.shape, _jnp.float32, 0.5, 1.5)
    if N_MICROBATCH > 1:
        for name, axis in PER_EXAMPLE_BATCH_AXIS.items():
            out[name] = _to_microbatches(out[name], axis)
    return {'x': out['x'], 'mem': out['mem'], 'positions': out['positions'], 'norm_mix_g': out['norm_mix_g'], 'w_in': out['w_in'], 'b_gate': out['b_gate'], 'ret_gn_g': out['ret_gn_g'], 'w_ret_out': out['w_ret_out'], 'conv_dw_w': out['conv_dw_w'], 'conv_dw_b': out['conv_dw_b'], 'conv_ln_g': out['conv_ln_g'], 'conv_ln_b': out['conv_ln_b'], 'w_conv_out': out['w_conv_out'], 'b_conv_out': out['b_conv_out'], 'w_mix_out': out['w_mix_out'], 'norm_xattn_g': out['norm_xattn_g'], 'norm_mem_g': out['norm_mem_g'], 'w_xq': out['w_xq'], 'w_xkv': out['w_xkv'], 'w_xo': out['w_xo'], 'norm_ffn_g': out['norm_ffn_g'], 'w_up': out['w_up'], 'ffn_dw_w': out['ffn_dw_w'], 'ffn_dw_b': out['ffn_dw_b'], 'w_down': out['w_down'], 'norm_final_g': out['norm_final_g'], 'loss_target': out['loss_target'], 'm_norm_mix_g': out['m_norm_mix_g'], 'm_w_in': out['m_w_in'], 'm_b_gate': out['m_b_gate'], 'm_ret_gn_g': out['m_ret_gn_g'], 'm_w_ret_out': out['m_w_ret_out'], 'm_conv_dw_w': out['m_conv_dw_w'], 'm_conv_dw_b': out['m_conv_dw_b'], 'm_conv_ln_g': out['m_conv_ln_g'], 'm_conv_ln_b': out['m_conv_ln_b'], 'm_w_conv_out': out['m_w_conv_out'], 'm_b_conv_out': out['m_b_conv_out'], 'm_w_mix_out': out['m_w_mix_out'], 'm_norm_xattn_g': out['m_norm_xattn_g'], 'm_norm_mem_g': out['m_norm_mem_g'], 'm_w_xq': out['m_w_xq'], 'm_w_xkv': out['m_w_xkv'], 'm_w_xo': out['m_w_xo'], 'm_norm_ffn_g': out['m_norm_ffn_g'], 'm_w_up': out['m_w_up'], 'm_ffn_dw_w': out['m_ffn_dw_w'], 'm_ffn_dw_b': out['m_ffn_dw_b'], 'm_w_down': out['m_w_down'], 'm_norm_final_g': out['m_norm_final_g'], 'v_norm_mix_g': out['v_norm_mix_g'], 'v_w_in': out['v_w_in'], 'v_b_gate': out['v_b_gate'], 'v_ret_gn_g': out['v_ret_gn_g'], 'v_w_ret_out': out['v_w_ret_out'], 'v_conv_dw_w': out['v_conv_dw_w'], 'v_conv_dw_b': out['v_conv_dw_b'], 'v_conv_ln_g': out['v_conv_ln_g'], 'v_conv_ln_b': out['v_conv_ln_b'], 'v_w_conv_out': out['v_w_conv_out'], 'v_b_conv_out': out['v_b_conv_out'], 'v_w_mix_out': out['v_w_mix_out'], 'v_norm_xattn_g': out['v_norm_xattn_g'], 'v_norm_mem_g': out['v_norm_mem_g'], 'v_w_xq': out['v_w_xq'], 'v_w_xkv': out['v_w_xkv'], 'v_w_xo': out['v_w_xo'], 'v_norm_ffn_g': out['v_norm_ffn_g'], 'v_w_up': out['v_w_up'], 'v_ffn_dw_w': out['v_ffn_dw_w'], 'v_ffn_dw_b': out['v_ffn_dw_b'], 'v_w_down': out['v_w_down'], 'v_norm_final_g': out['v_norm_final_g']}


def _loss(weights, diff, rest, loss_target):
    with _jax.named_scope("forward"):
        args = {**rest, TWIN_DIFF_INPUT: diff, **{k: w.astype(_WEIGHT_DTYPES[k]) for k, w in weights.items()}}
        y = _forward(args)
    with _jax.named_scope("loss_head"):
        err = _jnp.square(y.astype(_jnp.float32) - loss_target)
        return 0.5 * _jnp.sum(_jnp.mean(err, axis=-1)) if err.ndim else 0.5 * err


def _adamw(w, g, m, v):
    m = ADAM_B1 * m + (1.0 - ADAM_B1) * g
    v = ADAM_B2 * v + (1.0 - ADAM_B2) * _jnp.square(g)
    m_hat = m / (1.0 - ADAM_B1 ** ADAM_STEP)
    v_hat = v / (1.0 - ADAM_B2 ** ADAM_STEP)
    delta = -ADAM_LR * (m_hat / (_jnp.sqrt(v_hat) + ADAM_EPS) + ADAM_WD * w)
    return delta, m, v


def reference(x, mem, positions, norm_mix_g, w_in, b_gate, ret_gn_g, w_ret_out, conv_dw_w, conv_dw_b, conv_ln_g, conv_ln_b, w_conv_out, b_conv_out, w_mix_out, norm_xattn_g, norm_mem_g, w_xq, w_xkv, w_xo, norm_ffn_g, w_up, ffn_dw_w, ffn_dw_b, w_down, norm_final_g, loss_target, m_norm_mix_g, m_w_in, m_b_gate, m_ret_gn_g, m_w_ret_out, m_conv_dw_w, m_conv_dw_b, m_conv_ln_g, m_conv_ln_b, m_w_conv_out, m_b_conv_out, m_w_mix_out, m_norm_xattn_g, m_norm_mem_g, m_w_xq, m_w_xkv, m_w_xo, m_norm_ffn_g, m_w_up, m_ffn_dw_w, m_ffn_dw_b, m_w_down, m_norm_final_g, v_norm_mix_g, v_w_in, v_b_gate, v_ret_gn_g, v_w_ret_out, v_conv_dw_w, v_conv_dw_b, v_conv_ln_g, v_conv_ln_b, v_w_conv_out, v_b_conv_out, v_w_mix_out, v_norm_xattn_g, v_norm_mem_g, v_w_xq, v_w_xkv, v_w_xo, v_norm_ffn_g, v_w_up, v_ffn_dw_w, v_ffn_dw_b, v_w_down, v_norm_final_g):
    given = dict(x=x, mem=mem, positions=positions, norm_mix_g=norm_mix_g, w_in=w_in, b_gate=b_gate, ret_gn_g=ret_gn_g, w_ret_out=w_ret_out, conv_dw_w=conv_dw_w, conv_dw_b=conv_dw_b, conv_ln_g=conv_ln_g, conv_ln_b=conv_ln_b, w_conv_out=w_conv_out, b_conv_out=b_conv_out, w_mix_out=w_mix_out, norm_xattn_g=norm_xattn_g, norm_mem_g=norm_mem_g, w_xq=w_xq, w_xkv=w_xkv, w_xo=w_xo, norm_ffn_g=norm_ffn_g, w_up=w_up, ffn_dw_w=ffn_dw_w, ffn_dw_b=ffn_dw_b, w_down=w_down, norm_final_g=norm_final_g, loss_target=loss_target, m_norm_mix_g=m_norm_mix_g, m_w_in=m_w_in, m_b_gate=m_b_gate, m_ret_gn_g=m_ret_gn_g, m_w_ret_out=m_w_ret_out, m_conv_dw_w=m_conv_dw_w, m_conv_dw_b=m_conv_dw_b, m_conv_ln_g=m_conv_ln_g, m_conv_ln_b=m_conv_ln_b, m_w_conv_out=m_w_conv_out, m_b_conv_out=m_b_conv_out, m_w_mix_out=m_w_mix_out, m_norm_xattn_g=m_norm_xattn_g, m_norm_mem_g=m_norm_mem_g, m_w_xq=m_w_xq, m_w_xkv=m_w_xkv, m_w_xo=m_w_xo, m_norm_ffn_g=m_norm_ffn_g, m_w_up=m_w_up, m_ffn_dw_w=m_ffn_dw_w, m_ffn_dw_b=m_ffn_dw_b, m_w_down=m_w_down, m_norm_final_g=m_norm_final_g, v_norm_mix_g=v_norm_mix_g, v_w_in=v_w_in, v_b_gate=v_b_gate, v_ret_gn_g=v_ret_gn_g, v_w_ret_out=v_w_ret_out, v_conv_dw_w=v_conv_dw_w, v_conv_dw_b=v_conv_dw_b, v_conv_ln_g=v_conv_ln_g, v_conv_ln_b=v_conv_ln_b, v_w_conv_out=v_w_conv_out, v_b_conv_out=v_b_conv_out, v_w_mix_out=v_w_mix_out, v_norm_xattn_g=v_norm_xattn_g, v_norm_mem_g=v_norm_mem_g, v_w_xq=v_w_xq, v_w_xkv=v_w_xkv, v_w_xo=v_w_xo, v_norm_ffn_g=v_norm_ffn_g, v_w_up=v_w_up, v_ffn_dw_w=v_ffn_dw_w, v_ffn_dw_b=v_ffn_dw_b, v_w_down=v_w_down, v_norm_final_g=v_norm_final_g)
    weights = {n: given[n] for n in TWIN_WEIGHTS}
    shared = {n: given[n] for n in SHARED_INPUTS}
    per_example = {n: given[n] for n in ['x', 'mem', 'positions']}
    grad_fn = _jax.value_and_grad(_loss, argnums=(0, 1))

    def one_microbatch(ex, loss_target):
        ex = dict(ex)
        diff = ex.pop(TWIN_DIFF_INPUT)
        return grad_fn(weights, diff, {**shared, **ex}, loss_target)

    if N_MICROBATCH == 1:
        loss, (grad_w, grad_x) = one_microbatch(per_example, given["loss_target"])
    else:
        def body(carry, xs):
            loss_sum, grad_sum = carry
            l_k, (gw_k, gx_k) = one_microbatch(xs[0], xs[1])
            with _jax.named_scope("update"):
                return (loss_sum + l_k, _jax.tree.map(_jnp.add, grad_sum, gw_k)), gx_k

        init = (_jnp.zeros((), _jnp.float32), _jax.tree.map(_jnp.zeros_like, weights))
        (loss, grad_w), grad_x = _jax.lax.scan(body, init, (per_example, given["loss_target"]))
    with _jax.named_scope("update"):
        delta_w, new_m, new_v = {}, {}, {}
        for n in TWIN_WEIGHTS:
            delta_w[n], new_m[n], new_v[n] = _adamw(weights[n], grad_w[n], given["m_" + n], given["v_" + n])
    return (loss, grad_x, *[grad_w[n] for n in TWIN_WEIGHTS], *[delta_w[n] for n in TWIN_WEIGHTS],
            *[new_m[n] for n in TWIN_WEIGHTS], *[new_v[n] for n in TWIN_WEIGHTS])
```

```python
import functools

import jax
import jax.numpy as jnp
import numpy as np
from jax import lax
from jax.experimental import pallas as pl
from jax.experimental.pallas import tpu as pltpu

F32 = jnp.float32
BF16 = jnp.bfloat16
MESH = pl.DeviceIdType.MESH

D_MODEL = 1024
CHUNK = 64
RET_HEADS = 4
RET_QK_DIM = 256
RET_V_DIM = 512
ROPE_THETA = 10000.0
CONV_WIDTH = 31
X_HEADS = 4
X_HEAD_DIM = 256
FFN_DIM = 2816
RMS_EPS = 1e-6
LN_EPS = 1e-5
ADAM_LR = 0.001
ADAM_B1 = 0.9
ADAM_B2 = 0.999
ADAM_EPS = 1e-08
ADAM_WD = 0.01
ADAM_STEP = 10

N_CHIPS = 4
N_DEV = 8
CONV_HALO = 32
FFN_HALO = 8
V7X_VMEM_LIMIT = 56 * 1024 * 1024
ROW_TILE = 256
MM_TILE_M = 512
RET_TILE = 512

BIG = ("w_in", "w_ret_out", "w_conv_out", "w_mix_out", "w_xq", "w_xkv", "w_xo", "w_up", "w_down")
COL_SHARDED = ("w_in", "w_xkv", "w_up")
SMALL_REPL = ("norm_mix_g", "b_gate", "ret_gn_g", "conv_dw_b", "conv_ln_g", "conv_ln_b", "b_conv_out",
              "norm_xattn_g", "norm_mem_g", "norm_ffn_g", "ffn_dw_b", "norm_final_g")
SMALL_SHARDED = ("conv_dw_w", "ffn_dw_w")
WEIGHTS = ('norm_mix_g', 'w_in', 'b_gate', 'ret_gn_g', 'w_ret_out', 'conv_dw_w', 'conv_dw_b', 'conv_ln_g',
           'conv_ln_b', 'w_conv_out', 'b_conv_out', 'w_mix_out', 'norm_xattn_g', 'norm_mem_g', 'w_xq', 'w_xkv',
           'w_xo', 'norm_ffn_g', 'w_up', 'ffn_dw_w', 'ffn_dw_b', 'w_down', 'norm_final_g')


def _params(sem=None):
    return pltpu.CompilerParams(dimension_semantics=sem, vmem_limit_bytes=V7X_VMEM_LIMIT)


def _sds(shape, dtype):
    return jax.ShapeDtypeStruct(tuple(shape), dtype)


def _sigmoid(x):
    return jax.nn.sigmoid(x)


def _dot(a, b, ca, cb):
    return lax.dot_general(a, b, (((ca,), (cb,)), ((), ())), preferred_element_type=F32)


def _nn(a, b):
    return _dot(a, b, 1, 0)


def _nt(a, b):
    return _dot(a, b, 1, 1)


def _tn(a, b):
    return _dot(a, b, 0, 0)


def _mm(name, dims, grid, in_specs, out_spec, out_sds, nk, operands, with_res=False):
    def body(*refs):
        if with_res:
            a_ref, b_ref, r_ref, o_ref = refs
        else:
            a_ref, b_ref, o_ref = refs
            r_ref = None
        prod = _dot(a_ref[...].astype(BF16), b_ref[...].astype(BF16), *dims)
        if nk == 1:
            if r_ref is not None:
                prod = prod + r_ref[...]
            o_ref[...] = prod.astype(o_ref.dtype)
        else:
            k = pl.program_id(2)

            @pl.when(k == 0)
            def _():
                o_ref[...] = (prod + r_ref[...]) if r_ref is not None else prod

            @pl.when(k > 0)
            def _():
                o_ref[...] += prod

    assert nk == 1 or out_sds.dtype == F32
    return pl.pallas_call(body, grid=grid, in_specs=in_specs, out_specs=out_spec, out_shape=out_sds, name=name,
                          compiler_params=_params(("parallel", "parallel", "arbitrary")))(*operands)


def _div_tile(n, want):
    best = None
    for t in range(128, min(n, want) + 1, 128):
        if n % t == 0:
            best = t
    assert best is not None, (n, want)
    return best


def mm_fwd(name, a, g, l, col, out_dtype=F32, res=None):
    m, k_dim = a.shape
    tm = min(MM_TILE_M, m)
    if col:
        _, _, kk, b = g.shape
        assert kk == k_dim
        tn = _div_tile(b, 1408)
        nps = b // tn
        n = 4 * b
        grid = (m // tm, n // tn, 1)
        in_specs = [pl.BlockSpec((tm, k_dim), lambda i, j, k: (i, 0)),
                    pl.BlockSpec((None, None, k_dim, tn), lambda i, j, k: (l, j // nps, 0, j % nps))]
        nk = 1
        w = g
    else:
        lyr, _, a_rows, n = g.shape
        assert 4 * a_rows == k_dim
        w = g.reshape(lyr, k_dim, n)
        tk = _div_tile(k_dim, 1408)
        tn = n
        nk = k_dim // tk
        grid = (m // tm, 1, nk)
        in_specs = [pl.BlockSpec((tm, tk), lambda i, j, k: (i, k)),
                    pl.BlockSpec((None, tk, tn), lambda i, j, k: (l, k, j))]
    ops = [a, w]
    if res is not None:
        in_specs.append(pl.BlockSpec((tm, tn), lambda i, j, k: (i, j)))
        ops.append(res)
    return _mm(name, (1, 0), grid, in_specs, pl.BlockSpec((tm, tn), lambda i, j, k: (i, j)), _sds((m, n), out_dtype),
               nk, ops, with_res=res is not None)


def mm_dx(name, dy, g, l, col):
    m, n = dy.shape
    tm = min(MM_TILE_M, m)
    if col:
        _, _, k_dim, b = g.shape
        assert 4 * b == n
        tk = _div_tile(b, 1408)
        nps = b // tk
        nk = n // tk
        grid = (m // tm, 1, nk)
        in_specs = [pl.BlockSpec((tm, tk), lambda i, j, k: (i, k)),
                    pl.BlockSpec((None, None, k_dim, tk), lambda i, j, k: (l, k // nps, 0, k % nps))]
        out_spec = pl.BlockSpec((tm, k_dim), lambda i, j, k: (i, 0))
        w = g
    else:
        lyr, _, a_rows, nn_ = g.shape
        assert nn_ == n
        k_dim = 4 * a_rows
        w = g.reshape(lyr, k_dim, n)
        tno = _div_tile(k_dim, 1408)
        nk = 1
        grid = (m // tm, k_dim // tno, 1)
        in_specs = [pl.BlockSpec((tm, n), lambda i, j, k: (i, 0)),
                    pl.BlockSpec((None, tno, n), lambda i, j, k: (l, j, 0))]
        out_spec = pl.BlockSpec((tm, tno), lambda i, j, k: (i, j))
    return _mm(name, (1, 1), grid, in_specs, out_spec, _sds((m, k_dim), F32), nk, [dy, w])


def mm_dw(name, a, dy, col, shard_cols=None):
    m, k_dim = a.shape
    _, n = dy.shape
    ts = min(MM_TILE_M, m)
    ns = m // ts
    tko = _div_tile(k_dim, 1408)
    if col:
        b = n // 4
        tn = _div_tile(b, 1408)
        nps = b // tn
        grid = (k_dim // tko, n // tn, ns)
        out_spec = pl.BlockSpec((None, tko, tn), lambda i, j, s: (j // nps, i, j % nps))
        out_sds = _sds((4, k_dim, b), F32)
    else:
        tn = n
        grid = (k_dim // tko, 1, ns)
        out_spec = pl.BlockSpec((tko, tn), lambda i, j, s: (i, j))
        out_sds = _sds((k_dim, n), F32)
    in_specs = [pl.BlockSpec((ts, tko), lambda i, j, s: (s, i)),
                pl.BlockSpec((ts, tn), lambda i, j, s: (s, j))]
    out = _mm(name, (0, 0), grid, in_specs, out_spec, out_sds, ns, [a, dy])
    if not col:
        out = out.reshape(4, k_dim // 4, n)
    return out


def _row_spec(t, c, col=0):
    return pl.BlockSpec((t, c), lambda i: (i, col))


def _vec_spec(c):
    return pl.BlockSpec((1, c), lambda i: (0, 0))


def _acc_rows(ref, i, val):
    @pl.when(i == 0)
    def _():
        ref[...] = val

    @pl.when(i > 0)
    def _():
        ref[...] += val


def rope_tables(positions, inv_freq):
    s = positions.shape[0]
    t = min(ROW_TILE, s)
    half = inv_freq.shape[1]

    def body(p_ref, f_ref, c_ref, s_ref):
        ang = p_ref[...].astype(F32) * f_ref[...]
        c_ref[...] = jnp.cos(ang)
        s_ref[...] = jnp.sin(ang)

    return pl.pallas_call(
        body, grid=(s // t,), in_specs=[_row_spec(t, 1), _vec_spec(half)],
        out_specs=[_row_spec(t, half), _row_spec(t, half)], out_shape=[_sds((s, half), F32)] * 2, name="rope_tables",
        compiler_params=_params(("parallel",)))(positions, inv_freq)


def rms_cast(h, g):
    s, d = h.shape
    t = min(ROW_TILE, s)

    def body(h_ref, g_ref, o_ref):
        x = h_ref[...]
        r = lax.rsqrt(jnp.mean(x * x, axis=-1, keepdims=True) + RMS_EPS)
        o_ref[...] = (x * r * g_ref[...]).astype(BF16)

    return pl.pallas_call(body, grid=(s // t,), in_specs=[_row_spec(t, d), _vec_spec(d)], out_specs=_row_spec(t, d),
                          out_shape=_sds((s, d), BF16), name="rms_cast", compiler_params=_params(("parallel",)))(h, g)


def _rms_bwd_math(x, g, du):
    r = lax.rsqrt(jnp.mean(x * x, axis=-1, keepdims=True) + RMS_EPS)
    gd = g * du
    dx = r * gd - x * (r * r * r) * jnp.mean(x * gd, axis=-1, keepdims=True)
    dg = jnp.sum(x * r * du, axis=0, keepdims=True)
    return dx, dg


def rms_bwd(h, g, du, dres=None):
    s, d = h.shape
    t = min(ROW_TILE, s)

    def body(*refs):
        if dres is None:
            h_ref, g_ref, du_ref, dh_ref, dg_ref = refs
        else:
            h_ref, g_ref, du_ref, dr_ref, dh_ref, dg_ref = refs
        dx, dg = _rms_bwd_math(h_ref[...], g_ref[...], du_ref[...])
        if dres is not None:
            dx = dx + dr_ref[...]
        dh_ref[...] = dx
        _acc_rows(dg_ref, pl.program_id(0), dg)

    in_specs = [_row_spec(t, d), _vec_spec(d), _row_spec(t, d)]
    ops = [h, g, du]
    if dres is not None:
        in_specs.append(_row_spec(t, d))
        ops.append(dres)
    return pl.pallas_call(body, grid=(s // t,), in_specs=in_specs, out_specs=[_row_spec(t, d), _vec_spec(d)],
                          out_shape=[_sds((s, d), F32), _sds((1, d), F32)], name="rms_bwd",
                          compiler_params=_params(("arbitrary",)))(*ops)


def loss_head(h, g, target):
    s, d = h.shape
    t = min(ROW_TILE, s)

    def body(h_ref, g_ref, t_ref, dh_ref, dg_ref, loss_ref):
        x = h_ref[...]
        gg = g_ref[...]
        r = lax.rsqrt(jnp.mean(x * x, axis=-1, keepdims=True) + RMS_EPS)
        err = x * r * gg - t_ref[...]
        part = 0.5 * jnp.sum(jnp.mean(err * err, axis=-1, keepdims=True), axis=0, keepdims=True)
        dy = err * (1.0 / d)
        dx, dg = _rms_bwd_math(x, gg, dy)
        dh_ref[...] = dx
        i = pl.program_id(0)
        _acc_rows(dg_ref, i, dg)
        _acc_rows(loss_ref, i, jnp.broadcast_to(part, (1, 128)))

    return pl.pallas_call(
        body, grid=(s // t,), in_specs=[_row_spec(t, d), _vec_spec(d), _row_spec(t, d)],
        out_specs=[_row_spec(t, d), _vec_spec(d), _vec_spec(128)],
        out_shape=[_sds((s, d), F32), _sds((1, d), F32), _sds((1, 128), F32)], name="loss_head",
        compiler_params=_params(("arbitrary",)))(h, g, target)


def _rot(x, cos, sin):
    half = x.shape[-1] // 2
    x1, x2 = x[:, :half], x[:, half:]
    return jnp.concatenate([x1 * cos - x2 * sin, x2 * cos + x1 * sin], axis=-1)


def _rot_t(dy, cos, sin):
    half = dy.shape[-1] // 2
    d1, d2 = dy[:, :half], dy[:, half:]
    return jnp.concatenate([d1 * cos + d2 * sin, d2 * cos - d1 * sin], axis=-1)


def _decay_tables():
    log_gamma = jnp.log(1.0 - jnp.power(2.0, -5.0 - jnp.arange(RET_HEADS, dtype=F32)))
    idx = jnp.arange(CHUNK, dtype=F32)
    dist = jnp.abs(idx[:, None] - idx[None, :])
    d_inner = jnp.exp(log_gamma[:, None, None] * dist)
    decay_q = jnp.exp(log_gamma[None, :] * (idx[:, None] + 1.0))
    decay_k = jnp.exp(log_gamma[None, :] * (CHUNK - 1.0 - idx[:, None]))
    decay_chunk = jnp.exp(log_gamma * CHUNK)
    return d_inner, decay_q.T[:, :, None], decay_k.T[:, :, None], decay_chunk[:, None, None]


_QK_SCALE = RET_QK_DIM ** -0.5


def retention_fwd(p, cos, sin, gn_g, tables):
    s = p.shape[0]
    t = min(RET_TILE, s)
    nc = t // CHUNK
    nt = s // t
    dk, dv = RET_QK_DIM, RET_V_DIM
    d_inner, decay_q, decay_k, decay_chunk = tables

    def body(q_ref, k_ref, v_ref, gr_ref, cos_ref, sin_ref, gn_ref, di_ref, dq_ref, dkk_ref, dc_ref,
             o_ref, z_ref, st_ref, state):
        i = pl.program_id(1)

        @pl.when(i == 0)
        def _():
            state[...] = jnp.zeros_like(state)

        cs, sn = cos_ref[...], sin_ref[...]
        qr = _rot(q_ref[...], cs, sn) * _QK_SCALE
        kr = _rot(k_ref[...], cs, sn)
        dmat, dq, dkk, gam = di_ref[...], dq_ref[...], dkk_ref[...], dc_ref[...]
        for c in range(nc):
            sl = slice(c * CHUNK, (c + 1) * CHUNK)
            qc = qr[sl].astype(BF16)
            kc = kr[sl].astype(BF16)
            vc = v_ref[sl, :].astype(BF16)
            scores = _nt(qc, kc) * dmat
            st = state[...].astype(BF16)
            st_ref[c] = st
            o = _nn(scores.astype(BF16), vc) + _nn(qc, st) * dq
            kd = (kr[sl] * dkk).astype(BF16)
            state[...] = state[...] * gam + _tn(kd, vc)
            o_ref[sl, :] = o
            mu = jnp.mean(o, axis=-1, keepdims=True)
            oc = o - mu
            var = jnp.mean(oc * oc, axis=-1, keepdims=True)
            y = oc * lax.rsqrt(var + LN_EPS) * gn_ref[...]
            gr = gr_ref[sl, :]
            z_ref[sl, :] = (gr * _sigmoid(gr) * y).astype(BF16)

    hmap = lambda h, i: (h, 0, 0)
    in_specs = [
        pl.BlockSpec((t, dk), lambda h, i: (i, h)),
        pl.BlockSpec((t, dk), lambda h, i: (i, RET_HEADS + h)),
        pl.BlockSpec((t, dv), lambda h, i: (i, 4 + h)),
        pl.BlockSpec((t, dv), lambda h, i: (i, 8 + h)),
        pl.BlockSpec((t, dk // 2), lambda h, i: (i, 0)),
        pl.BlockSpec((t, dk // 2), lambda h, i: (i, 0)),
        pl.BlockSpec((1, dv), lambda h, i: (0, h)),
        pl.BlockSpec((None, CHUNK, CHUNK), hmap),
        pl.BlockSpec((None, CHUNK, 1), hmap),
        pl.BlockSpec((None, CHUNK, 1), hmap),
        pl.BlockSpec((None, 1, 1), hmap),
    ]
    out_specs = [pl.BlockSpec((t, dv), lambda h, i: (i, h)),
                 pl.BlockSpec((t, dv), lambda h, i: (i, h)),
                 pl.BlockSpec((None, nc, dk, dv), lambda h, i: (h, i, 0, 0))]
    out_shape = [_sds((s, RET_HEADS * dv), F32), _sds((s, RET_HEADS * dv), BF16),
                 _sds((RET_HEADS, s // CHUNK, dk, dv), BF16)]
    return pl.pallas_call(
        body, grid=(RET_HEADS, nt), in_specs=in_specs, out_specs=out_specs, out_shape=out_shape,
        scratch_shapes=[pltpu.VMEM((dk, dv), F32)], name="retention_fwd",
        compiler_params=_params(("parallel", "arbitrary")))(p, p, p, p, cos, sin, gn_g, d_inner, decay_q, decay_k,
                                                            decay_chunk)


def gn_gate_bwd(o, p, gn_g, dz):
    s = o.shape[0]
    t = min(ROW_TILE, s)
    dv = RET_V_DIM
    w = RET_HEADS * dv

    def body(o_ref, gr_ref, gn_ref, dz_ref, do_ref, dgr_ref, dgn_ref):
        dgn_parts = []
        for h in range(RET_HEADS):
            sl = slice(h * dv, (h + 1) * dv)
            oo = o_ref[:, sl]
            gr = gr_ref[:, sl]
            dz = dz_ref[:, sl]
            gn = gn_ref[:, sl]
            mu = jnp.mean(oo, axis=-1, keepdims=True)
            oc = oo - mu
            rstd = lax.rsqrt(jnp.mean(oc * oc, axis=-1, keepdims=True) + LN_EPS)
            y = oc * rstd
            sg = _sigmoid(gr)
            act = gr * sg
            dyg = dz * act
            dgn_parts.append(jnp.sum(dyg * y, axis=0, keepdims=True))
            dy = dyg * gn
            do_ref[:, sl] = rstd * (dy - jnp.mean(dy, axis=-1, keepdims=True)
                                    - y * jnp.mean(dy * y, axis=-1, keepdims=True))
            dgr_ref[:, sl] = (dz * (y * gn) * (sg * (1.0 + gr * (1.0 - sg)))).astype(BF16)
        _acc_rows(dgn_ref, pl.program_id(0), jnp.concatenate(dgn_parts, axis=-1))

    return pl.pallas_call(
        body, grid=(s // t,), in_specs=[_row_spec(t, w), _row_spec(t, w, 2), _vec_spec(w), _row_spec(t, w)],
        out_specs=[_row_spec(t, w), _row_spec(t, w), _vec_spec(w)],
        out_shape=[_sds((s, w), F32), _sds((s, w), BF16), _sds((1, w), F32)], name="gn_gate_bwd",
        compiler_params=_params(("arbitrary",)))(o, p, gn_g, dz)


def retention_bwd(p, cos, sin, states, do, tables):
    s = p.shape[0]
    t = min(RET_TILE, s)
    nc = t // CHUNK
    nt = s // t
    dk, dv = RET_QK_DIM, RET_V_DIM
    d_inner, decay_q, decay_k, decay_chunk = tables

    def body(q_ref, k_ref, v_ref, cos_ref, sin_ref, st_ref, do_ref, di_ref, dq_ref, dkk_ref, dc_ref,
             gq_ref, gk_ref, gv_ref, dstate):
        i = pl.program_id(1)

        @pl.when(i == 0)
        def _():
            dstate[...] = jnp.zeros_like(dstate)

        cs, sn = cos_ref[...], sin_ref[...]
        qr = _rot(q_ref[...], cs, sn) * _QK_SCALE
        kr = _rot(k_ref[...], cs, sn)
        dmat, dq, dkk, gam = di_ref[...], dq_ref[...], dkk_ref[...], dc_ref[...]
        for c in range(nc - 1, -1, -1):
            sl = slice(c * CHUNK, (c + 1) * CHUNK)
            qc = qr[sl].astype(BF16)
            kc = kr[sl].astype(BF16)
            vc = v_ref[sl, :].astype(BF16)
            kd = (kr[sl] * dkk).astype(BF16)
            st = st_ref[c]
            d_o = do_ref[sl, :]
            dob = d_o.astype(BF16)
            ab = (_nt(qc, kc) * dmat).astype(BF16)
            dsb = dstate[...].astype(BF16)
            dvv = _tn(ab, dob) + _nn(kd, dsb)
            dkd = _nt(vc, dsb)
            dcb = (d_o * dq).astype(BF16)
            dpb = (_nt(dob, vc) * dmat).astype(BF16)
            dqq = _nt(dcb, st) + _nn(dpb, kc)
            dkv = _tn(dpb, qc) + dkd * dkk
            dstate[...] = dstate[...] * gam + _tn(qc, dcb)
            gq_ref[sl, :] = _rot_t(dqq * _QK_SCALE, cs[sl], sn[sl]).astype(BF16)
            gk_ref[sl, :] = _rot_t(dkv, cs[sl], sn[sl]).astype(BF16)
            gv_ref[sl, :] = dvv.astype(BF16)

    hmap = lambda h, i: (h, 0, 0)
    rev = lambda i: nt - 1 - i
    in_specs = [
        pl.BlockSpec((t, dk), lambda h, i: (rev(i), h)),
        pl.BlockSpec((t, dk), lambda h, i: (rev(i), RET_HEADS + h)),
        pl.BlockSpec((t, dv), lambda h, i: (rev(i), 4 + h)),
        pl.BlockSpec((t, dk // 2), lambda h, i: (rev(i), 0)),
        pl.BlockSpec((t, dk // 2), lambda h, i: (rev(i), 0)),
        pl.BlockSpec((None, nc, dk, dv), lambda h, i: (h, rev(i), 0, 0)),
        pl.BlockSpec((t, dv), lambda h, i: (rev(i), h)),
        pl.BlockSpec((None, CHUNK, CHUNK), hmap),
        pl.BlockSpec((None, CHUNK, 1), hmap),
        pl.BlockSpec((None, CHUNK, 1), hmap),
        pl.BlockSpec((None, 1, 1), hmap),
    ]
    out_specs = [pl.BlockSpec((t, dk), lambda h, i: (rev(i), h)),
                 pl.BlockSpec((t, dk), lambda h, i: (rev(i), h)),
                 pl.BlockSpec((t, dv), lambda h, i: (rev(i), h))]
    out_shape = [_sds((s, RET_HEADS * dk), BF16), _sds((s, RET_HEADS * dk), BF16), _sds((s, RET_HEADS * dv), BF16)]
    return pl.pallas_call(
        body, grid=(RET_HEADS, nt), in_specs=in_specs, out_specs=out_specs, out_shape=out_shape,
        scratch_shapes=[pltpu.VMEM((dk, dv), F32)], name="retention_bwd",
        compiler_params=_params(("parallel", "arbitrary")))(p, p, p, cos, sin, states, do, d_inner, decay_q, decay_k,
                                                            decay_chunk)


A_COL, B_COL = 6, 7


def _prev_rows_spec(t, halo, width, col):
    per = t // halo
    return pl.BlockSpec((halo, width), lambda i: (jnp.maximum(i * per - 1, 0), col))


def _next_rows_spec(t, halo, width, col, n_rows):
    per = t // halo
    last = n_rows // halo - 1
    return pl.BlockSpec((halo, width), lambda i: (jnp.minimum((i + 1) * per, last), col))


def conv_fwd(p, dw_w, dw_b, ln_g, ln_b):
    s = p.shape[0]
    t = min(ROW_TILE, s)
    c = D_MODEL
    hl = CONV_HALO

    def body(a_ref, b_ref, ah_ref, bh_ref, w_ref, wb_ref, g_ref, bb_ref, c1_ref, c3_ref, ext):
        i = pl.program_id(0)
        ext[pl.ds(0, hl), :] = jnp.where(i > 0, ah_ref[...] * _sigmoid(bh_ref[...]), 0.0)
        ext[pl.ds(hl, t), :] = a_ref[...] * _sigmoid(b_ref[...])
        acc = jnp.broadcast_to(wb_ref[...], (t, c))
        for j in range(CONV_WIDTH):
            acc = acc + w_ref[j:j + 1, :] * ext[pl.ds(hl - (CONV_WIDTH - 1) + j, t), :]
        c1_ref[...] = acc
        mu = jnp.mean(acc, axis=-1, keepdims=True)
        xc = acc - mu
        var = jnp.mean(xc * xc, axis=-1, keepdims=True)
        c2 = xc * lax.rsqrt(var + LN_EPS) * g_ref[...] + bb_ref[...]
        c3_ref[...] = (c2 * _sigmoid(c2)).astype(BF16)

    in_specs = [_row_spec(t, c, A_COL), _row_spec(t, c, B_COL),
                _prev_rows_spec(t, hl, c, A_COL), _prev_rows_spec(t, hl, c, B_COL),
                pl.BlockSpec((CONV_WIDTH, c), lambda i: (0, 0)), _vec_spec(c), _vec_spec(c), _vec_spec(c)]
    return pl.pallas_call(
        body, grid=(s // t,), in_specs=in_specs, out_specs=[_row_spec(t, c), _row_spec(t, c)],
        out_shape=[_sds((s, c), F32), _sds((s, c), BF16)], scratch_shapes=[pltpu.VMEM((t + hl, c), F32)],
        name="conv_fwd", compiler_params=_params(("parallel",)))(p, p, p, p, dw_w, dw_b, ln_g, ln_b)


def conv_ln_bwd(c1, ln_g, ln_b, dc3):
    s, c = c1.shape
    t = min(ROW_TILE, s)

    def body(c1_ref, g_ref, b_ref, d_ref, dc1_ref, dg_ref, db_ref):
        x = c1_ref[...]
        g = g_ref[...]
        mu = jnp.mean(x, axis=-1, keepdims=True)
        xc = x - mu
        rstd = lax.rsqrt(jnp.mean(xc * xc, axis=-1, keepdims=True) + LN_EPS)
        y = xc * rstd
        c2 = y * g + b_ref[...]
        sg = _sigmoid(c2)
        dc2 = d_ref[...] * (sg * (1.0 + c2 * (1.0 - sg)))
        i = pl.program_id(0)
        _acc_rows(db_ref, i, jnp.sum(dc2, axis=0, keepdims=True))
        _acc_rows(dg_ref, i, jnp.sum(dc2 * y, axis=0, keepdims=True))
        dy = dc2 * g
        dc1_ref[...] = rstd * (dy - jnp.mean(dy, axis=-1, keepdims=True)
                               - y * jnp.mean(dy * y, axis=-1, keepdims=True))

    return pl.pallas_call(
        body, grid=(s // t,), in_specs=[_row_spec(t, c), _vec_spec(c), _vec_spec(c), _row_spec(t, c)],
        out_specs=[_row_spec(t, c), _vec_spec(c), _vec_spec(c)],
        out_shape=[_sds((s, c), F32), _sds((1, c), F32), _sds((1, c), F32)], name="conv_ln_bwd",
        compiler_params=_params(("arbitrary",)))(c1, ln_g, ln_b, dc3)


def conv_dw_bwd(p, dc1, dw_w):
    s = p.shape[0]
    t = min(ROW_TILE, s)
    c = D_MODEL
    hl = CONV_HALO
    nt = s // t

    def body(a_ref, b_ref, ah_ref, bh_ref, d_ref, dn_ref, w_ref, dab_ref, dw_ref, dbias_ref, ext_c, ext_d):
        i = pl.program_id(0)
        a = a_ref[...]
        sb = _sigmoid(b_ref[...])
        ext_c[pl.ds(0, hl), :] = jnp.where(i > 0, ah_ref[...] * _sigmoid(bh_ref[...]), 0.0)
        ext_c[pl.ds(hl, t), :] = a * sb
        d = d_ref[...]
        ext_d[pl.ds(0, t), :] = d
        ext_d[pl.ds(t, hl), :] = jnp.where(i < nt - 1, dn_ref[...], 0.0)
        dc0 = jnp.zeros((t, c), F32)
        rows = []
        for j in range(CONV_WIDTH):
            dc0 = dc0 + w_ref[j:j + 1, :] * ext_d[pl.ds(CONV_WIDTH - 1 - j, t), :]
            rows.append(jnp.sum(d * ext_c[pl.ds(hl - (CONV_WIDTH - 1) + j, t), :], axis=0, keepdims=True))
        _acc_rows(dw_ref, i, jnp.concatenate(rows, axis=0))
        _acc_rows(dbias_ref, i, jnp.sum(d, axis=0, keepdims=True))
        dab_ref[:, :c] = (dc0 * sb).astype(BF16)
        dab_ref[:, c:] = (dc0 * a * sb * (1.0 - sb)).astype(BF16)

    in_specs = [_row_spec(t, c, A_COL), _row_spec(t, c, B_COL),
                _prev_rows_spec(t, hl, c, A_COL), _prev_rows_spec(t, hl, c, B_COL),
                _row_spec(t, c), _next_rows_spec(t, hl, c, 0, s),
                pl.BlockSpec((CONV_WIDTH, c), lambda i: (0, 0))]
    return pl.pallas_call(
        body, grid=(nt,), in_specs=in_specs,
        out_specs=[_row_spec(t, 2 * c), pl.BlockSpec((CONV_WIDTH, c), lambda i: (0, 0)), _vec_spec(c)],
        out_shape=[_sds((s, 2 * c), BF16), _sds((CONV_WIDTH, c), F32), _sds((1, c), F32)],
        scratch_shapes=[pltpu.VMEM((t + hl, c), F32), pltpu.VMEM((t + hl, c), F32)], name="conv_dw_bwd",
        compiler_params=_params(("arbitrary",)))(p, p, p, p, dc1, dc1, dw_w)


GATE_COL = 4


def gate_mix_fwd(p, b_gate, y_a, y_b, b_conv_out):
    s = p.shape[0]
    t = min(ROW_TILE, s)
    c = D_MODEL

    def body(gt_ref, bg_ref, ya_ref, yb_ref, bc_ref, o_ref):
        gs = _sigmoid(gt_ref[...] + bg_ref[...])
        o_ref[...] = (gs[:, :c] * ya_ref[...] + gs[:, c:] * (yb_ref[...] + bc_ref[...])).astype(BF16)

    return pl.pallas_call(
        body, grid=(s // t,),
        in_specs=[_row_spec(t, 2 * c, GATE_COL), _vec_spec(2 * c), _row_spec(t, c), _row_spec(t, c), _vec_spec(c)],
        out_specs=_row_spec(t, c), out_shape=_sds((s, c), BF16), name="gate_mix_fwd",
        compiler_params=_params(("parallel",)))(p, b_gate, y_a, y_b, b_conv_out)


def gate_mix_bwd(p, b_gate, y_a, y_b, b_conv_out, dmix):
    s = p.shape[0]
    t = min(ROW_TILE, s)
    c = D_MODEL

    def body(gt_ref, bg_ref, ya_ref, yb_ref, bc_ref, d_ref, dya_ref, dyb_ref, dgt_ref, dbg_ref, dbc_ref):
        gs = _sigmoid(gt_ref[...] + bg_ref[...])
        ga, gb = gs[:, :c], gs[:, c:]
        d = d_ref[...]
        dya = ga * d
        dyb = gb * d
        dya_ref[...] = dya.astype(BF16)
        dyb_ref[...] = dyb.astype(BF16)
        dga = d * ya_ref[...] * ga * (1.0 - ga)
        dgb = d * (yb_ref[...] + bc_ref[...]) * gb * (1.0 - gb)
        dgt_ref[:, :c] = dga.astype(BF16)
        dgt_ref[:, c:] = dgb.astype(BF16)
        i = pl.program_id(0)
        _acc_rows(dbg_ref, i, jnp.concatenate([jnp.sum(dga, axis=0, keepdims=True),
                                               jnp.sum(dgb, axis=0, keepdims=True)], axis=-1))
        _acc_rows(dbc_ref, i, jnp.sum(dyb, axis=0, keepdims=True))

    return pl.pallas_call(
        body, grid=(s // t,),
        in_specs=[_row_spec(t, 2 * c, GATE_COL), _vec_spec(2 * c), _row_spec(t, c), _row_spec(t, c), _vec_spec(c),
                  _row_spec(t, c)],
        out_specs=[_row_spec(t, c), _row_spec(t, c), _row_spec(t, 2 * c), _vec_spec(2 * c), _vec_spec(c)],
        out_shape=[_sds((s, c), BF16), _sds((s, c), BF16), _sds((s, 2 * c), BF16), _sds((1, 2 * c), F32),
                   _sds((1, c), F32)],
        name="gate_mix_bwd", compiler_params=_params(("arbitrary",)))(p, b_gate, y_a, y_b, b_conv_out, dmix)


_X_SCALE = X_HEAD_DIM ** -0.5


def _softmax_rows(sc):
    m = jnp.max(sc, axis=-1, keepdims=True)
    e = jnp.exp(sc - m)
    return e / jnp.sum(e, axis=-1, keepdims=True)


def attn_fwd(qx, kv):
    s, d = qx.shape
    m = kv.shape[0]
    t = min(ROW_TILE, s)
    hd = X_HEAD_DIM

    def body(q_ref, kv_ref, o_ref):
        for h in range(X_HEADS):
            sl = slice(h * hd, (h + 1) * hd)
            kh = kv_ref[:, sl].astype(BF16)
            vh = kv_ref[:, d + h * hd:d + (h + 1) * hd].astype(BF16)
            pr = _softmax_rows(_nt(q_ref[:, sl], kh) * _X_SCALE)
            o_ref[:, sl] = _nn(pr.astype(BF16), vh).astype(BF16)

    return pl.pallas_call(
        body, grid=(s // t,), in_specs=[_row_spec(t, d), pl.BlockSpec((m, 2 * d), lambda i: (0, 0))],
        out_specs=_row_spec(t, d), out_shape=_sds((s, d), BF16), name="attn_fwd",
        compiler_params=_params(("parallel",)))(qx, kv)


def attn_bwd(qx, kv, dox):
    s, d = qx.shape
    m = kv.shape[0]
    t = min(ROW_TILE, s)
    hd = X_HEAD_DIM

    def body(q_ref, kv_ref, do_ref, dq_ref, dkv_ref):
        dks, dvs = [], []
        for h in range(X_HEADS):
            sl = slice(h * hd, (h + 1) * hd)
            qh = q_ref[:, sl]
            kh = kv_ref[:, sl].astype(BF16)
            vh = kv_ref[:, d + h * hd:d + (h + 1) * hd].astype(BF16)
            pr = _softmax_rows(_nt(qh, kh) * _X_SCALE)
            doh = do_ref[:, sl].astype(BF16)
            dpr = _nt(doh, vh)
            dvs.append(_tn(pr.astype(BF16), doh))
            ds = pr * (dpr - jnp.sum(dpr * pr, axis=-1, keepdims=True))
            dsb = (ds * _X_SCALE).astype(BF16)
            dq_ref[:, sl] = _nn(dsb, kh).astype(BF16)
            dks.append(_tn(dsb, qh))
        _acc_rows(dkv_ref, pl.program_id(0), jnp.concatenate(dks + dvs, axis=-1))

    return pl.pallas_call(
        body, grid=(s // t,),
        in_specs=[_row_spec(t, d), pl.BlockSpec((m, 2 * d), lambda i: (0, 0)), _row_spec(t, d)],
        out_specs=[_row_spec(t, d), pl.BlockSpec((m, 2 * d), lambda i: (0, 0))],
        out_shape=[_sds((s, d), BF16), _sds((m, 2 * d), F32)], name="attn_bwd",
        compiler_params=_params(("arbitrary",)))(qx, kv, dox)


def ffn_act_fwd(up, dw_w, dw_b):
    s = up.shape[0]
    f = FFN_DIM
    t = min(ROW_TILE, s)
    hl = FFN_HALO

    def body(val_ref, gt_ref, gh_ref, w_ref, b_ref, o_ref, ext):
        i = pl.program_id(0)
        ext[pl.ds(0, hl), :] = jnp.where(i > 0, gh_ref[...], 0.0)
        ext[pl.ds(hl, t), :] = gt_ref[...]
        gc = b_ref[...] + w_ref[0:1, :] * ext[pl.ds(hl - 2, t), :] + w_ref[1:2, :] * ext[pl.ds(hl - 1, t), :] \
            + w_ref[2:3, :] * ext[pl.ds(hl, t), :]
        o_ref[...] = (gc * _sigmoid(gc) * val_ref[...]).astype(BF16)

    return pl.pallas_call(
        body, grid=(s // t,),
        in_specs=[_row_spec(t, f, 0), _row_spec(t, f, 1), _prev_rows_spec(t, hl, f, 1),
                  pl.BlockSpec((3, f), lambda i: (0, 0)), _vec_spec(f)],
        out_specs=_row_spec(t, f), out_shape=_sds((s, f), BF16), scratch_shapes=[pltpu.VMEM((t + hl, f), F32)],
        name="ffn_act_fwd", compiler_params=_params(("parallel",)))(up, up, up, dw_w, dw_b)


def ffn_act_bwd(up, dw_w, dw_b, da):
    s = up.shape[0]
    f = FFN_DIM
    t = min(ROW_TILE, s)
    hl = FFN_HALO
    nt = s // t

    def body(val_ref, valn_ref, gt_ref, gp_ref, gn_ref, da_ref, dan_ref, w_ref, b_ref,
             dup_ref, dw_ref, db_ref, ext_g, ext_d):
        i = pl.program_id(0)
        w0, w1, w2 = w_ref[0:1, :], w_ref[1:2, :], w_ref[2:3, :]
        ext_g[pl.ds(0, hl), :] = jnp.where(i > 0, gp_ref[...], 0.0)
        ext_g[pl.ds(hl, t), :] = gt_ref[...]
        ext_g[pl.ds(hl + t, hl), :] = gn_ref[...]

        def dgc_of(rows, off, val, da_rows):
            gc = b_ref[...] + w0 * ext_g[pl.ds(off + hl - 2, rows), :] + w1 * ext_g[pl.ds(off + hl - 1, rows), :] \
                + w2 * ext_g[pl.ds(off + hl, rows), :]
            sg = _sigmoid(gc)
            return da_rows * val * (sg * (1.0 + gc * (1.0 - sg))), gc * sg

        da_t = da_ref[...]
        dgc, act = dgc_of(t, 0, val_ref[...], da_t)
        dgc_n, _ = dgc_of(hl, t, valn_ref[...], dan_ref[...])
        ext_d[pl.ds(0, t), :] = dgc
        ext_d[pl.ds(t, hl), :] = jnp.where(i < nt - 1, dgc_n, 0.0)
        dup_ref[:, :f] = (da_t * act).astype(BF16)
        dup_ref[:, f:] = (w2 * dgc + w1 * ext_d[pl.ds(1, t), :] + w0 * ext_d[pl.ds(2, t), :]).astype(BF16)
        rows = [jnp.sum(dgc * ext_g[pl.ds(hl - 2 + k, t), :], axis=0, keepdims=True) for k in range(3)]
        _acc_rows(dw_ref, i, jnp.concatenate(rows, axis=0))
        _acc_rows(db_ref, i, jnp.sum(dgc, axis=0, keepdims=True))

    in_specs = [_row_spec(t, f, 0), _next_rows_spec(t, hl, f, 0, s),
                _row_spec(t, f, 1), _prev_rows_spec(t, hl, f, 1), _next_rows_spec(t, hl, f, 1, s),
                _row_spec(t, f), _next_rows_spec(t, hl, f, 0, s),
                pl.BlockSpec((3, f), lambda i: (0, 0)), _vec_spec(f)]
    return pl.pallas_call(
        body, grid=(nt,), in_specs=in_specs,
        out_specs=[_row_spec(t, 2 * f), pl.BlockSpec((3, f), lambda i: (0, 0)), _vec_spec(f)],
        out_shape=[_sds((s, 2 * f), BF16), _sds((3, f), F32), _sds((1, f), F32)],
        scratch_shapes=[pltpu.VMEM((t + 2 * hl, f), F32), pltpu.VMEM((t + hl, f), F32)], name="ffn_act_bwd",
        compiler_params=_params(("arbitrary",)))(up, up, up, up, up, da, da, dw_w, dw_b)


def local_step(x, mem, positions, target, gw, sp):
    n_layers = gw["w_in"].shape[0]
    inv_freq = 1.0 / (ROPE_THETA ** (jnp.arange(0, RET_QK_DIM, 2, dtype=F32) / RET_QK_DIM))
    cos, sin = rope_tables(positions, inv_freq[None, :])
    tables = _decay_tables()
    row = lambda name, l: sp[name][l][None, :]

    saved = []
    h = x
    for l in range(n_layers):
        a = {"h0": h}
        a["u"] = rms_cast(h, row("norm_mix_g", l))
        a["p"] = mm_fwd("mm_in", a["u"], gw["w_in"], l, True)
        a["o"], a["z"], a["states"] = retention_fwd(a["p"], cos, sin, row("ret_gn_g", l), tables)
        a["y_a"] = mm_fwd("mm_ret_out", a["z"], gw["w_ret_out"], l, False)
        a["c1"], a["c3"] = conv_fwd(a["p"], sp["conv_dw_w"][l], row("conv_dw_b", l), row("conv_ln_g", l),
                                    row("conv_ln_b", l))
        a["y_b"] = mm_fwd("mm_conv_out", a["c3"], gw["w_conv_out"], l, False)
        a["mixed"] = gate_mix_fwd(a["p"], row("b_gate", l), a["y_a"], a["y_b"], row("b_conv_out", l))
        a["h1"] = mm_fwd("mm_mix_out", a["mixed"], gw["w_mix_out"], l, False, res=h)
        a["hx"] = rms_cast(a["h1"], row("norm_xattn_g", l))
        a["qx"] = mm_fwd("mm_xq", a["hx"], gw["w_xq"], l, False, out_dtype=BF16)
        a["mem_n"] = rms_cast(mem, row("norm_mem_g", l))
        a["kv"] = mm_fwd("mm_xkv", a["mem_n"], gw["w_xkv"], l, True)
        a["ox"] = attn_fwd(a["qx"], a["kv"])
        a["h2"] = mm_fwd("mm_xo", a["ox"], gw["w_xo"], l, False, res=a["h1"])
        a["hf"] = rms_cast(a["h2"], row("norm_ffn_g", l))
        a["up"] = mm_fwd("mm_up", a["hf"], gw["w_up"], l, True)
        a["act"] = ffn_act_fwd(a["up"], sp["ffn_dw_w"][l], row("ffn_dw_b", l))
        h = mm_fwd("mm_down", a["act"], gw["w_down"], l, False, res=a["h2"])
        saved.append(a)

    dh, d_final_g, loss = loss_head(h, sp["norm_final_g"][None, :], target)

    big = {n: [None] * n_layers for n in BIG}
    small = {n: [None] * n_layers for n in SMALL_REPL + SMALL_SHARDED if n != "norm_final_g"}
    for l in range(n_layers - 1, -1, -1):
        a = saved[l]
        d_act = mm_dx("mm_down_dx", dh, gw["w_down"], l, False)
        big["w_down"][l] = mm_dw("mm_down_dw", a["act"], dh, False)
        d_up, small["ffn_dw_w"][l], small["ffn_dw_b"][l] = ffn_act_bwd(a["up"], sp["ffn_dw_w"][l],
                                                                        row("ffn_dw_b", l), d_act)
        d_hf = mm_dx("mm_up_dx", d_up, gw["w_up"], l, True)
        big["w_up"][l] = mm_dw("mm_up_dw", a["hf"], d_up, True)
        dh, small["norm_ffn_g"][l] = rms_bwd(a["h2"], row("norm_ffn_g", l), d_hf, dh)
        d_ox = mm_dx("mm_xo_dx", dh, gw["w_xo"], l, False)
        big["w_xo"][l] = mm_dw("mm_xo_dw", a["ox"], dh, False)
        d_qx, d_kv = attn_bwd(a["qx"], a["kv"], d_ox)
        d_hx = mm_dx("mm_xq_dx", d_qx, gw["w_xq"], l, False)
        big["w_xq"][l] = mm_dw("mm_xq_dw", a["hx"], d_qx, False)
        d_mem_n = mm_dx("mm_xkv_dx", d_kv, gw["w_xkv"], l, True)
        big["w_xkv"][l] = mm_dw("mm_xkv_dw", a["mem_n"], d_kv, True)
        _, small["norm_mem_g"][l] = rms_bwd(mem, row("norm_mem_g", l), d_mem_n)
        dh, small["norm_xattn_g"][l] = rms_bwd(a["h1"], row("norm_xattn_g", l), d_hx, dh)
        d_mixed = mm_dx("mm_mix_out_dx", dh, gw["w_mix_out"], l, False)
        big["w_mix_out"][l] = mm_dw("mm_mix_out_dw", a["mixed"], dh, False)
        d_ya, d_yb, dp_gate, small["b_gate"][l], small["b_conv_out"][l] = gate_mix_bwd(
            a["p"], row("b_gate", l), a["y_a"], a["y_b"], row("b_conv_out", l), d_mixed)
        d_c3 = mm_dx("mm_conv_out_dx", d_yb, gw["w_conv_out"], l, False)
        big["w_conv_out"][l] = mm_dw("mm_conv_out_dw", a["c3"], d_yb, False)
        d_c1, small["conv_ln_g"][l], small["conv_ln_b"][l] = conv_ln_bwd(a["c1"], row("conv_ln_g", l),
                                                                         row("conv_ln_b", l), d_c3)
        dp_conv, small["conv_dw_w"][l], small["conv_dw_b"][l] = conv_dw_bwd(a["p"], d_c1, sp["conv_dw_w"][l])
        d_z = mm_dx("mm_ret_out_dx", d_ya, gw["w_ret_out"], l, False)
        big["w_ret_out"][l] = mm_dw("mm_ret_out_dw", a["z"], d_ya, False)
        d_o, dp_gret, small["ret_gn_g"][l] = gn_gate_bwd(a["o"], a["p"], row("ret_gn_g", l), d_z)
        dp_q, dp_k, dp_v = retention_bwd(a["p"], cos, sin, a["states"], d_o, tables)
        dp = jnp.concatenate([dp_q, dp_k, dp_v, dp_gret, dp_conv, dp_gate], axis=1)
        d_u = mm_dx("mm_in_dx", dp, gw["w_in"], l, True)
        big["w_in"][l] = mm_dw("mm_in_dw", a["u"], dp, True)
        dh, small["norm_mix_g"][l] = rms_bwd(a["h0"], row("norm_mix_g", l), d_u, dh)

    big = {n: jnp.stack(v) for n, v in big.items()}
    small = {n: jnp.stack([g.reshape(sp[n].shape[1:]) for g in v]) for n, v in small.items()}
    small["norm_final_g"] = d_final_g.reshape(-1)
    return loss, dh, big, small


_ANY = pl.BlockSpec(memory_space=pl.ANY)


def _place():
    x, y, c = lax.axis_index("x"), lax.axis_index("y"), lax.axis_index("c")
    return x, y, c


def _other_chips(x, y):
    return [(1 - x, y), (x, 1 - y), (1 - x, 1 - y)]


def gather_weights(shards):
    n = len(shards)

    def body(*refs):
        ins, outs = refs[:n], refs[n:2 * n]
        send_sems, recv_sems, local_sems = refs[2 * n:]
        x, y, c = _place()
        mine = 2 * x + y
        chips = _other_chips(x, y)
        copies = []
        for w in range(n):
            loc = pltpu.make_async_copy(ins[w], outs[w].at[:, mine], local_sems.at[w])
            loc.start()
            copies.append(loc)
            for j, (px, py) in enumerate(chips):
                cp = pltpu.make_async_remote_copy(
                    src_ref=ins[w], dst_ref=outs[w].at[:, mine], send_sem=send_sems.at[w, j],
                    recv_sem=recv_sems.at[w, j], device_id=(px, py, c), device_id_type=MESH)
                cp.start()
        for w in range(n):
            copies[w].wait()
            for j, (px, py) in enumerate(chips):
                cp = pltpu.make_async_remote_copy(
                    src_ref=ins[w], dst_ref=outs[w].at[:, 2 * px + py], send_sem=send_sems.at[w, j],
                    recv_sem=recv_sems.at[w, j], device_id=(px, py, c), device_id_type=MESH)
                cp.wait_send()
                cp.wait_recv()

    out_shape = [_sds((s.shape[0], N_CHIPS) + s.shape[1:], s.dtype) for s in shards]
    return pl.pallas_call(
        body, in_specs=[_ANY] * n, out_specs=[_ANY] * n, out_shape=out_shape,
        scratch_shapes=[pltpu.SemaphoreType.DMA((n, 3)), pltpu.SemaphoreType.DMA((n, 3)),
                        pltpu.SemaphoreType.DMA((n,))],
        name="gather_weights", compiler_params=_params())(*shards)


def pair_exchange(grads):
    n = len(grads)

    def body(*refs):
        ins, outs = refs[:n], refs[n:2 * n]
        send_sems, recv_sems = refs[2 * n:]
        x, y, c = _place()
        cps = []
        for w in range(n):
            lh = ins[w].shape[0] // 2
            cp = pltpu.make_async_remote_copy(
                src_ref=ins[w].at[pl.ds((1 - c) * lh, lh)], dst_ref=outs[w], send_sem=send_sems.at[w],
                recv_sem=recv_sems.at[w], device_id=(x, y, 1 - c), device_id_type=MESH)
            cp.start()
            cps.append(cp)
        for cp in cps:
            cp.wait_send()
            cp.wait_recv()

    out_shape = [_sds((g.shape[0] // 2,) + g.shape[1:], g.dtype) for g in grads]
    return pl.pallas_call(
        body, in_specs=[_ANY] * n, out_specs=[_ANY] * n, out_shape=out_shape,
        scratch_shapes=[pltpu.SemaphoreType.DMA((n,)), pltpu.SemaphoreType.DMA((n,))],
        name="pair_exchange", compiler_params=_params())(*grads)


def chip_exchange(parts):
    n = len(parts)

    def body(*refs):
        ins, outs = refs[:n], refs[n:2 * n]
        send_sems, recv_sems, local_sems = refs[2 * n:]
        x, y, c = _place()
        mine = 2 * x + y
        chips = _other_chips(x, y)
        locs = []
        for w in range(n):
            loc = pltpu.make_async_copy(ins[w].at[:, mine], outs[w].at[mine], local_sems.at[w])
            loc.start()
            locs.append(loc)
            for j, (px, py) in enumerate(chips):
                pltpu.make_async_remote_copy(
                    src_ref=ins[w].at[:, 2 * px + py], dst_ref=outs[w].at[mine], send_sem=send_sems.at[w, j],
                    recv_sem=recv_sems.at[w, j], device_id=(px, py, c), device_id_type=MESH).start()
        for w in range(n):
            locs[w].wait()
            for j, (px, py) in enumerate(chips):
                cp = pltpu.make_async_remote_copy(
                    src_ref=ins[w].at[:, 2 * px + py], dst_ref=outs[w].at[2 * px + py], send_sem=send_sems.at[w, j],
                    recv_sem=recv_sems.at[w, j], device_id=(px, py, c), device_id_type=MESH)
                cp.wait_send()
                cp.wait_recv()

    out_shape = [_sds((N_CHIPS, g.shape[0]) + g.shape[2:], g.dtype) for g in parts]
    return pl.pallas_call(
        body, in_specs=[_ANY] * n, out_specs=[_ANY] * n, out_shape=out_shape,
        scratch_shapes=[pltpu.SemaphoreType.DMA((n, 3)), pltpu.SemaphoreType.DMA((n, 3)),
                        pltpu.SemaphoreType.DMA((n,))],
        name="chip_exchange", compiler_params=_params())(*parts)


def pair_share(halves):
    n = len(halves)

    def body(*refs):
        ins, outs = refs[:n], refs[n:2 * n]
        send_sems, recv_sems, local_sems = refs[2 * n:]
        x, y, c = _place()
        locs, cps = [], []
        for w in range(n):
            lh = ins[w].shape[0]
            loc = pltpu.make_async_copy(ins[w], outs[w].at[pl.ds(c * lh, lh)], local_sems.at[w])
            loc.start()
            locs.append(loc)
            cp = pltpu.make_async_remote_copy(
                src_ref=ins[w], dst_ref=outs[w].at[pl.ds(c * lh, lh)], send_sem=send_sems.at[w],
                recv_sem=recv_sems.at[w], device_id=(x, y, 1 - c), device_id_type=MESH)
            cp.start()
            cps.append(cp)
        for w in range(n):
            lh = ins[w].shape[0]
            locs[w].wait()
            cps[w].wait_send()
            pltpu.make_async_remote_copy(
                src_ref=ins[w], dst_ref=outs[w].at[pl.ds((1 - c) * lh, lh)], send_sem=send_sems.at[w],
                recv_sem=recv_sems.at[w], device_id=(x, y, 1 - c), device_id_type=MESH).wait_recv()

    out_shape = [_sds((2 * g.shape[0],) + g.shape[1:], g.dtype) for g in halves]
    return pl.pallas_call(
        body, in_specs=[_ANY] * n, out_specs=[_ANY] * n, out_shape=out_shape,
        scratch_shapes=[pltpu.SemaphoreType.DMA((n,)), pltpu.SemaphoreType.DMA((n,)), pltpu.SemaphoreType.DMA((n,))],
        name="pair_share", compiler_params=_params())(*halves)


def all_reduce_small(vec):
    r, lanes = vec.shape

    def body(v_ref, o_ref, buf, send_sems, recv_sems):
        x, y, c = _place()
        me = 4 * x + 2 * y + c
        buf[me] = v_ref[...]
        cps = []
        for k in range(1, N_DEV):
            peer = (me + k) % N_DEV
            cp = pltpu.make_async_remote_copy(
                src_ref=v_ref, dst_ref=buf.at[me], send_sem=send_sems.at[k - 1], recv_sem=recv_sems.at[k - 1],
                device_id=(peer // 4, (peer // 2) % 2, peer % 2), device_id_type=MESH)
            cp.start()
            cps.append(cp)
        for k in range(1, N_DEV):
            src = (me + N_DEV - k) % N_DEV
            cps[k - 1].wait_send()
            pltpu.make_async_remote_copy(
                src_ref=v_ref, dst_ref=buf.at[src], send_sem=send_sems.at[k - 1], recv_sem=recv_sems.at[k - 1],
                device_id=(src // 4, (src // 2) % 2, src % 2), device_id_type=MESH).wait_recv()
        acc = buf[0]
        for d in range(1, N_DEV):
            acc = acc + buf[d]
        o_ref[...] = acc

    vm = pl.BlockSpec(memory_space=pltpu.VMEM)
    return pl.pallas_call(
        body, in_specs=[vm], out_specs=vm, out_shape=_sds((r, lanes), F32),
        scratch_shapes=[pltpu.VMEM((N_DEV, r, lanes), F32), pltpu.SemaphoreType.DMA((N_DEV - 1,)),
                        pltpu.SemaphoreType.DMA((N_DEV - 1,))],
        name="all_reduce_small", compiler_params=_params())(vec)


ELEMENTWISE_BLOCK_BYTES = 1 << 20


def _flat_tile(rows, cols):
    for t in (512, 256, 128, 64, 32, 16, 8):
        if rows % t == 0 and t * cols * 4 <= ELEMENTWISE_BLOCK_BYTES:
            return t
    return rows


def add_pair(g, r, half):
    lyr, _, a, b = g.shape
    lh = lyr // 2
    rows = lh * 4 * a
    t = _flat_tile(rows, b)
    nb = rows // t
    g2 = g.reshape(lyr * 4 * a, b)
    r2 = r.reshape(rows, b)

    def body(half_ref, g_ref, r_ref, o_ref):
        o_ref[...] = g_ref[...] + r_ref[...]

    grid_spec = pltpu.PrefetchScalarGridSpec(
        num_scalar_prefetch=1, grid=(nb,),
        in_specs=[pl.BlockSpec((t, b), lambda i, hr: (hr[0] * nb + i, 0)), pl.BlockSpec((t, b), lambda i, hr: (i, 0))],
        out_specs=pl.BlockSpec((t, b), lambda i, hr: (i, 0)))
    out = pl.pallas_call(body, grid_spec=grid_spec, out_shape=_sds((rows, b), F32), name="add_pair",
                         compiler_params=_params(("parallel",)))(half, g2, r2)
    return out.reshape(lh, 4, a, b)


def sum_chips(parts):
    _, lh, a, b = parts.shape
    rows = lh * a
    t = _flat_tile(rows, b)
    p3 = parts.reshape(N_CHIPS, rows, b)

    def body(p_ref, o_ref):
        o_ref[...] = ((p_ref[0] + p_ref[1]) + p_ref[2]) + p_ref[3]

    out = pl.pallas_call(
        body, grid=(rows // t,), in_specs=[pl.BlockSpec((N_CHIPS, t, b), lambda i: (0, i, 0))],
        out_specs=pl.BlockSpec((t, b), lambda i: (i, 0)), out_shape=_sds((rows, b), F32), name="sum_chips",
        compiler_params=_params(("parallel",)))(p3)
    return out.reshape(lh, a, b)


def adamw(w, g, m, v):
    shape = w.shape
    c = shape[-1]
    rows = int(np.prod(shape[:-1])) if len(shape) > 1 else 1
    t = _flat_tile(rows, c)
    flat = lambda z: z.reshape(rows, c)

    def body(w_ref, g_ref, m_ref, v_ref, d_ref, nm_ref, nv_ref):
        gg = g_ref[...]
        mm = ADAM_B1 * m_ref[...] + (1.0 - ADAM_B1) * gg
        vv = ADAM_B2 * v_ref[...] + (1.0 - ADAM_B2) * jnp.square(gg)
        m_hat = mm / (1.0 - ADAM_B1 ** ADAM_STEP)
        v_hat = vv / (1.0 - ADAM_B2 ** ADAM_STEP)
        d_ref[...] = -ADAM_LR * (m_hat / (jnp.sqrt(v_hat) + ADAM_EPS) + ADAM_WD * w_ref[...])
        nm_ref[...] = mm
        nv_ref[...] = vv

    spec = pl.BlockSpec((t, c), lambda i: (i, 0))
    outs = pl.pallas_call(body, grid=(rows // t,), in_specs=[spec] * 4, out_specs=[spec] * 3,
                          out_shape=[_sds((rows, c), F32)] * 3, name="adamw",
                          compiler_params=_params(("parallel",)))(flat(w), flat(g), flat(m), flat(v))
    return tuple(o.reshape(shape) for o in outs)


def _pack(parts):
    flat = jnp.concatenate([p.reshape(-1) for p in parts])
    pad = (-flat.shape[0]) % 1024
    return jnp.pad(flat, (0, pad)).reshape(-1, 128)


def _unpack(packed, shapes):
    flat = packed.reshape(-1)
    out, off = [], 0
    for shp in shapes:
        size = int(np.prod(shp))
        out.append(flat[off:off + size].reshape(shp))
        off += size
    return out


def kernel(x, mem, positions, norm_mix_g, w_in, b_gate, ret_gn_g, w_ret_out, conv_dw_w, conv_dw_b, conv_ln_g, conv_ln_b, w_conv_out, b_conv_out, w_mix_out, norm_xattn_g, norm_mem_g, w_xq, w_xkv, w_xo, norm_ffn_g, w_up, ffn_dw_w, ffn_dw_b, w_down, norm_final_g, loss_target, m_norm_mix_g, m_w_in, m_b_gate, m_ret_gn_g, m_w_ret_out, m_conv_dw_w, m_conv_dw_b, m_conv_ln_g, m_conv_ln_b, m_w_conv_out, m_b_conv_out, m_w_mix_out, m_norm_xattn_g, m_norm_mem_g, m_w_xq, m_w_xkv, m_w_xo, m_norm_ffn_g, m_w_up, m_ffn_dw_w, m_ffn_dw_b, m_w_down, m_norm_final_g, v_norm_mix_g, v_w_in, v_b_gate, v_ret_gn_g, v_w_ret_out, v_conv_dw_w, v_conv_dw_b, v_conv_ln_g, v_conv_ln_b, v_w_conv_out, v_b_conv_out, v_w_mix_out, v_norm_xattn_g, v_norm_mem_g, v_w_xq, v_w_xkv, v_w_xo, v_norm_ffn_g, v_w_up, v_ffn_dw_w, v_ffn_dw_b, v_w_down, v_norm_final_g):
    args = locals()
    w = {n: args[n] for n in WEIGHTS}
    m = {n: args["m_" + n] for n in WEIGHTS}
    v = {n: args["v_" + n] for n in WEIGHTS}
    chip = 2 * lax.axis_index("x") + lax.axis_index("y")
    core = lax.axis_index("c")

    gathered = gather_weights([w[n].astype(BF16) for n in BIG])
    gw = dict(zip(BIG, gathered))

    sp = {n: w[n] for n in SMALL_REPL}
    placed = []
    for n in SMALL_SHARDED:
        cols = w[n].shape[-1]
        full = jnp.zeros(w[n].shape[:-1] + (N_CHIPS * cols,), F32)
        placed.append(lax.dynamic_update_slice_in_dim(full, w[n], chip * cols, axis=2))
    placed_shapes = [p.shape for p in placed]
    gathered_small = all_reduce_small(_pack([jnp.where(core == 0, p, 0.0) for p in placed]))
    for n, arr in zip(SMALL_SHARDED, _unpack(gathered_small, placed_shapes)):
        sp[n] = arr

    loss, grad_x, big, small = local_step(x[0], mem[0], positions.reshape(-1, 1), loss_target[0], gw, sp)

    names = [n for n in SMALL_REPL + SMALL_SHARDED]
    shapes = [small[n].shape for n in names] + [(128,)]
    reduced = _unpack(all_reduce_small(_pack([small[n] for n in names] + [loss.reshape(-1)])), shapes)
    grads = dict(zip(names, reduced[:-1]))
    loss_out = reduced[-1][0]
    for n in SMALL_SHARDED:
        cols = w[n].shape[-1]
        grads[n] = lax.dynamic_slice_in_dim(grads[n], chip * cols, cols, axis=2)

    blist = [big[n] for n in BIG]
    from_pair = pair_exchange(blist)
    half = core.reshape(1).astype(jnp.int32)
    pair_sum = [add_pair(g, r, half) for g, r in zip(blist, from_pair)]
    from_chips = chip_exchange(pair_sum)
    halves = [sum_chips(p) for p in from_chips]
    for n, g in zip(BIG, pair_share(halves)):
        grads[n] = g

    delta, new_m, new_v = {}, {}, {}
    for n in WEIGHTS:
        delta[n], new_m[n], new_v[n] = adamw(w[n], grads[n], m[n], v[n])
    return (loss_out, grad_x[None], *[grads[n] for n in WEIGHTS], *[delta[n] for n in WEIGHTS],
            *[new_m[n] for n in WEIGHTS], *[new_v[n] for n in WEIGHTS])
```

```python
import functools

import jax
import jax.numpy as jnp
import numpy as np
from jax import lax
from jax.experimental import pallas as pl
from jax.experimental.pallas import tpu as pltpu

F32 = jnp.float32
BF16 = jnp.bfloat16
MESH = pl.DeviceIdType.MESH

D_MODEL = 1024
CHUNK = 64
RET_HEADS = 4
RET_QK_DIM = 256
RET_V_DIM = 512
ROPE_THETA = 10000.0
CONV_WIDTH = 31
X_HEADS = 4
X_HEAD_DIM = 256
FFN_DIM = 2816
RMS_EPS = 1e-6
LN_EPS = 1e-5
ADAM_LR = 0.001
ADAM_B1 = 0.9
ADAM_B2 = 0.999
ADAM_EPS = 1e-08
ADAM_WD = 0.01
ADAM_STEP = 10

N_CHIPS = 4
N_DEV = 8
CONV_HALO = 32
FFN_HALO = 8
V7X_VMEM_LIMIT = 56 * 1024 * 1024
ROW_TILE = 256
MM_TILE_M = 1024
RET_TILE = 512

BIG = ("w_in", "w_ret_out", "w_conv_out", "w_mix_out", "w_xq", "w_xkv", "w_xo", "w_up", "w_down")
COL_SHARDED = ("w_in", "w_xkv", "w_up")
SMALL_REPL = ("norm_mix_g", "b_gate", "ret_gn_g", "conv_dw_b", "conv_ln_g", "conv_ln_b", "b_conv_out",
              "norm_xattn_g", "norm_mem_g", "norm_ffn_g", "ffn_dw_b", "norm_final_g")
SMALL_SHARDED = ("conv_dw_w", "ffn_dw_w")
WEIGHTS = ('norm_mix_g', 'w_in', 'b_gate', 'ret_gn_g', 'w_ret_out', 'conv_dw_w', 'conv_dw_b', 'conv_ln_g',
           'conv_ln_b', 'w_conv_out', 'b_conv_out', 'w_mix_out', 'norm_xattn_g', 'norm_mem_g', 'w_xq', 'w_xkv',
           'w_xo', 'norm_ffn_g', 'w_up', 'ffn_dw_w', 'ffn_dw_b', 'w_down', 'norm_final_g')


def _params(sem=None):
    return pltpu.CompilerParams(dimension_semantics=sem, vmem_limit_bytes=V7X_VMEM_LIMIT)


def _sds(shape, dtype):
    return jax.ShapeDtypeStruct(tuple(shape), dtype)


def _sigmoid(x):
    return jax.nn.sigmoid(x)


def _dot(a, b, ca, cb):
    return lax.dot_general(a, b, (((ca,), (cb,)), ((), ())), preferred_element_type=F32)


def _nn(a, b):
    return _dot(a, b, 1, 0)


def _nt(a, b):
    return _dot(a, b, 1, 1)


def _tn(a, b):
    return _dot(a, b, 0, 0)


def _mm(name, dims, grid, in_specs, out_spec, out_sds, nk, operands, with_res=False):
    def body(*refs):
        if with_res:
            a_ref, b_ref, r_ref, o_ref = refs
        else:
            a_ref, b_ref, o_ref = refs
            r_ref = None
        prod = _dot(a_ref[...].astype(BF16), b_ref[...].astype(BF16), *dims)
        if nk == 1:
            if r_ref is not None:
                prod = prod + r_ref[...]
            o_ref[...] = prod.astype(o_ref.dtype)
        else:
            k = pl.program_id(2)

            @pl.when(k == 0)
            def _():
                o_ref[...] = (prod + r_ref[...]) if r_ref is not None else prod

            @pl.when(k > 0)
            def _():
                o_ref[...] += prod

    assert nk == 1 or out_sds.dtype == F32
    return pl.pallas_call(body, grid=grid, in_specs=in_specs, out_specs=out_spec, out_shape=out_sds, name=name,
                          compiler_params=_params(("parallel", "parallel", "arbitrary")))(*operands)


def _div_tile(n, want):
    best = None
    for t in range(128, min(n, want) + 1, 128):
        if n % t == 0:
            best = t
    assert best is not None, (n, want)
    return best


def mm_fwd(name, a, g, l, col, out_dtype=F32, res=None):
    m, k_dim = a.shape
    tm = min(MM_TILE_M, m)
    if col:
        _, _, kk, b = g.shape
        assert kk == k_dim
        tn = _div_tile(b, 1408)
        nps = b // tn
        n = 4 * b
        grid = (m // tm, n // tn, 1)
        in_specs = [pl.BlockSpec((tm, k_dim), lambda i, j, k: (i, 0)),
                    pl.BlockSpec((None, None, k_dim, tn), lambda i, j, k: (l, j // nps, 0, j % nps))]
        nk = 1
        w = g
    else:
        lyr, _, a_rows, n = g.shape
        assert 4 * a_rows == k_dim
        w = g.reshape(lyr, k_dim, n)
        tk = _div_tile(k_dim, 1408)
        tn = n
        nk = k_dim // tk
        grid = (m // tm, 1, nk)
        in_specs = [pl.BlockSpec((tm, tk), lambda i, j, k: (i, k)),
                    pl.BlockSpec((None, tk, tn), lambda i, j, k: (l, k, j))]
    ops = [a, w]
    if res is not None:
        in_specs.append(pl.BlockSpec((tm, tn), lambda i, j, k: (i, j)))
        ops.append(res)
    return _mm(name, (1, 0), grid, in_specs, pl.BlockSpec((tm, tn), lambda i, j, k: (i, j)), _sds((m, n), out_dtype),
               nk, ops, with_res=res is not None)


def mm_dx(name, dy, g, l, col):
    m, n = dy.shape
    tm = min(MM_TILE_M, m)
    if col:
        _, _, k_dim, b = g.shape
        assert 4 * b == n
        tk = _div_tile(b, 1408)
        nps = b // tk
        nk = n // tk
        grid = (m // tm, 1, nk)
        in_specs = [pl.BlockSpec((tm, tk), lambda i, j, k: (i, k)),
                    pl.BlockSpec((None, None, k_dim, tk), lambda i, j, k: (l, k // nps, 0, k % nps))]
        out_spec = pl.BlockSpec((tm, k_dim), lambda i, j, k: (i, 0))
        w = g
    else:
        lyr, _, a_rows, nn_ = g.shape
        assert nn_ == n
        k_dim = 4 * a_rows
        w = g.reshape(lyr, k_dim, n)
        tno = _div_tile(k_dim, 1408)
        nk = 1
        grid = (m // tm, k_dim // tno, 1)
        in_specs = [pl.BlockSpec((tm, n), lambda i, j, k: (i, 0)),
                    pl.BlockSpec((None, tno, n), lambda i, j, k: (l, j, 0))]
        out_spec = pl.BlockSpec((tm, tno), lambda i, j, k: (i, j))
    return _mm(name, (1, 1), grid, in_specs, out_spec, _sds((m, k_dim), F32), nk, [dy, w])


def mm_dw(name, a, dy, col, l, n_layers, into=None):
    m, k_dim = a.shape
    _, n = dy.shape
    ts = min(MM_TILE_M, m)
    ns = m // ts
    tko = _div_tile(k_dim, 1408)
    if col:
        b = n // 4
        tn = _div_tile(b, 1408)
        nps = b // tn
        grid = (k_dim // tko, n // tn, ns)
        out_spec = pl.BlockSpec((None, None, tko, tn), lambda i, j, s: (l, j // nps, i, j % nps))
        shape = (n_layers, 4, k_dim, b)
    else:
        tn = n
        grid = (k_dim // tko, 1, ns)
        out_spec = pl.BlockSpec((None, tko, tn), lambda i, j, s: (l, i, j))
        shape = (n_layers, k_dim, n)
    in_specs = [pl.BlockSpec((ts, tko), lambda i, j, s: (s, i)),
                pl.BlockSpec((ts, tn), lambda i, j, s: (s, j))]
    ops = [a, dy]
    aliases = {}
    if into is not None:
        in_specs.append(_ANY)
        ops.append(into.reshape(shape))
        aliases = {2: 0}

    def body(a_ref, b_ref, *rest):
        o_ref = rest[-1]
        prod = _tn(a_ref[...].astype(BF16), b_ref[...].astype(BF16))
        if ns == 1:
            o_ref[...] = prod
        else:
            s = pl.program_id(2)

            @pl.when(s == 0)
            def _():
                o_ref[...] = prod

            @pl.when(s > 0)
            def _():
                o_ref[...] += prod

    out = pl.pallas_call(body, grid=grid, in_specs=in_specs, out_specs=out_spec, out_shape=_sds(shape, F32),
                         input_output_aliases=aliases, name=name,
                         compiler_params=_params(("parallel", "parallel", "arbitrary")))(*ops)
    return out.reshape(n_layers, 4, k_dim if col else k_dim // 4, shape[-1])


def _row_spec(t, c, col=0):
    return pl.BlockSpec((t, c), lambda i: (i, col))


def _vec_spec(c):
    return pl.BlockSpec((1, c), lambda i: (0, 0))


def _acc_rows(ref, i, val):
    @pl.when(i == 0)
    def _():
        ref[...] = val

    @pl.when(i > 0)
    def _():
        ref[...] += val


def rope_tables(positions, inv_freq):
    s = positions.shape[0]
    t = min(ROW_TILE, s)
    half = inv_freq.shape[1]

    def body(p_ref, f_ref, c_ref, s_ref):
        ang = p_ref[...].astype(F32) * f_ref[...]
        c_ref[...] = jnp.cos(ang)
        s_ref[...] = jnp.sin(ang)

    return pl.pallas_call(
        body, grid=(s // t,), in_specs=[_row_spec(t, 1), _vec_spec(half)],
        out_specs=[_row_spec(t, half), _row_spec(t, half)], out_shape=[_sds((s, half), F32)] * 2, name="rope_tables",
        compiler_params=_params(("parallel",)))(positions, inv_freq)


def rms_cast(h, g):
    s, d = h.shape
    t = min(ROW_TILE, s)

    def body(h_ref, g_ref, o_ref):
        x = h_ref[...]
        r = lax.rsqrt(jnp.mean(x * x, axis=-1, keepdims=True) + RMS_EPS)
        o_ref[...] = (x * r * g_ref[...]).astype(BF16)

    return pl.pallas_call(body, grid=(s // t,), in_specs=[_row_spec(t, d), _vec_spec(d)], out_specs=_row_spec(t, d),
                          out_shape=_sds((s, d), BF16), name="rms_cast", compiler_params=_params(("parallel",)))(h, g)


def _rms_bwd_math(x, g, du):
    r = lax.rsqrt(jnp.mean(x * x, axis=-1, keepdims=True) + RMS_EPS)
    gd = g * du
    dx = r * gd - x * (r * r * r) * jnp.mean(x * gd, axis=-1, keepdims=True)
    dg = jnp.sum(x * r * du, axis=0, keepdims=True)
    return dx, dg


def rms_bwd(h, g, du, dres=None):
    s, d = h.shape
    t = min(ROW_TILE, s)

    def body(*refs):
        if dres is None:
            h_ref, g_ref, du_ref, dh_ref, dg_ref = refs
        else:
            h_ref, g_ref, du_ref, dr_ref, dh_ref, dg_ref = refs
        dx, dg = _rms_bwd_math(h_ref[...], g_ref[...], du_ref[...])
        if dres is not None:
            dx = dx + dr_ref[...]
        dh_ref[...] = dx
        _acc_rows(dg_ref, pl.program_id(0), dg)

    in_specs = [_row_spec(t, d), _vec_spec(d), _row_spec(t, d)]
    ops = [h, g, du]
    if dres is not None:
        in_specs.append(_row_spec(t, d))
        ops.append(dres)
    return pl.pallas_call(body, grid=(s // t,), in_specs=in_specs, out_specs=[_row_spec(t, d), _vec_spec(d)],
                          out_shape=[_sds((s, d), F32), _sds((1, d), F32)], name="rms_bwd",
                          compiler_params=_params(("arbitrary",)))(*ops)


def loss_head(h, g, target):
    s, d = h.shape
    t = min(ROW_TILE, s)

    def body(h_ref, g_ref, t_ref, dh_ref, dg_ref, loss_ref):
        x = h_ref[...]
        gg = g_ref[...]
        r = lax.rsqrt(jnp.mean(x * x, axis=-1, keepdims=True) + RMS_EPS)
        err = x * r * gg - t_ref[...]
        part = 0.5 * jnp.sum(jnp.mean(err * err, axis=-1, keepdims=True), axis=0, keepdims=True)
        dy = err * (1.0 / d)
        dx, dg = _rms_bwd_math(x, gg, dy)
        dh_ref[...] = dx
        i = pl.program_id(0)
        _acc_rows(dg_ref, i, dg)
        _acc_rows(loss_ref, i, jnp.broadcast_to(part, (1, 128)))

    return pl.pallas_call(
        body, grid=(s // t,), in_specs=[_row_spec(t, d), _vec_spec(d), _row_spec(t, d)],
        out_specs=[_row_spec(t, d), _vec_spec(d), _vec_spec(128)],
        out_shape=[_sds((s, d), F32), _sds((1, d), F32), _sds((1, 128), F32)], name="loss_head",
        compiler_params=_params(("arbitrary",)))(h, g, target)


def _rot(x, cos, sin):
    half = x.shape[-1] // 2
    x1, x2 = x[:, :half], x[:, half:]
    return jnp.concatenate([x1 * cos - x2 * sin, x2 * cos + x1 * sin], axis=-1)


def _rot_t(dy, cos, sin):
    half = dy.shape[-1] // 2
    d1, d2 = dy[:, :half], dy[:, half:]
    return jnp.concatenate([d1 * cos + d2 * sin, d2 * cos - d1 * sin], axis=-1)


def _decay_tables():
    log_gamma = jnp.log(1.0 - jnp.power(2.0, -5.0 - jnp.arange(RET_HEADS, dtype=F32)))
    idx = jnp.arange(CHUNK, dtype=F32)
    dist = jnp.abs(idx[:, None] - idx[None, :])
    d_inner = jnp.exp(log_gamma[:, None, None] * dist)
    decay_q = jnp.exp(log_gamma[None, :] * (idx[:, None] + 1.0))
    decay_k = jnp.exp(log_gamma[None, :] * (CHUNK - 1.0 - idx[:, None]))
    decay_chunk = jnp.exp(log_gamma * CHUNK)
    return d_inner, decay_q.T[:, :, None], decay_k.T[:, :, None], decay_chunk[:, None, None]


_QK_SCALE = RET_QK_DIM ** -0.5


def retention_fwd(p, cos, sin, gn_g, tables):
    s = p.shape[0]
    t = min(RET_TILE, s)
    nc = t // CHUNK
    nt = s // t
    dk, dv = RET_QK_DIM, RET_V_DIM
    d_inner, decay_q, decay_k, decay_chunk = tables

    def body(q_ref, k_ref, v_ref, gr_ref, cos_ref, sin_ref, gn_ref, di_ref, dq_ref, dkk_ref, dc_ref,
             o_ref, z_ref, st_ref, state):
        i = pl.program_id(1)

        @pl.when(i == 0)
        def _():
            state[...] = jnp.zeros_like(state)

        cs, sn = cos_ref[...], sin_ref[...]
        qr = _rot(q_ref[...], cs, sn) * _QK_SCALE
        kr = _rot(k_ref[...], cs, sn)
        dmat, dq, dkk, gam = di_ref[...], dq_ref[...], dkk_ref[...], dc_ref[...]
        for c in range(nc):
            sl = slice(c * CHUNK, (c + 1) * CHUNK)
            qc = qr[sl].astype(BF16)
            kc = kr[sl].astype(BF16)
            vc = v_ref[sl, :].astype(BF16)
            scores = _nt(qc, kc) * dmat
            st = state[...].astype(BF16)
            st_ref[c] = st
            o = _nn(scores.astype(BF16), vc) + _nn(qc, st) * dq
            kd = (kr[sl] * dkk).astype(BF16)
            state[...] = state[...] * gam + _tn(kd, vc)
            o_ref[sl, :] = o
            mu = jnp.mean(o, axis=-1, keepdims=True)
            oc = o - mu
            var = jnp.mean(oc * oc, axis=-1, keepdims=True)
            y = oc * lax.rsqrt(var + LN_EPS) * gn_ref[...]
            gr = gr_ref[sl, :]
            z_ref[sl, :] = (gr * _sigmoid(gr) * y).astype(BF16)

    hmap = lambda h, i: (h, 0, 0)
    in_specs = [
        pl.BlockSpec((t, dk), lambda h, i: (i, h)),
        pl.BlockSpec((t, dk), lambda h, i: (i, RET_HEADS + h)),
        pl.BlockSpec((t, dv), lambda h, i: (i, 4 + h)),
        pl.BlockSpec((t, dv), lambda h, i: (i, 8 + h)),
        pl.BlockSpec((t, dk // 2), lambda h, i: (i, 0)),
        pl.BlockSpec((t, dk // 2), lambda h, i: (i, 0)),
        pl.BlockSpec((1, dv), lambda h, i: (0, h)),
        pl.BlockSpec((None, CHUNK, CHUNK), hmap),
        pl.BlockSpec((None, CHUNK, 1), hmap),
        pl.BlockSpec((None, CHUNK, 1), hmap),
        pl.BlockSpec((None, 1, 1), hmap),
    ]
    out_specs = [pl.BlockSpec((t, dv), lambda h, i: (i, h)),
                 pl.BlockSpec((t, dv), lambda h, i: (i, h)),
                 pl.BlockSpec((None, nc, dk, dv), lambda h, i: (h, i, 0, 0))]
    out_shape = [_sds((s, RET_HEADS * dv), F32), _sds((s, RET_HEADS * dv), BF16),
                 _sds((RET_HEADS, s // CHUNK, dk, dv), BF16)]
    return pl.pallas_call(
        body, grid=(RET_HEADS, nt), in_specs=in_specs, out_specs=out_specs, out_shape=out_shape,
        scratch_shapes=[pltpu.VMEM((dk, dv), F32)], name="retention_fwd",
        compiler_params=_params(("parallel", "arbitrary")))(p, p, p, p, cos, sin, gn_g, d_inner, decay_q, decay_k,
                                                            decay_chunk)


def gn_gate_bwd(o, p, gn_g, dz):
    s = o.shape[0]
    t = min(ROW_TILE, s)
    dv = RET_V_DIM
    w = RET_HEADS * dv

    def body(o_ref, gr_ref, gn_ref, dz_ref, do_ref, dgr_ref, dgn_ref):
        dgn_parts = []
        for h in range(RET_HEADS):
            sl = slice(h * dv, (h + 1) * dv)
            oo = o_ref[:, sl]
            gr = gr_ref[:, sl]
            dz = dz_ref[:, sl]
            gn = gn_ref[:, sl]
            mu = jnp.mean(oo, axis=-1, keepdims=True)
            oc = oo - mu
            rstd = lax.rsqrt(jnp.mean(oc * oc, axis=-1, keepdims=True) + LN_EPS)
            y = oc * rstd
            sg = _sigmoid(gr)
            act = gr * sg
            dyg = dz * act
            dgn_parts.append(jnp.sum(dyg * y, axis=0, keepdims=True))
            dy = dyg * gn
            do_ref[:, sl] = rstd * (dy - jnp.mean(dy, axis=-1, keepdims=True)
                                    - y * jnp.mean(dy * y, axis=-1, keepdims=True))
            dgr_ref[:, sl] = (dz * (y * gn) * (sg * (1.0 + gr * (1.0 - sg)))).astype(BF16)
        _acc_rows(dgn_ref, pl.program_id(0), jnp.concatenate(dgn_parts, axis=-1))

    return pl.pallas_call(
        body, grid=(s // t,), in_specs=[_row_spec(t, w), _row_spec(t, w, 2), _vec_spec(w), _row_spec(t, w)],
        out_specs=[_row_spec(t, w), _row_spec(t, w), _vec_spec(w)],
        out_shape=[_sds((s, w), F32), _sds((s, w), BF16), _sds((1, w), F32)], name="gn_gate_bwd",
        compiler_params=_params(("arbitrary",)))(o, p, gn_g, dz)


def retention_bwd(p, cos, sin, states, do, tables):
    s = p.shape[0]
    t = min(RET_TILE, s)
    nc = t // CHUNK
    nt = s // t
    dk, dv = RET_QK_DIM, RET_V_DIM
    d_inner, decay_q, decay_k, decay_chunk = tables

    def body(q_ref, k_ref, v_ref, cos_ref, sin_ref, st_ref, do_ref, di_ref, dq_ref, dkk_ref, dc_ref,
             gq_ref, gk_ref, gv_ref, dstate):
        i = pl.program_id(1)

        @pl.when(i == 0)
        def _():
            dstate[...] = jnp.zeros_like(dstate)

        cs, sn = cos_ref[...], sin_ref[...]
        qr = _rot(q_ref[...], cs, sn) * _QK_SCALE
        kr = _rot(k_ref[...], cs, sn)
        dmat, dq, dkk, gam = di_ref[...], dq_ref[...], dkk_ref[...], dc_ref[...]
        for c in range(nc - 1, -1, -1):
            sl = slice(c * CHUNK, (c + 1) * CHUNK)
            qc = qr[sl].astype(BF16)
            kc = kr[sl].astype(BF16)
            vc = v_ref[sl, :].astype(BF16)
            kd = (kr[sl] * dkk).astype(BF16)
            st = st_ref[c]
            d_o = do_ref[sl, :]
            dob = d_o.astype(BF16)
            ab = (_nt(qc, kc) * dmat).astype(BF16)
            dsb = dstate[...].astype(BF16)
            dvv = _tn(ab, dob) + _nn(kd, dsb)
            dkd = _nt(vc, dsb)
            dcb = (d_o * dq).astype(BF16)
            dpb = (_nt(dob, vc) * dmat).astype(BF16)
            dqq = _nt(dcb, st) + _nn(dpb, kc)
            dkv = _tn(dpb, qc) + dkd * dkk
            dstate[...] = dstate[...] * gam + _tn(qc, dcb)
            gq_ref[sl, :] = _rot_t(dqq * _QK_SCALE, cs[sl], sn[sl]).astype(BF16)
            gk_ref[sl, :] = _rot_t(dkv, cs[sl], sn[sl]).astype(BF16)
            gv_ref[sl, :] = dvv.astype(BF16)

    hmap = lambda h, i: (h, 0, 0)
    rev = lambda i: nt - 1 - i
    in_specs = [
        pl.BlockSpec((t, dk), lambda h, i: (rev(i), h)),
        pl.BlockSpec((t, dk), lambda h, i: (rev(i), RET_HEADS + h)),
        pl.BlockSpec((t, dv), lambda h, i: (rev(i), 4 + h)),
        pl.BlockSpec((t, dk // 2), lambda h, i: (rev(i), 0)),
        pl.BlockSpec((t, dk // 2), lambda h, i: (rev(i), 0)),
        pl.BlockSpec((None, nc, dk, dv), lambda h, i: (h, rev(i), 0, 0)),
        pl.BlockSpec((t, dv), lambda h, i: (rev(i), h)),
        pl.BlockSpec((None, CHUNK, CHUNK), hmap),
        pl.BlockSpec((None, CHUNK, 1), hmap),
        pl.BlockSpec((None, CHUNK, 1), hmap),
        pl.BlockSpec((None, 1, 1), hmap),
    ]
    out_specs = [pl.BlockSpec((t, dk), lambda h, i: (rev(i), h)),
                 pl.BlockSpec((t, dk), lambda h, i: (rev(i), h)),
                 pl.BlockSpec((t, dv), lambda h, i: (rev(i), h))]
    out_shape = [_sds((s, RET_HEADS * dk), BF16), _sds((s, RET_HEADS * dk), BF16), _sds((s, RET_HEADS * dv), BF16)]
    return pl.pallas_call(
        body, grid=(RET_HEADS, nt), in_specs=in_specs, out_specs=out_specs, out_shape=out_shape,
        scratch_shapes=[pltpu.VMEM((dk, dv), F32)], name="retention_bwd",
        compiler_params=_params(("parallel", "arbitrary")))(p, p, p, cos, sin, states, do, d_inner, decay_q, decay_k,
                                                            decay_chunk)


A_COL, B_COL = 6, 7


def _prev_rows_spec(t, halo, width, col):
    per = t // halo
    return pl.BlockSpec((halo, width), lambda i: (jnp.maximum(i * per - 1, 0), col))


def _next_rows_spec(t, halo, width, col, n_rows):
    per = t // halo
    last = n_rows // halo - 1
    return pl.BlockSpec((halo, width), lambda i: (jnp.minimum((i + 1) * per, last), col))


def conv_fwd(p, dw_w, dw_b, ln_g, ln_b):
    s = p.shape[0]
    t = min(ROW_TILE, s)
    c = D_MODEL
    hl = CONV_HALO

    def body(a_ref, b_ref, ah_ref, bh_ref, w_ref, wb_ref, g_ref, bb_ref, c1_ref, c3_ref, ext):
        i = pl.program_id(0)
        ext[pl.ds(0, hl), :] = jnp.where(i > 0, ah_ref[...] * _sigmoid(bh_ref[...]), 0.0)
        ext[pl.ds(hl, t), :] = a_ref[...] * _sigmoid(b_ref[...])
        acc = jnp.broadcast_to(wb_ref[...], (t, c))
        for j in range(CONV_WIDTH):
            acc = acc + w_ref[j:j + 1, :] * ext[pl.ds(hl - (CONV_WIDTH - 1) + j, t), :]
        c1_ref[...] = acc
        mu = jnp.mean(acc, axis=-1, keepdims=True)
        xc = acc - mu
        var = jnp.mean(xc * xc, axis=-1, keepdims=True)
        c2 = xc * lax.rsqrt(var + LN_EPS) * g_ref[...] + bb_ref[...]
        c3_ref[...] = (c2 * _sigmoid(c2)).astype(BF16)

    in_specs = [_row_spec(t, c, A_COL), _row_spec(t, c, B_COL),
                _prev_rows_spec(t, hl, c, A_COL), _prev_rows_spec(t, hl, c, B_COL),
                pl.BlockSpec((CONV_WIDTH, c), lambda i: (0, 0)), _vec_spec(c), _vec_spec(c), _vec_spec(c)]
    return pl.pallas_call(
        body, grid=(s // t,), in_specs=in_specs, out_specs=[_row_spec(t, c), _row_spec(t, c)],
        out_shape=[_sds((s, c), F32), _sds((s, c), BF16)], scratch_shapes=[pltpu.VMEM((t + hl, c), F32)],
        name="conv_fwd", compiler_params=_params(("parallel",)))(p, p, p, p, dw_w, dw_b, ln_g, ln_b)


def conv_ln_bwd(c1, ln_g, ln_b, dc3):
    s, c = c1.shape
    t = min(ROW_TILE, s)

    def body(c1_ref, g_ref, b_ref, d_ref, dc1_ref, dg_ref, db_ref):
        x = c1_ref[...]
        g = g_ref[...]
        mu = jnp.mean(x, axis=-1, keepdims=True)
        xc = x - mu
        rstd = lax.rsqrt(jnp.mean(xc * xc, axis=-1, keepdims=True) + LN_EPS)
        y = xc * rstd
        c2 = y * g + b_ref[...]
        sg = _sigmoid(c2)
        dc2 = d_ref[...] * (sg * (1.0 + c2 * (1.0 - sg)))
        i = pl.program_id(0)
        _acc_rows(db_ref, i, jnp.sum(dc2, axis=0, keepdims=True))
        _acc_rows(dg_ref, i, jnp.sum(dc2 * y, axis=0, keepdims=True))
        dy = dc2 * g
        dc1_ref[...] = rstd * (dy - jnp.mean(dy, axis=-1, keepdims=True)
                               - y * jnp.mean(dy * y, axis=-1, keepdims=True))

    return pl.pallas_call(
        body, grid=(s // t,), in_specs=[_row_spec(t, c), _vec_spec(c), _vec_spec(c), _row_spec(t, c)],
        out_specs=[_row_spec(t, c), _vec_spec(c), _vec_spec(c)],
        out_shape=[_sds((s, c), F32), _sds((1, c), F32), _sds((1, c), F32)], name="conv_ln_bwd",
        compiler_params=_params(("arbitrary",)))(c1, ln_g, ln_b, dc3)


def conv_dw_bwd(p, dc1, dw_w):
    s = p.shape[0]
    t = min(ROW_TILE, s)
    c = D_MODEL
    hl = CONV_HALO
    nt = s // t

    def body(a_ref, b_ref, ah_ref, bh_ref, d_ref, dn_ref, w_ref, dab_ref, dw_ref, dbias_ref, ext_c, ext_d):
        i = pl.program_id(0)
        a = a_ref[...]
        sb = _sigmoid(b_ref[...])
        ext_c[pl.ds(0, hl), :] = jnp.where(i > 0, ah_ref[...] * _sigmoid(bh_ref[...]), 0.0)
        ext_c[pl.ds(hl, t), :] = a * sb
        d = d_ref[...]
        ext_d[pl.ds(0, t), :] = d
        ext_d[pl.ds(t, hl), :] = jnp.where(i < nt - 1, dn_ref[...], 0.0)
        dc0 = jnp.zeros((t, c), F32)
        rows = []
        for j in range(CONV_WIDTH):
            dc0 = dc0 + w_ref[j:j + 1, :] * ext_d[pl.ds(CONV_WIDTH - 1 - j, t), :]
            rows.append(jnp.sum(d * ext_c[pl.ds(hl - (CONV_WIDTH - 1) + j, t), :], axis=0, keepdims=True))
        _acc_rows(dw_ref, i, jnp.concatenate(rows, axis=0))
        _acc_rows(dbias_ref, i, jnp.sum(d, axis=0, keepdims=True))
        dab_ref[:, :c] = (dc0 * sb).astype(BF16)
        dab_ref[:, c:] = (dc0 * a * sb * (1.0 - sb)).astype(BF16)

    in_specs = [_row_spec(t, c, A_COL), _row_spec(t, c, B_COL),
                _prev_rows_spec(t, hl, c, A_COL), _prev_rows_spec(t, hl, c, B_COL),
                _row_spec(t, c), _next_rows_spec(t, hl, c, 0, s),
                pl.BlockSpec((CONV_WIDTH, c), lambda i: (0, 0))]
    return pl.pallas_call(
        body, grid=(nt,), in_specs=in_specs,
        out_specs=[_row_spec(t, 2 * c), pl.BlockSpec((CONV_WIDTH, c), lambda i: (0, 0)), _vec_spec(c)],
        out_shape=[_sds((s, 2 * c), BF16), _sds((CONV_WIDTH, c), F32), _sds((1, c), F32)],
        scratch_shapes=[pltpu.VMEM((t + hl, c), F32), pltpu.VMEM((t + hl, c), F32)], name="conv_dw_bwd",
        compiler_params=_params(("arbitrary",)))(p, p, p, p, dc1, dc1, dw_w)


GATE_COL = 4


def gate_mix_fwd(p, b_gate, y_a, y_b, b_conv_out):
    s = p.shape[0]
    t = min(ROW_TILE, s)
    c = D_MODEL

    def body(gt_ref, bg_ref, ya_ref, yb_ref, bc_ref, o_ref):
        gs = _sigmoid(gt_ref[...] + bg_ref[...])
        o_ref[...] = (gs[:, :c] * ya_ref[...] + gs[:, c:] * (yb_ref[...] + bc_ref[...])).astype(BF16)

    return pl.pallas_call(
        body, grid=(s // t,),
        in_specs=[_row_spec(t, 2 * c, GATE_COL), _vec_spec(2 * c), _row_spec(t, c), _row_spec(t, c), _vec_spec(c)],
        out_specs=_row_spec(t, c), out_shape=_sds((s, c), BF16), name="gate_mix_fwd",
        compiler_params=_params(("parallel",)))(p, b_gate, y_a, y_b, b_conv_out)


def gate_mix_bwd(p, b_gate, y_a, y_b, b_conv_out, dmix):
    s = p.shape[0]
    t = min(ROW_TILE, s)
    c = D_MODEL

    def body(gt_ref, bg_ref, ya_ref, yb_ref, bc_ref, d_ref, dya_ref, dyb_ref, dgt_ref, dbg_ref, dbc_ref):
        gs = _sigmoid(gt_ref[...] + bg_ref[...])
        ga, gb = gs[:, :c], gs[:, c:]
        d = d_ref[...]
        dya = ga * d
        dyb = gb * d
        dya_ref[...] = dya.astype(BF16)
        dyb_ref[...] = dyb.astype(BF16)
        dga = d * ya_ref[...] * ga * (1.0 - ga)
        dgb = d * (yb_ref[...] + bc_ref[...]) * gb * (1.0 - gb)
        dgt_ref[:, :c] = dga.astype(BF16)
        dgt_ref[:, c:] = dgb.astype(BF16)
        i = pl.program_id(0)
        _acc_rows(dbg_ref, i, jnp.concatenate([jnp.sum(dga, axis=0, keepdims=True),
                                               jnp.sum(dgb, axis=0, keepdims=True)], axis=-1))
        _acc_rows(dbc_ref, i, jnp.sum(dyb, axis=0, keepdims=True))

    return pl.pallas_call(
        body, grid=(s // t,),
        in_specs=[_row_spec(t, 2 * c, GATE_COL), _vec_spec(2 * c), _row_spec(t, c), _row_spec(t, c), _vec_spec(c),
                  _row_spec(t, c)],
        out_specs=[_row_spec(t, c), _row_spec(t, c), _row_spec(t, 2 * c), _vec_spec(2 * c), _vec_spec(c)],
        out_shape=[_sds((s, c), BF16), _sds((s, c), BF16), _sds((s, 2 * c), BF16), _sds((1, 2 * c), F32),
                   _sds((1, c), F32)],
        name="gate_mix_bwd", compiler_params=_params(("arbitrary",)))(p, b_gate, y_a, y_b, b_conv_out, dmix)


_X_SCALE = X_HEAD_DIM ** -0.5


def _softmax_rows(sc):
    m = jnp.max(sc, axis=-1, keepdims=True)
    e = jnp.exp(sc - m)
    return e / jnp.sum(e, axis=-1, keepdims=True)


def attn_fwd(qx, kv):
    s, d = qx.shape
    m = kv.shape[0]
    t = min(ROW_TILE, s)
    hd = X_HEAD_DIM

    def body(q_ref, kv_ref, o_ref):
        for h in range(X_HEADS):
            sl = slice(h * hd, (h + 1) * hd)
            kh = kv_ref[:, sl].astype(BF16)
            vh = kv_ref[:, d + h * hd:d + (h + 1) * hd].astype(BF16)
            pr = _softmax_rows(_nt(q_ref[:, sl], kh) * _X_SCALE)
            o_ref[:, sl] = _nn(pr.astype(BF16), vh).astype(BF16)

    return pl.pallas_call(
        body, grid=(s // t,), in_specs=[_row_spec(t, d), pl.BlockSpec((m, 2 * d), lambda i: (0, 0))],
        out_specs=_row_spec(t, d), out_shape=_sds((s, d), BF16), name="attn_fwd",
        compiler_params=_params(("parallel",)))(qx, kv)


def attn_bwd(qx, kv, dox):
    s, d = qx.shape
    m = kv.shape[0]
    t = min(ROW_TILE, s)
    hd = X_HEAD_DIM

    def body(q_ref, kv_ref, do_ref, dq_ref, dkv_ref):
        dks, dvs = [], []
        for h in range(X_HEADS):
            sl = slice(h * hd, (h + 1) * hd)
            qh = q_ref[:, sl]
            kh = kv_ref[:, sl].astype(BF16)
            vh = kv_ref[:, d + h * hd:d + (h + 1) * hd].astype(BF16)
            pr = _softmax_rows(_nt(qh, kh) * _X_SCALE)
            doh = do_ref[:, sl].astype(BF16)
            dpr = _nt(doh, vh)
            dvs.append(_tn(pr.astype(BF16), doh))
            ds = pr * (dpr - jnp.sum(dpr * pr, axis=-1, keepdims=True))
            dsb = (ds * _X_SCALE).astype(BF16)
            dq_ref[:, sl] = _nn(dsb, kh).astype(BF16)
            dks.append(_tn(dsb, qh))
        _acc_rows(dkv_ref, pl.program_id(0), jnp.concatenate(dks + dvs, axis=-1))

    return pl.pallas_call(
        body, grid=(s // t,),
        in_specs=[_row_spec(t, d), pl.BlockSpec((m, 2 * d), lambda i: (0, 0)), _row_spec(t, d)],
        out_specs=[_row_spec(t, d), pl.BlockSpec((m, 2 * d), lambda i: (0, 0))],
        out_shape=[_sds((s, d), BF16), _sds((m, 2 * d), F32)], name="attn_bwd",
        compiler_params=_params(("arbitrary",)))(qx, kv, dox)


def ffn_act_fwd(up, dw_w, dw_b):
    s = up.shape[0]
    f = FFN_DIM
    t = min(ROW_TILE, s)
    hl = FFN_HALO

    def body(val_ref, gt_ref, gh_ref, w_ref, b_ref, o_ref, ext):
        i = pl.program_id(0)
        ext[pl.ds(0, hl), :] = jnp.where(i > 0, gh_ref[...], 0.0)
        ext[pl.ds(hl, t), :] = gt_ref[...]
        gc = b_ref[...] + w_ref[0:1, :] * ext[pl.ds(hl - 2, t), :] + w_ref[1:2, :] * ext[pl.ds(hl - 1, t), :] \
            + w_ref[2:3, :] * ext[pl.ds(hl, t), :]
        o_ref[...] = (gc * _sigmoid(gc) * val_ref[...]).astype(BF16)

    return pl.pallas_call(
        body, grid=(s // t,),
        in_specs=[_row_spec(t, f, 0), _row_spec(t, f, 1), _prev_rows_spec(t, hl, f, 1),
                  pl.BlockSpec((3, f), lambda i: (0, 0)), _vec_spec(f)],
        out_specs=_row_spec(t, f), out_shape=_sds((s, f), BF16), scratch_shapes=[pltpu.VMEM((t + hl, f), F32)],
        name="ffn_act_fwd", compiler_params=_params(("parallel",)))(up, up, up, dw_w, dw_b)


def ffn_act_bwd(up, dw_w, dw_b, da):
    s = up.shape[0]
    f = FFN_DIM
    t = min(ROW_TILE, s)
    hl = FFN_HALO
    nt = s // t

    def body(val_ref, valn_ref, gt_ref, gp_ref, gn_ref, da_ref, dan_ref, w_ref, b_ref,
             dup_ref, dw_ref, db_ref, ext_g, ext_d):
        i = pl.program_id(0)
        w0, w1, w2 = w_ref[0:1, :], w_ref[1:2, :], w_ref[2:3, :]
        ext_g[pl.ds(0, hl), :] = jnp.where(i > 0, gp_ref[...], 0.0)
        ext_g[pl.ds(hl, t), :] = gt_ref[...]
        ext_g[pl.ds(hl + t, hl), :] = gn_ref[...]

        def dgc_of(rows, off, val, da_rows):
            gc = b_ref[...] + w0 * ext_g[pl.ds(off + hl - 2, rows), :] + w1 * ext_g[pl.ds(off + hl - 1, rows), :] \
                + w2 * ext_g[pl.ds(off + hl, rows), :]
            sg = _sigmoid(gc)
            return da_rows * val * (sg * (1.0 + gc * (1.0 - sg))), gc * sg

        da_t = da_ref[...]
        dgc, act = dgc_of(t, 0, val_ref[...], da_t)
        dgc_n, _ = dgc_of(hl, t, valn_ref[...], dan_ref[...])
        ext_d[pl.ds(0, t), :] = dgc
        ext_d[pl.ds(t, hl), :] = jnp.where(i < nt - 1, dgc_n, 0.0)
        dup_ref[:, :f] = (da_t * act).astype(BF16)
        dup_ref[:, f:] = (w2 * dgc + w1 * ext_d[pl.ds(1, t), :] + w0 * ext_d[pl.ds(2, t), :]).astype(BF16)
        rows = [jnp.sum(dgc * ext_g[pl.ds(hl - 2 + k, t), :], axis=0, keepdims=True) for k in range(3)]
        _acc_rows(dw_ref, i, jnp.concatenate(rows, axis=0))
        _acc_rows(db_ref, i, jnp.sum(dgc, axis=0, keepdims=True))

    in_specs = [_row_spec(t, f, 0), _next_rows_spec(t, hl, f, 0, s),
                _row_spec(t, f, 1), _prev_rows_spec(t, hl, f, 1), _next_rows_spec(t, hl, f, 1, s),
                _row_spec(t, f), _next_rows_spec(t, hl, f, 0, s),
                pl.BlockSpec((3, f), lambda i: (0, 0)), _vec_spec(f)]
    return pl.pallas_call(
        body, grid=(nt,), in_specs=in_specs,
        out_specs=[_row_spec(t, 2 * f), pl.BlockSpec((3, f), lambda i: (0, 0)), _vec_spec(f)],
        out_shape=[_sds((s, 2 * f), BF16), _sds((3, f), F32), _sds((1, f), F32)],
        scratch_shapes=[pltpu.VMEM((t + 2 * hl, f), F32), pltpu.VMEM((t + hl, f), F32)], name="ffn_act_bwd",
        compiler_params=_params(("arbitrary",)))(up, up, up, up, up, da, da, dw_w, dw_b)


def local_step(x, mem, positions, target, gw, sp):
    n_layers = gw["w_in"].shape[0]
    inv_freq = 1.0 / (ROPE_THETA ** (jnp.arange(0, RET_QK_DIM, 2, dtype=F32) / RET_QK_DIM))
    cos, sin = rope_tables(positions, inv_freq[None, :])
    tables = _decay_tables()
    row = lambda name, l: sp[name][l][None, :]

    saved = []
    h = x
    for l in range(n_layers):
        a = {"h0": h}
        a["u"] = rms_cast(h, row("norm_mix_g", l))
        a["p"] = mm_fwd("mm_in", a["u"], gw["w_in"], l, True)
        a["o"], a["z"], a["states"] = retention_fwd(a["p"], cos, sin, row("ret_gn_g", l), tables)
        a["y_a"] = mm_fwd("mm_ret_out", a["z"], gw["w_ret_out"], l, False)
        a["c1"], a["c3"] = conv_fwd(a["p"], sp["conv_dw_w"][l], row("conv_dw_b", l), row("conv_ln_g", l),
                                    row("conv_ln_b", l))
        a["y_b"] = mm_fwd("mm_conv_out", a["c3"], gw["w_conv_out"], l, False)
        a["mixed"] = gate_mix_fwd(a["p"], row("b_gate", l), a["y_a"], a["y_b"], row("b_conv_out", l))
        a["h1"] = mm_fwd("mm_mix_out", a["mixed"], gw["w_mix_out"], l, False, res=h)
        a["hx"] = rms_cast(a["h1"], row("norm_xattn_g", l))
        a["qx"] = mm_fwd("mm_xq", a["hx"], gw["w_xq"], l, False, out_dtype=BF16)
        a["mem_n"] = rms_cast(mem, row("norm_mem_g", l))
        a["kv"] = mm_fwd("mm_xkv", a["mem_n"], gw["w_xkv"], l, True)
        a["ox"] = attn_fwd(a["qx"], a["kv"])
        a["h2"] = mm_fwd("mm_xo", a["ox"], gw["w_xo"], l, False, res=a["h1"])
        a["hf"] = rms_cast(a["h2"], row("norm_ffn_g", l))
        a["up"] = mm_fwd("mm_up", a["hf"], gw["w_up"], l, True)
        a["act"] = ffn_act_fwd(a["up"], sp["ffn_dw_w"][l], row("ffn_dw_b", l))
        h = mm_fwd("mm_down", a["act"], gw["w_down"], l, False, res=a["h2"])
        saved.append(a)

    dh, d_final_g, loss = loss_head(h, sp["norm_final_g"][None, :], target)

    big = {}

    def dw(name, key, act, dy, col, l):
        big[key] = mm_dw(name, act, dy, col, l, n_layers, big.get(key))

    small = {n: [None] * n_layers for n in SMALL_REPL + SMALL_SHARDED if n != "norm_final_g"}
    for l in range(n_layers - 1, -1, -1):
        a = saved[l]
        d_act = mm_dx("mm_down_dx", dh, gw["w_down"], l, False)
        dw("mm_down_dw", "w_down", a["act"], dh, False, l)
        d_up, small["ffn_dw_w"][l], small["ffn_dw_b"][l] = ffn_act_bwd(a["up"], sp["ffn_dw_w"][l],
                                                                        row("ffn_dw_b", l), d_act)
        d_hf = mm_dx("mm_up_dx", d_up, gw["w_up"], l, True)
        dw("mm_up_dw", "w_up", a["hf"], d_up, True, l)
        dh, small["norm_ffn_g"][l] = rms_bwd(a["h2"], row("norm_ffn_g", l), d_hf, dh)
        d_ox = mm_dx("mm_xo_dx", dh, gw["w_xo"], l, False)
        dw("mm_xo_dw", "w_xo", a["ox"], dh, False, l)
        d_qx, d_kv = attn_bwd(a["qx"], a["kv"], d_ox)
        d_hx = mm_dx("mm_xq_dx", d_qx, gw["w_xq"], l, False)
        dw("mm_xq_dw", "w_xq", a["hx"], d_qx, False, l)
        d_mem_n = mm_dx("mm_xkv_dx", d_kv, gw["w_xkv"], l, True)
        dw("mm_xkv_dw", "w_xkv", a["mem_n"], d_kv, True, l)
        _, small["norm_mem_g"][l] = rms_bwd(mem, row("norm_mem_g", l), d_mem_n)
        dh, small["norm_xattn_g"][l] = rms_bwd(a["h1"], row("norm_xattn_g", l), d_hx, dh)
        d_mixed = mm_dx("mm_mix_out_dx", dh, gw["w_mix_out"], l, False)
        dw("mm_mix_out_dw", "w_mix_out", a["mixed"], dh, False, l)
        d_ya, d_yb, dp_gate, small["b_gate"][l], small["b_conv_out"][l] = gate_mix_bwd(
            a["p"], row("b_gate", l), a["y_a"], a["y_b"], row("b_conv_out", l), d_mixed)
        d_c3 = mm_dx("mm_conv_out_dx", d_yb, gw["w_conv_out"], l, False)
        dw("mm_conv_out_dw", "w_conv_out", a["c3"], d_yb, False, l)
        d_c1, small["conv_ln_g"][l], small["conv_ln_b"][l] = conv_ln_bwd(a["c1"], row("conv_ln_g", l),
                                                                         row("conv_ln_b", l), d_c3)
        dp_conv, small["conv_dw_w"][l], small["conv_dw_b"][l] = conv_dw_bwd(a["p"], d_c1, sp["conv_dw_w"][l])
        d_z = mm_dx("mm_ret_out_dx", d_ya, gw["w_ret_out"], l, False)
        dw("mm_ret_out_dw", "w_ret_out", a["z"], d_ya, False, l)
        d_o, dp_gret, small["ret_gn_g"][l] = gn_gate_bwd(a["o"], a["p"], row("ret_gn_g", l), d_z)
        dp_q, dp_k, dp_v = retention_bwd(a["p"], cos, sin, a["states"], d_o, tables)
        dp = jnp.concatenate([dp_q, dp_k, dp_v, dp_gret, dp_conv, dp_gate], axis=1)
        d_u = mm_dx("mm_in_dx", dp, gw["w_in"], l, True)
        dw("mm_in_dw", "w_in", a["u"], dp, True, l)
        dh, small["norm_mix_g"][l] = rms_bwd(a["h0"], row("norm_mix_g", l), d_u, dh)

    small = {n: jnp.stack([g.reshape(sp[n].shape[1:]) for g in v]) for n, v in small.items()}
    small["norm_final_g"] = d_final_g.reshape(-1)
    return loss, dh, big, small


_ANY = pl.BlockSpec(memory_space=pl.ANY)


def _place():
    x, y, c = lax.axis_index("x"), lax.axis_index("y"), lax.axis_index("c")
    return x, y, c


def _other_chips(x, y):
    return [(1 - x, y), (x, 1 - y), (1 - x, 1 - y)]


def place_shard(w, chip):
    lyr, a, b = w.shape
    t = _flat_tile(a, b)

    def body(chip_ref, w_ref, o_ref):
        o_ref[...] = w_ref[...].astype(BF16)

    grid_spec = pltpu.PrefetchScalarGridSpec(
        num_scalar_prefetch=1, grid=(lyr, a // t),
        in_specs=[pl.BlockSpec((None, t, b), lambda l, i, cr: (l, i, 0))],
        out_specs=pl.BlockSpec((None, None, t, b), lambda l, i, cr: (l, cr[0], i, 0)))
    return pl.pallas_call(body, grid_spec=grid_spec, out_shape=_sds((lyr, N_CHIPS, a, b), BF16), name="place_shard",
                          compiler_params=_params(("parallel", "parallel")))(chip, w)


def gather_weights(bufs):
    n = len(bufs)

    def body(*refs):
        outs = refs[n:2 * n]
        ici_send, ici_recv, pair_send, pair_recv = refs[2 * n:]
        x, y, c = _place()
        mine = 2 * x + y
        chips = _other_chips(x, y)

        def part(w, core, slot):
            lh = outs[w].shape[0] // 2
            return outs[w].at[pl.ds(core * lh, lh), slot]

        def over_ici(w, j, slot):
            px, py = chips[j]
            return pltpu.make_async_remote_copy(
                src_ref=part(w, c, slot), dst_ref=part(w, c, slot), send_sem=ici_send.at[w, j],
                recv_sem=ici_recv.at[w, j], device_id=(px, py, c), device_id_type=MESH)

        def to_pair(w, j, core):
            px, py = chips[j]
            return pltpu.make_async_remote_copy(
                src_ref=part(w, core, 2 * px + py), dst_ref=part(w, core, 2 * px + py), send_sem=pair_send.at[w, j],
                recv_sem=pair_recv.at[w, j], device_id=(x, y, 1 - c), device_id_type=MESH)

        for w in range(n):
            for j in range(3):
                over_ici(w, j, mine).start()
        for w in range(n):
            for j, (px, py) in enumerate(chips):
                over_ici(w, j, 2 * px + py).wait_recv()
                to_pair(w, j, c).start()
        for w in range(n):
            for j in range(3):
                to_pair(w, j, 1 - c).wait_recv()
                to_pair(w, j, c).wait_send()
                over_ici(w, j, mine).wait_send()

    return pl.pallas_call(
        body, in_specs=[_ANY] * n, out_specs=[_ANY] * n, out_shape=[_sds(b.shape, b.dtype) for b in bufs],
        input_output_aliases={i: i for i in range(n)},
        scratch_shapes=[pltpu.SemaphoreType.DMA((n, 3))] * 4,
        name="gather_weights", compiler_params=_params())(*bufs)


def pair_exchange(grads):
    n = len(grads)

    def body(*refs):
        ins, outs = refs[:n], refs[n:2 * n]
        send_sems, recv_sems = refs[2 * n:]
        x, y, c = _place()
        cps = []
        for w in range(n):
            lh = ins[w].shape[0] // 2
            cp = pltpu.make_async_remote_copy(
                src_ref=ins[w].at[pl.ds((1 - c) * lh, lh)], dst_ref=outs[w], send_sem=send_sems.at[w],
                recv_sem=recv_sems.at[w], device_id=(x, y, 1 - c), device_id_type=MESH)
            cp.start()
            cps.append(cp)
        for cp in cps:
            cp.wait_send()
            cp.wait_recv()

    out_shape = [_sds((g.shape[0] // 2,) + g.shape[1:], g.dtype) for g in grads]
    return pl.pallas_call(
        body, in_specs=[_ANY] * n, out_specs=[_ANY] * n, out_shape=out_shape,
        scratch_shapes=[pltpu.SemaphoreType.DMA((n,)), pltpu.SemaphoreType.DMA((n,))],
        name="pair_exchange", compiler_params=_params())(*grads)


def chip_exchange(parts):
    n = len(parts)

    def body(*refs):
        ins, outs = refs[:n], refs[n:2 * n]
        send_sems, recv_sems = refs[2 * n:]
        x, y, c = _place()
        mine = 2 * x + y
        chips = _other_chips(x, y)

        def copy(w, j, slot):
            px, py = chips[j]
            return pltpu.make_async_remote_copy(
                src_ref=ins[w].at[:, 2 * px + py], dst_ref=outs[w].at[slot], send_sem=send_sems.at[w, j],
                recv_sem=recv_sems.at[w, j], device_id=(px, py, c), device_id_type=MESH)

        for w in range(n):
            for j in range(3):
                copy(w, j, mine).start()
        for w in range(n):
            for j, (px, py) in enumerate(chips):
                copy(w, j, 2 * px + py).wait_recv()
                copy(w, j, mine).wait_send()

    out_shape = [_sds((N_CHIPS, g.shape[0]) + g.shape[2:], g.dtype) for g in parts]
    return pl.pallas_call(
        body, in_specs=[_ANY] * n, out_specs=[_ANY] * n, out_shape=out_shape,
        scratch_shapes=[pltpu.SemaphoreType.DMA((n, 3)), pltpu.SemaphoreType.DMA((n, 3))],
        name="chip_exchange", compiler_params=_params())(*parts)


def pair_share(halves):
    n = len(halves)

    def body(*refs):
        ins, outs = refs[:n], refs[n:2 * n]
        send_sems, recv_sems = refs[2 * n:]
        x, y, c = _place()
        cps = []
        for w in range(n):
            cp = pltpu.make_async_remote_copy(
                src_ref=ins[w], dst_ref=outs[w], send_sem=send_sems.at[w], recv_sem=recv_sems.at[w],
                device_id=(x, y, 1 - c), device_id_type=MESH)
            cp.start()
            cps.append(cp)
        for cp in cps:
            cp.wait_send()
            cp.wait_recv()

    return pl.pallas_call(
        body, in_specs=[_ANY] * n, out_specs=[_ANY] * n, out_shape=[_sds(g.shape, g.dtype) for g in halves],
        scratch_shapes=[pltpu.SemaphoreType.DMA((n,)), pltpu.SemaphoreType.DMA((n,))],
        name="pair_share", compiler_params=_params())(*halves)


def all_reduce_small(vec):
    r, lanes = vec.shape

    def body(v_ref, o_ref, buf, send_sems, recv_sems):
        x, y, c = _place()
        me = 4 * x + 2 * y + c
        buf[me] = v_ref[...]
        cps = []
        for k in range(1, N_DEV):
            peer = (me + k) % N_DEV
            cp = pltpu.make_async_remote_copy(
                src_ref=v_ref, dst_ref=buf.at[me], send_sem=send_sems.at[k - 1], recv_sem=recv_sems.at[k - 1],
                device_id=(peer // 4, (peer // 2) % 2, peer % 2), device_id_type=MESH)
            cp.start()
            cps.append(cp)
        for k in range(1, N_DEV):
            src = (me + N_DEV - k) % N_DEV
            cps[k - 1].wait_send()
            pltpu.make_async_remote_copy(
                src_ref=v_ref, dst_ref=buf.at[src], send_sem=send_sems.at[k - 1], recv_sem=recv_sems.at[k - 1],
                device_id=(src // 4, (src // 2) % 2, src % 2), device_id_type=MESH).wait_recv()
        acc = buf[0]
        for d in range(1, N_DEV):
            acc = acc + buf[d]
        o_ref[...] = acc

    vm = pl.BlockSpec(memory_space=pltpu.VMEM)
    return pl.pallas_call(
        body, in_specs=[vm], out_specs=vm, out_shape=_sds((r, lanes), F32),
        scratch_shapes=[pltpu.VMEM((N_DEV, r, lanes), F32), pltpu.SemaphoreType.DMA((N_DEV - 1,)),
                        pltpu.SemaphoreType.DMA((N_DEV - 1,))],
        name="all_reduce_small", compiler_params=_params())(vec)


ELEMENTWISE_BLOCK_BYTES = 1 << 20


def _flat_tile(rows, cols):
    for t in (512, 256, 128, 64, 32, 16, 8):
        if rows % t == 0 and t * cols * 4 <= ELEMENTWISE_BLOCK_BYTES:
            return t
    return rows


def add_pair(g, r, half):
    lyr, _, a, b = g.shape
    lh = lyr // 2
    rows = lh * 4 * a
    t = _flat_tile(rows, b)
    nb = rows // t
    g2 = g.reshape(lyr * 4 * a, b)
    r2 = r.reshape(rows, b)

    def body(half_ref, g_ref, r_ref, o_ref):
        o_ref[...] = (g_ref[...] + r_ref[...]).astype(BF16)

    grid_spec = pltpu.PrefetchScalarGridSpec(
        num_scalar_prefetch=1, grid=(nb,),
        in_specs=[pl.BlockSpec((t, b), lambda i, hr: (hr[0] * nb + i, 0)), pl.BlockSpec((t, b), lambda i, hr: (i, 0))],
        out_specs=pl.BlockSpec((t, b), lambda i, hr: (i, 0)))
    out = pl.pallas_call(body, grid_spec=grid_spec, out_shape=_sds((rows, b), BF16), name="add_pair",
                         compiler_params=_params(("parallel",)))(half, g2, r2)
    return out.reshape(lh, 4, a, b)


def sum_chips(own, parts, chip):
    _, lh, a, b = parts.shape
    t = _flat_tile(a, b)

    def body(chip_ref, own_ref, p_ref, o_ref):
        mine = chip_ref[0]

        def term(s):
            return jnp.where(mine == s, own_ref[...], p_ref[s]).astype(F32)

        o_ref[...] = ((term(0) + term(1)) + term(2)) + term(3)

    grid_spec = pltpu.PrefetchScalarGridSpec(
        num_scalar_prefetch=1, grid=(lh, a // t),
        in_specs=[pl.BlockSpec((None, None, t, b), lambda l, i, cr: (l, cr[0], i, 0)),
                  pl.BlockSpec((N_CHIPS, None, t, b), lambda l, i, cr: (0, l, i, 0))],
        out_specs=pl.BlockSpec((None, t, b), lambda l, i, cr: (l, i, 0)))
    return pl.pallas_call(body, grid_spec=grid_spec, out_shape=_sds((lh, a, b), F32), name="sum_chips",
                          compiler_params=_params(("parallel", "parallel")))(chip, own, parts)


def _adamw_math(w, g, m, v):
    mm = ADAM_B1 * m + (1.0 - ADAM_B1) * g
    vv = ADAM_B2 * v + (1.0 - ADAM_B2) * jnp.square(g)
    m_hat = mm / (1.0 - ADAM_B1 ** ADAM_STEP)
    v_hat = vv / (1.0 - ADAM_B2 ** ADAM_STEP)
    return -ADAM_LR * (m_hat / (jnp.sqrt(v_hat) + ADAM_EPS) + ADAM_WD * w), mm, vv


def adamw(w, g, m, v):
    shape = w.shape
    c = shape[-1]
    rows = int(np.prod(shape[:-1])) if len(shape) > 1 else 1
    t = _flat_tile(rows, c)
    flat = lambda z: z.reshape(rows, c)

    def body(w_ref, g_ref, m_ref, v_ref, d_ref, nm_ref, nv_ref):
        d_ref[...], nm_ref[...], nv_ref[...] = _adamw_math(w_ref[...], g_ref[...], m_ref[...], v_ref[...])

    spec = pl.BlockSpec((t, c), lambda i: (i, 0))
    outs = pl.pallas_call(body, grid=(rows // t,), in_specs=[spec] * 4, out_specs=[spec] * 3,
                          out_shape=[_sds((rows, c), F32)] * 3, name="adamw",
                          compiler_params=_params(("parallel",)))(flat(w), flat(g), flat(m), flat(v))
    return tuple(o.reshape(shape) for o in outs)


def adamw_halves(w, g_own, g_other, m, v, core):
    lyr, a, b = w.shape
    lh = lyr // 2
    t = _flat_tile(a, b)

    def body(core_ref, w_ref, go_ref, gs_ref, m_ref, v_ref, g_ref, d_ref, nm_ref, nv_ref):
        own = pl.program_id(0) // lh == core_ref[0]
        g = jnp.where(own, go_ref[...], gs_ref[...])
        g_ref[...] = g
        d_ref[...], nm_ref[...], nv_ref[...] = _adamw_math(w_ref[...], g, m_ref[...], v_ref[...])

    full = pl.BlockSpec((None, t, b), lambda l, i, cr: (l, i, 0))
    own_spec = pl.BlockSpec((None, t, b), lambda l, i, cr: (jnp.clip(l - cr[0] * lh, 0, lh - 1), i, 0))
    other_spec = pl.BlockSpec((None, t, b), lambda l, i, cr: (jnp.clip(l - (1 - cr[0]) * lh, 0, lh - 1), i, 0))
    grid_spec = pltpu.PrefetchScalarGridSpec(
        num_scalar_prefetch=1, grid=(lyr, a // t), in_specs=[full, own_spec, other_spec, full, full],
        out_specs=[full] * 4)
    return pl.pallas_call(body, grid_spec=grid_spec, out_shape=[_sds(w.shape, F32)] * 4, name="adamw_halves",
                          compiler_params=_params(("parallel", "parallel")))(core, w, g_own, g_other, m, v)


def _pack(parts):
    flat = jnp.concatenate([p.reshape(-1) for p in parts])
    pad = (-flat.shape[0]) % 1024
    return jnp.pad(flat, (0, pad)).reshape(-1, 128)


def _unpack(packed, shapes):
    flat = packed.reshape(-1)
    out, off = [], 0
    for shp in shapes:
        size = int(np.prod(shp))
        out.append(flat[off:off + size].reshape(shp))
        off += size
    return out


def kernel(x, mem, positions, norm_mix_g, w_in, b_gate, ret_gn_g, w_ret_out, conv_dw_w, conv_dw_b, conv_ln_g, conv_ln_b, w_conv_out, b_conv_out, w_mix_out, norm_xattn_g, norm_mem_g, w_xq, w_xkv, w_xo, norm_ffn_g, w_up, ffn_dw_w, ffn_dw_b, w_down, norm_final_g, loss_target, m_norm_mix_g, m_w_in, m_b_gate, m_ret_gn_g, m_w_ret_out, m_conv_dw_w, m_conv_dw_b, m_conv_ln_g, m_conv_ln_b, m_w_conv_out, m_b_conv_out, m_w_mix_out, m_norm_xattn_g, m_norm_mem_g, m_w_xq, m_w_xkv, m_w_xo, m_norm_ffn_g, m_w_up, m_ffn_dw_w, m_ffn_dw_b, m_w_down, m_norm_final_g, v_norm_mix_g, v_w_in, v_b_gate, v_ret_gn_g, v_w_ret_out, v_conv_dw_w, v_conv_dw_b, v_conv_ln_g, v_conv_ln_b, v_w_conv_out, v_b_conv_out, v_w_mix_out, v_norm_xattn_g, v_norm_mem_g, v_w_xq, v_w_xkv, v_w_xo, v_norm_ffn_g, v_w_up, v_ffn_dw_w, v_ffn_dw_b, v_w_down, v_norm_final_g):
    args = locals()
    w = {n: args[n] for n in WEIGHTS}
    m = {n: args["m_" + n] for n in WEIGHTS}
    v = {n: args["v_" + n] for n in WEIGHTS}
    chip = 2 * lax.axis_index("x") + lax.axis_index("y")
    core = lax.axis_index("c")

    chip_op = chip.reshape(1).astype(jnp.int32)
    core_op = core.reshape(1).astype(jnp.int32)
    gathered = gather_weights([place_shard(w[n], chip_op) for n in BIG])
    gw = dict(zip(BIG, gathered))

    sp = {n: w[n] for n in SMALL_REPL}
    placed = []
    for n in SMALL_SHARDED:
        cols = w[n].shape[-1]
        full = jnp.zeros(w[n].shape[:-1] + (N_CHIPS * cols,), F32)
        placed.append(lax.dynamic_update_slice_in_dim(full, w[n], chip * cols, axis=2))
    placed_shapes = [p.shape for p in placed]
    gathered_small = all_reduce_small(_pack([jnp.where(core == 0, p, 0.0) for p in placed]))
    for n, arr in zip(SMALL_SHARDED, _unpack(gathered_small, placed_shapes)):
        sp[n] = arr

    loss, grad_x, big, small = local_step(x[0], mem[0], positions.reshape(-1, 1), loss_target[0], gw, sp)

    names = [n for n in SMALL_REPL + SMALL_SHARDED]
    shapes = [small[n].shape for n in names] + [(128,)]
    reduced = _unpack(all_reduce_small(_pack([small[n] for n in names] + [loss.reshape(-1)])), shapes)
    grads = dict(zip(names, reduced[:-1]))
    loss_out = reduced[-1][0]
    for n in SMALL_SHARDED:
        cols = w[n].shape[-1]
        grads[n] = lax.dynamic_slice_in_dim(grads[n], chip * cols, cols, axis=2)

    blist = [big[n] for n in BIG]
    from_pair = pair_exchange(blist)
    pair_sum = [add_pair(g, r, core_op) for g, r in zip(blist, from_pair)]
    from_chips = chip_exchange(pair_sum)
    halves = [sum_chips(own, parts, chip_op) for own, parts in zip(pair_sum, from_chips)]
    other_halves = pair_share(halves)

    delta, new_m, new_v = {}, {}, {}
    for n, g_own, g_other in zip(BIG, halves, other_halves):
        grads[n], delta[n], new_m[n], new_v[n] = adamw_halves(w[n], g_own, g_other, m[n], v[n], core_op)
    for n in WEIGHTS:
        if n not in BIG:
            delta[n], new_m[n], new_v[n] = adamw(w[n], grads[n], m[n], v[n])
    return (loss_out, grad_x[None], *[grads[n] for n in WEIGHTS], *[delta[n] for n in WEIGHTS],
            *[new_m[n] for n in WEIGHTS], *[new_v[n] for n in WEIGHTS])
```

```python
import functools

import jax
import jax.numpy as jnp
import numpy as np
from jax import lax
from jax.experimental import pallas as pl
from jax.experimental.pallas import tpu as pltpu

F32 = jnp.float32
BF16 = jnp.bfloat16
MESH = pl.DeviceIdType.MESH

D_MODEL = 1024
CHUNK = 64
RET_HEADS = 4
RET_QK_DIM = 256
RET_V_DIM = 512
ROPE_THETA = 10000.0
CONV_WIDTH = 31
X_HEADS = 4
X_HEAD_DIM = 256
FFN_DIM = 2816
RMS_EPS = 1e-6
LN_EPS = 1e-5
ADAM_LR = 0.001
ADAM_B1 = 0.9
ADAM_B2 = 0.999
ADAM_EPS = 1e-08
ADAM_WD = 0.01
ADAM_STEP = 10

N_CHIPS = 4
N_DEV = 8
CONV_HALO = 32
FFN_HALO = 8
V7X_VMEM_LIMIT = 56 * 1024 * 1024
ROW_TILE = 256
STRIP_ROWS = 16
STRIP_LANES = 1024
DW_TAPS = 2
MM_TILE_M = 1024
RET_TILE = 512

BIG = ("w_in", "w_ret_out", "w_conv_out", "w_mix_out", "w_xq", "w_xkv", "w_xo", "w_up", "w_down")
COL_SHARDED = ("w_in", "w_xkv", "w_up")
SMALL_REPL = ("norm_mix_g", "b_gate", "ret_gn_g", "conv_dw_b", "conv_ln_g", "conv_ln_b", "b_conv_out",
              "norm_xattn_g", "norm_mem_g", "norm_ffn_g", "ffn_dw_b", "norm_final_g")
SMALL_SHARDED = ("conv_dw_w", "ffn_dw_w")
WEIGHTS = ('norm_mix_g', 'w_in', 'b_gate', 'ret_gn_g', 'w_ret_out', 'conv_dw_w', 'conv_dw_b', 'conv_ln_g',
           'conv_ln_b', 'w_conv_out', 'b_conv_out', 'w_mix_out', 'norm_xattn_g', 'norm_mem_g', 'w_xq', 'w_xkv',
           'w_xo', 'norm_ffn_g', 'w_up', 'ffn_dw_w', 'ffn_dw_b', 'w_down', 'norm_final_g')


def _params(sem=None):
    return pltpu.CompilerParams(dimension_semantics=sem, vmem_limit_bytes=V7X_VMEM_LIMIT)


def _sds(shape, dtype):
    return jax.ShapeDtypeStruct(tuple(shape), dtype)


def _sigmoid(x):
    return jax.nn.sigmoid(x)


def _dot(a, b, ca, cb):
    return lax.dot_general(a, b, (((ca,), (cb,)), ((), ())), preferred_element_type=F32)


def _nn(a, b):
    return _dot(a, b, 1, 0)


def _nt(a, b):
    return _dot(a, b, 1, 1)


def _tn(a, b):
    return _dot(a, b, 0, 0)


def _mm(name, dims, grid, in_specs, out_spec, out_sds, nk, operands, with_res=False):
    def body(*refs):
        if with_res:
            a_ref, b_ref, r_ref, o_ref = refs
        else:
            a_ref, b_ref, o_ref = refs
            r_ref = None
        prod = _dot(a_ref[...].astype(BF16), b_ref[...].astype(BF16), *dims)
        if nk == 1:
            if r_ref is not None:
                prod = prod + r_ref[...]
            o_ref[...] = prod.astype(o_ref.dtype)
        else:
            k = pl.program_id(2)

            @pl.when(k == 0)
            def _():
                o_ref[...] = (prod + r_ref[...]) if r_ref is not None else prod

            @pl.when(k > 0)
            def _():
                o_ref[...] += prod

    assert nk == 1 or out_sds.dtype == F32
    return pl.pallas_call(body, grid=grid, in_specs=in_specs, out_specs=out_spec, out_shape=out_sds, name=name,
                          compiler_params=_params(("parallel", "parallel", "arbitrary")))(*operands)


def _div_tile(n, want):
    best = None
    for t in range(128, min(n, want) + 1, 128):
        if n % t == 0:
            best = t
    assert best is not None, (n, want)
    return best


def mm_fwd(name, a, g, l, col, out_dtype=F32, res=None):
    m, k_dim = a.shape
    tm = min(MM_TILE_M, m)
    if col:
        _, _, kk, b = g.shape
        assert kk == k_dim
        tn = _div_tile(b, 1408)
        nps = b // tn
        n = 4 * b
        grid = (m // tm, n // tn, 1)
        in_specs = [pl.BlockSpec((tm, k_dim), lambda i, j, k: (i, 0)),
                    pl.BlockSpec((None, None, k_dim, tn), lambda i, j, k: (l, j // nps, 0, j % nps))]
        nk = 1
        w = g
    else:
        lyr, _, a_rows, n = g.shape
        assert 4 * a_rows == k_dim
        w = g.reshape(lyr, k_dim, n)
        tk = _div_tile(k_dim, 1408)
        tn = n
        nk = k_dim // tk
        grid = (m // tm, 1, nk)
        in_specs = [pl.BlockSpec((tm, tk), lambda i, j, k: (i, k)),
                    pl.BlockSpec((None, tk, tn), lambda i, j, k: (l, k, j))]
    ops = [a, w]
    if res is not None:
        in_specs.append(pl.BlockSpec((tm, tn), lambda i, j, k: (i, j)))
        ops.append(res)
    return _mm(name, (1, 0), grid, in_specs, pl.BlockSpec((tm, tn), lambda i, j, k: (i, j)), _sds((m, n), out_dtype),
               nk, ops, with_res=res is not None)


def mm_dx(name, dy, g, l, col):
    m, n = dy.shape
    tm = min(MM_TILE_M, m)
    if col:
        _, _, k_dim, b = g.shape
        assert 4 * b == n
        tk = _div_tile(b, 1408)
        nps = b // tk
        nk = n // tk
        grid = (m // tm, 1, nk)
        in_specs = [pl.BlockSpec((tm, tk), lambda i, j, k: (i, k)),
                    pl.BlockSpec((None, None, k_dim, tk), lambda i, j, k: (l, k // nps, 0, k % nps))]
        out_spec = pl.BlockSpec((tm, k_dim), lambda i, j, k: (i, 0))
        w = g
    else:
        lyr, _, a_rows, nn_ = g.shape
        assert nn_ == n
        k_dim = 4 * a_rows
        w = g.reshape(lyr, k_dim, n)
        tno = _div_tile(k_dim, 1408)
        nk = 1
        grid = (m // tm, k_dim // tno, 1)
        in_specs = [pl.BlockSpec((tm, n), lambda i, j, k: (i, 0)),
                    pl.BlockSpec((None, tno, n), lambda i, j, k: (l, j, 0))]
        out_spec = pl.BlockSpec((tm, tno), lambda i, j, k: (i, j))
    return _mm(name, (1, 1), grid, in_specs, out_spec, _sds((m, k_dim), F32), nk, [dy, w])


def mm_dw(name, a, dy, col, l, n_layers, into=None, a_is_t=False):
    k_dim, m = a.shape if a_is_t else a.shape[::-1]
    _, n = dy.shape
    ts = min(MM_TILE_M, m)
    ns = m // ts
    tko = _div_tile(k_dim, 1408)
    if col:
        b = n // 4
        tn = _div_tile(b, 1408)
        nps = b // tn
        grid = (k_dim // tko, n // tn, ns)
        out_spec = pl.BlockSpec((None, None, tko, tn), lambda i, j, s: (l, j // nps, i, j % nps))
        shape = (n_layers, 4, k_dim, b)
    else:
        tn = n
        grid = (k_dim // tko, 1, ns)
        out_spec = pl.BlockSpec((None, tko, tn), lambda i, j, s: (l, i, j))
        shape = (n_layers, k_dim, n)
    a_spec = pl.BlockSpec((tko, ts), lambda i, j, s: (i, s)) if a_is_t else pl.BlockSpec((ts, tko), lambda i, j, s: (s, i))
    in_specs = [a_spec, pl.BlockSpec((ts, tn), lambda i, j, s: (s, j))]
    ops = [a, dy]
    aliases = {}
    if into is not None:
        in_specs.append(_ANY)
        ops.append(into.reshape(shape))
        aliases = {2: 0}

    def body(a_ref, b_ref, *rest):
        o_ref = rest[-1]
        prod = (_nn if a_is_t else _tn)(a_ref[...].astype(BF16), b_ref[...].astype(BF16))
        if ns == 1:
            o_ref[...] = prod
        else:
            s = pl.program_id(2)

            @pl.when(s == 0)
            def _():
                o_ref[...] = prod

            @pl.when(s > 0)
            def _():
                o_ref[...] += prod

    out = pl.pallas_call(body, grid=grid, in_specs=in_specs, out_specs=out_spec, out_shape=_sds(shape, F32),
                         input_output_aliases=aliases, name=name,
                         compiler_params=_params(("parallel", "parallel", "arbitrary")))(*ops)
    return out.reshape(n_layers, 4, k_dim if col else k_dim // 4, shape[-1])


SEGMENT_BLOCK = 512


def _segment_blocks(segs):
    out, start = [], 0
    for sg in segs:
        n = sg.shape[1] // SEGMENT_BLOCK
        out.append((start, n))
        start += n
    return out, start


def mm_in_dx(segs, g, l):
    m = segs[0].shape[0]
    tm = min(MM_TILE_M, m)
    _, _, k_dim, b = g.shape
    tk = SEGMENT_BLOCK
    nps = b // tk
    blocks, nk = _segment_blocks(segs)
    assert nk * tk == 4 * b

    def body(*refs):
        seg_refs, w_ref, o_ref = refs[:-2], refs[-2], refs[-1]
        k = pl.program_id(2)
        for seg_ref, (s0, n) in zip(seg_refs, blocks):
            @pl.when((k >= s0) & (k < s0 + n))
            def _(seg_ref=seg_ref):
                prod = _nt(seg_ref[...], w_ref[...])

                @pl.when(k == 0)
                def _():
                    o_ref[...] = prod

                @pl.when(k > 0)
                def _():
                    o_ref[...] += prod

    in_specs = [pl.BlockSpec((tm, tk), lambda i, j, k, s0=s0, n=n: (i, jnp.clip(k - s0, 0, n - 1)))
                for s0, n in blocks]
    in_specs.append(pl.BlockSpec((None, None, k_dim, tk), lambda i, j, k: (l, k // nps, 0, k % nps)))
    return pl.pallas_call(
        body, grid=(m // tm, 1, nk), in_specs=in_specs, out_specs=pl.BlockSpec((tm, k_dim), lambda i, j, k: (i, 0)),
        out_shape=_sds((m, k_dim), F32), name="mm_in_dx",
        compiler_params=_params(("parallel", "parallel", "arbitrary")))(*segs, g)


def mm_in_dw(u_t, segs, l, n_layers, into=None):
    k_dim, m = u_t.shape
    ts = min(MM_TILE_M, m)
    ns = m // ts
    tn = SEGMENT_BLOCK
    blocks, nj = _segment_blocks(segs)
    b = nj * tn // 4
    nps = b // tn
    shape = (n_layers, 4, k_dim, b)

    def body(*refs):
        u_ref, seg_refs, o_ref = refs[0], refs[1:1 + len(segs)], refs[-1]
        j = pl.program_id(1)
        s = pl.program_id(2)
        for seg_ref, (s0, n) in zip(seg_refs, blocks):
            @pl.when((j >= s0) & (j < s0 + n))
            def _(seg_ref=seg_ref):
                prod = _nn(u_ref[...], seg_ref[...])

                @pl.when(s == 0)
                def _():
                    o_ref[...] = prod

                @pl.when(s > 0)
                def _():
                    o_ref[...] += prod

    def seg_map(s0, n):
        return lambda i, j, s: (jnp.where((j >= s0) & (j < s0 + n), s, 0), jnp.clip(j - s0, 0, n - 1))

    in_specs = [pl.BlockSpec((k_dim, ts), lambda i, j, s: (0, s))]
    in_specs += [pl.BlockSpec((ts, tn), seg_map(s0, n)) for s0, n in blocks]
    ops = [u_t, *segs]
    aliases = {}
    if into is not None:
        in_specs.append(_ANY)
        ops.append(into)
        aliases = {len(ops) - 1: 0}
    return pl.pallas_call(
        body, grid=(1, nj, ns), in_specs=in_specs,
        out_specs=pl.BlockSpec((None, None, k_dim, tn), lambda i, j, s: (l, j // nps, 0, j % nps)),
        out_shape=_sds(shape, F32), input_output_aliases=aliases, name="mm_in_dw",
        compiler_params=_params(("parallel", "parallel", "arbitrary")))(*ops)


def _row_spec(t, c, col=0):
    return pl.BlockSpec((t, c), lambda i: (i, col))


def _vec_spec(c):
    return pl.BlockSpec((1, c), lambda i: (0, 0))


def _acc_rows(ref, i, val):
    @pl.when(i == 0)
    def _():
        ref[...] = val

    @pl.when(i > 0)
    def _():
        ref[...] += val


def rope_tables(positions, inv_freq):
    s = positions.shape[0]
    t = min(ROW_TILE, s)
    half = inv_freq.shape[1]

    def body(p_ref, f_ref, c_ref, s_ref):
        ang = p_ref[...].astype(F32) * f_ref[...]
        c_ref[...] = jnp.cos(ang)
        s_ref[...] = jnp.sin(ang)

    return pl.pallas_call(
        body, grid=(s // t,), in_specs=[_row_spec(t, 1), _vec_spec(half)],
        out_specs=[_row_spec(t, half), _row_spec(t, half)], out_shape=[_sds((s, half), F32)] * 2, name="rope_tables",
        compiler_params=_params(("parallel",)))(positions, inv_freq)


def rms_cast(h, g, with_transpose=False):
    s, d = h.shape
    t = min(ROW_TILE, s)

    def body(h_ref, g_ref, o_ref, *t_ref):
        x = h_ref[...]
        r = lax.rsqrt(jnp.mean(x * x, axis=-1, keepdims=True) + RMS_EPS)
        y = (x * r * g_ref[...]).astype(BF16)
        o_ref[...] = y
        if with_transpose:
            t_ref[0][...] = y.T

    out_specs = [_row_spec(t, d)]
    out_shape = [_sds((s, d), BF16)]
    if with_transpose:
        out_specs.append(pl.BlockSpec((d, t), lambda i: (0, i)))
        out_shape.append(_sds((d, s), BF16))
    out = pl.pallas_call(body, grid=(s // t,), in_specs=[_row_spec(t, d), _vec_spec(d)], out_specs=out_specs,
                         out_shape=out_shape, name="rms_cast_t" if with_transpose else "rms_cast",
                         compiler_params=_params(("parallel",)))(h, g)
    return out if with_transpose else out[0]


def _rms_bwd_math(x, g, du):
    r = lax.rsqrt(jnp.mean(x * x, axis=-1, keepdims=True) + RMS_EPS)
    gd = g * du
    dx = r * gd - x * (r * r * r) * jnp.mean(x * gd, axis=-1, keepdims=True)
    dg = jnp.sum(x * r * du, axis=0, keepdims=True)
    return dx, dg


def rms_bwd(h, g, du, dres=None):
    s, d = h.shape
    t = min(ROW_TILE, s)

    def body(*refs):
        if dres is None:
            h_ref, g_ref, du_ref, dh_ref, dg_ref = refs
        else:
            h_ref, g_ref, du_ref, dr_ref, dh_ref, dg_ref = refs
        dx, dg = _rms_bwd_math(h_ref[...], g_ref[...], du_ref[...])
        if dres is not None:
            dx = dx + dr_ref[...]
        dh_ref[...] = dx
        _acc_rows(dg_ref, pl.program_id(0), dg)

    in_specs = [_row_spec(t, d), _vec_spec(d), _row_spec(t, d)]
    ops = [h, g, du]
    if dres is not None:
        in_specs.append(_row_spec(t, d))
        ops.append(dres)
    return pl.pallas_call(body, grid=(s // t,), in_specs=in_specs, out_specs=[_row_spec(t, d), _vec_spec(d)],
                          out_shape=[_sds((s, d), F32), _sds((1, d), F32)], name="rms_bwd",
                          compiler_params=_params(("arbitrary",)))(*ops)


def loss_head(h, g, target):
    s, d = h.shape
    t = min(ROW_TILE, s)

    def body(h_ref, g_ref, t_ref, dh_ref, dg_ref, loss_ref):
        x = h_ref[...]
        gg = g_ref[...]
        r = lax.rsqrt(jnp.mean(x * x, axis=-1, keepdims=True) + RMS_EPS)
        err = x * r * gg - t_ref[...]
        part = 0.5 * jnp.sum(jnp.mean(err * err, axis=-1, keepdims=True), axis=0, keepdims=True)
        dy = err * (1.0 / d)
        dx, dg = _rms_bwd_math(x, gg, dy)
        dh_ref[...] = dx
        i = pl.program_id(0)
        _acc_rows(dg_ref, i, dg)
        _acc_rows(loss_ref, i, jnp.broadcast_to(part, (1, 128)))

    return pl.pallas_call(
        body, grid=(s // t,), in_specs=[_row_spec(t, d), _vec_spec(d), _row_spec(t, d)],
        out_specs=[_row_spec(t, d), _vec_spec(d), _vec_spec(128)],
        out_shape=[_sds((s, d), F32), _sds((1, d), F32), _sds((1, 128), F32)], name="loss_head",
        compiler_params=_params(("arbitrary",)))(h, g, target)


def _rot(x, cos, sin):
    half = x.shape[-1] // 2
    x1, x2 = x[:, :half], x[:, half:]
    return jnp.concatenate([x1 * cos - x2 * sin, x2 * cos + x1 * sin], axis=-1)


def _rot_t(dy, cos, sin):
    half = dy.shape[-1] // 2
    d1, d2 = dy[:, :half], dy[:, half:]
    return jnp.concatenate([d1 * cos + d2 * sin, d2 * cos - d1 * sin], axis=-1)


def _decay_tables():
    log_gamma = jnp.log(1.0 - jnp.power(2.0, -5.0 - jnp.arange(RET_HEADS, dtype=F32)))
    idx = jnp.arange(CHUNK, dtype=F32)
    dist = jnp.abs(idx[:, None] - idx[None, :])
    d_inner = jnp.exp(log_gamma[:, None, None] * dist)
    decay_q = jnp.exp(log_gamma[None, :] * (idx[:, None] + 1.0))
    decay_k = jnp.exp(log_gamma[None, :] * (CHUNK - 1.0 - idx[:, None]))
    decay_chunk = jnp.exp(log_gamma * CHUNK)
    return d_inner, decay_q.T[:, :, None], decay_k.T[:, :, None], decay_chunk[:, None, None]


_QK_SCALE = RET_QK_DIM ** -0.5


def retention_fwd(p, cos, sin, gn_g, tables):
    s = p.shape[0]
    t = min(RET_TILE, s)
    nc = t // CHUNK
    nt = s // t
    dk, dv = RET_QK_DIM, RET_V_DIM
    d_inner, decay_q, decay_k, decay_chunk = tables

    def body(q_ref, k_ref, v_ref, gr_ref, cos_ref, sin_ref, gn_ref, di_ref, dq_ref, dkk_ref, dc_ref,
             o_ref, z_ref, st_ref, state):
        i = pl.program_id(1)

        @pl.when(i == 0)
        def _():
            state[...] = jnp.zeros_like(state)

        cs, sn = cos_ref[...], sin_ref[...]
        qr = _rot(q_ref[...], cs, sn) * _QK_SCALE
        kr = _rot(k_ref[...], cs, sn)
        dmat, dq, dkk, gam = di_ref[...], dq_ref[...], dkk_ref[...], dc_ref[...]
        for c in range(nc):
            sl = slice(c * CHUNK, (c + 1) * CHUNK)
            qc = qr[sl].astype(BF16)
            kc = kr[sl].astype(BF16)
            vc = v_ref[sl, :].astype(BF16)
            scores = _nt(qc, kc) * dmat
            st = state[...].astype(BF16)
            st_ref[c] = st
            o = _nn(scores.astype(BF16), vc) + _nn(qc, st) * dq
            kd = (kr[sl] * dkk).astype(BF16)
            state[...] = state[...] * gam + _tn(kd, vc)
            o_ref[sl, :] = o
            mu = jnp.mean(o, axis=-1, keepdims=True)
            oc = o - mu
            var = jnp.mean(oc * oc, axis=-1, keepdims=True)
            y = oc * lax.rsqrt(var + LN_EPS) * gn_ref[...]
            gr = gr_ref[sl, :]
            z_ref[sl, :] = (gr * _sigmoid(gr) * y).astype(BF16)

    hmap = lambda h, i: (h, 0, 0)
    in_specs = [
        pl.BlockSpec((t, dk), lambda h, i: (i, h)),
        pl.BlockSpec((t, dk), lambda h, i: (i, RET_HEADS + h)),
        pl.BlockSpec((t, dv), lambda h, i: (i, 4 + h)),
        pl.BlockSpec((t, dv), lambda h, i: (i, 8 + h)),
        pl.BlockSpec((t, dk // 2), lambda h, i: (i, 0)),
        pl.BlockSpec((t, dk // 2), lambda h, i: (i, 0)),
        pl.BlockSpec((1, dv), lambda h, i: (0, h)),
        pl.BlockSpec((None, CHUNK, CHUNK), hmap),
        pl.BlockSpec((None, CHUNK, 1), hmap),
        pl.BlockSpec((None, CHUNK, 1), hmap),
        pl.BlockSpec((None, 1, 1), hmap),
    ]
    out_specs = [pl.BlockSpec((t, dv), lambda h, i: (i, h)),
                 pl.BlockSpec((t, dv), lambda h, i: (i, h)),
                 pl.BlockSpec((None, nc, dk, dv), lambda h, i: (h, i, 0, 0))]
    out_shape = [_sds((s, RET_HEADS * dv), F32), _sds((s, RET_HEADS * dv), BF16),
                 _sds((RET_HEADS, s // CHUNK, dk, dv), BF16)]
    return pl.pallas_call(
        body, grid=(RET_HEADS, nt), in_specs=in_specs, out_specs=out_specs, out_shape=out_shape,
        scratch_shapes=[pltpu.VMEM((dk, dv), F32)], name="retention_fwd",
        compiler_params=_params(("parallel", "arbitrary")))(p, p, p, p, cos, sin, gn_g, d_inner, decay_q, decay_k,
                                                            decay_chunk)


def gn_gate_bwd(o, p, gn_g, dz):
    s = o.shape[0]
    t = min(ROW_TILE, s)
    dv = RET_V_DIM
    w = RET_HEADS * dv

    def body(o_ref, gr_ref, gn_ref, dz_ref, do_ref, dgr_ref, dgn_ref):
        dgn_parts = []
        for h in range(RET_HEADS):
            sl = slice(h * dv, (h + 1) * dv)
            oo = o_ref[:, sl]
            gr = gr_ref[:, sl]
            dz = dz_ref[:, sl]
            gn = gn_ref[:, sl]
            mu = jnp.mean(oo, axis=-1, keepdims=True)
            oc = oo - mu
            rstd = lax.rsqrt(jnp.mean(oc * oc, axis=-1, keepdims=True) + LN_EPS)
            y = oc * rstd
            sg = _sigmoid(gr)
            act = gr * sg
            dyg = dz * act
            dgn_parts.append(jnp.sum(dyg * y, axis=0, keepdims=True))
            dy = dyg * gn
            do_ref[:, sl] = rstd * (dy - jnp.mean(dy, axis=-1, keepdims=True)
                                    - y * jnp.mean(dy * y, axis=-1, keepdims=True))
            dgr_ref[:, sl] = (dz * (y * gn) * (sg * (1.0 + gr * (1.0 - sg)))).astype(BF16)
        _acc_rows(dgn_ref, pl.program_id(0), jnp.concatenate(dgn_parts, axis=-1))

    return pl.pallas_call(
        body, grid=(s // t,), in_specs=[_row_spec(t, w), _row_spec(t, w, 2), _vec_spec(w), _row_spec(t, w)],
        out_specs=[_row_spec(t, w), _row_spec(t, w), _vec_spec(w)],
        out_shape=[_sds((s, w), F32), _sds((s, w), BF16), _sds((1, w), F32)], name="gn_gate_bwd",
        compiler_params=_params(("arbitrary",)))(o, p, gn_g, dz)


def retention_bwd(p, cos, sin, states, do, tables):
    s = p.shape[0]
    t = min(RET_TILE, s)
    nc = t // CHUNK
    nt = s // t
    dk, dv = RET_QK_DIM, RET_V_DIM
    d_inner, decay_q, decay_k, decay_chunk = tables

    def body(q_ref, k_ref, v_ref, cos_ref, sin_ref, st_ref, do_ref, di_ref, dq_ref, dkk_ref, dc_ref,
             gq_ref, gk_ref, gv_ref, dstate):
        i = pl.program_id(1)

        @pl.when(i == 0)
        def _():
            dstate[...] = jnp.zeros_like(dstate)

        cs, sn = cos_ref[...], sin_ref[...]
        qr = _rot(q_ref[...], cs, sn) * _QK_SCALE
        kr = _rot(k_ref[...], cs, sn)
        dmat, dq, dkk, gam = di_ref[...], dq_ref[...], dkk_ref[...], dc_ref[...]
        for c in range(nc - 1, -1, -1):
            sl = slice(c * CHUNK, (c + 1) * CHUNK)
            qc = qr[sl].astype(BF16)
            kc = kr[sl].astype(BF16)
            vc = v_ref[sl, :].astype(BF16)
            kd = (kr[sl] * dkk).astype(BF16)
            st = st_ref[c]
            d_o = do_ref[sl, :]
            dob = d_o.astype(BF16)
            ab = (_nt(qc, kc) * dmat).astype(BF16)
            dsb = dstate[...].astype(BF16)
            dvv = _tn(ab, dob) + _nn(kd, dsb)
            dkd = _nt(vc, dsb)
            dcb = (d_o * dq).astype(BF16)
            dpb = (_nt(dob, vc) * dmat).astype(BF16)
            dqq = _nt(dcb, st) + _nn(dpb, kc)
            dkv = _tn(dpb, qc) + dkd * dkk
            dstate[...] = dstate[...] * gam + _tn(qc, dcb)
            gq_ref[sl, :] = _rot_t(dqq * _QK_SCALE, cs[sl], sn[sl]).astype(BF16)
            gk_ref[sl, :] = _rot_t(dkv, cs[sl], sn[sl]).astype(BF16)
            gv_ref[sl, :] = dvv.astype(BF16)

    hmap = lambda h, i: (h, 0, 0)
    rev = lambda i: nt - 1 - i
    in_specs = [
        pl.BlockSpec((t, dk), lambda h, i: (rev(i), h)),
        pl.BlockSpec((t, dk), lambda h, i: (rev(i), RET_HEADS + h)),
        pl.BlockSpec((t, dv), lambda h, i: (rev(i), 4 + h)),
        pl.BlockSpec((t, dk // 2), lambda h, i: (rev(i), 0)),
        pl.BlockSpec((t, dk // 2), lambda h, i: (rev(i), 0)),
        pl.BlockSpec((None, nc, dk, dv), lambda h, i: (h, rev(i), 0, 0)),
        pl.BlockSpec((t, dv), lambda h, i: (rev(i), h)),
        pl.BlockSpec((None, CHUNK, CHUNK), hmap),
        pl.BlockSpec((None, CHUNK, 1), hmap),
        pl.BlockSpec((None, CHUNK, 1), hmap),
        pl.BlockSpec((None, 1, 1), hmap),
    ]
    out_specs = [pl.BlockSpec((t, dk), lambda h, i: (rev(i), h)),
                 pl.BlockSpec((t, dk), lambda h, i: (rev(i), h)),
                 pl.BlockSpec((t, dv), lambda h, i: (rev(i), h))]
    out_shape = [_sds((s, RET_HEADS * dk), BF16), _sds((s, RET_HEADS * dk), BF16), _sds((s, RET_HEADS * dv), BF16)]
    return pl.pallas_call(
        body, grid=(RET_HEADS, nt), in_specs=in_specs, out_specs=out_specs, out_shape=out_shape,
        scratch_shapes=[pltpu.VMEM((dk, dv), F32)], name="retention_bwd",
        compiler_params=_params(("parallel", "arbitrary")))(p, p, p, cos, sin, states, do, d_inner, decay_q, decay_k,
                                                            decay_chunk)


A_COL, B_COL = 6, 7


def _prev_rows_spec(t, halo, width, col):
    per = t // halo
    return pl.BlockSpec((halo, width), lambda i: (jnp.maximum(i * per - 1, 0), col))


def _next_rows_spec(t, halo, width, col, n_rows):
    per = t // halo
    last = n_rows // halo - 1
    return pl.BlockSpec((halo, width), lambda i: (jnp.minimum((i + 1) * per, last), col))


def _shifted_copies(ext, rows):
    for b in range(1, 8):
        ext[b, pl.ds(0, rows - 8), :] = ext[0, pl.ds(b, rows - 8), :]


def _shifted(ext, start, lanes):
    return ext[start % 8, pl.ds(start - start % 8, STRIP_ROWS), lanes]


def conv_fwd(p, dw_w, dw_b, ln_g, ln_b):
    s = p.shape[0]
    t = min(ROW_TILE, s)
    c = D_MODEL
    hl = CONV_HALO

    def body(a_ref, b_ref, ah_ref, bh_ref, w_ref, wb_ref, g_ref, bb_ref, c1_ref, c3_ref, ext):
        i = pl.program_id(0)
        ext[0, pl.ds(0, hl), :] = jnp.where(i > 0, ah_ref[...] * _sigmoid(bh_ref[...]), 0.0)
        ext[0, pl.ds(hl, t), :] = a_ref[...] * _sigmoid(b_ref[...])
        _shifted_copies(ext, t + hl)
        first = hl - (CONV_WIDTH - 1)
        for lane in range(0, c, STRIP_LANES):
            ls = slice(lane, lane + STRIP_LANES)
            for r0 in range(0, t, STRIP_ROWS):
                accs = [jnp.broadcast_to(wb_ref[:, ls], (STRIP_ROWS, STRIP_LANES)),
                        jnp.zeros((STRIP_ROWS, STRIP_LANES), F32)]
                for j in range(CONV_WIDTH):
                    accs[j % 2] = accs[j % 2] + w_ref[j:j + 1, ls] * _shifted(ext, r0 + first + j, ls)
                c1_ref[r0:r0 + STRIP_ROWS, ls] = accs[0] + accs[1]
        acc = c1_ref[...]
        mu = jnp.mean(acc, axis=-1, keepdims=True)
        xc = acc - mu
        var = jnp.mean(xc * xc, axis=-1, keepdims=True)
        c2 = xc * lax.rsqrt(var + LN_EPS) * g_ref[...] + bb_ref[...]
        c3_ref[...] = (c2 * _sigmoid(c2)).astype(BF16)

    in_specs = [_row_spec(t, c, A_COL), _row_spec(t, c, B_COL),
                _prev_rows_spec(t, hl, c, A_COL), _prev_rows_spec(t, hl, c, B_COL),
                pl.BlockSpec((CONV_WIDTH, c), lambda i: (0, 0)), _vec_spec(c), _vec_spec(c), _vec_spec(c)]
    return pl.pallas_call(
        body, grid=(s // t,), in_specs=in_specs, out_specs=[_row_spec(t, c), _row_spec(t, c)],
        out_shape=[_sds((s, c), F32), _sds((s, c), BF16)], scratch_shapes=[pltpu.VMEM((8, t + hl, c), F32)],
        name="conv_fwd", compiler_params=_params(("parallel",)))(p, p, p, p, dw_w, dw_b, ln_g, ln_b)


def conv_ln_bwd(c1, ln_g, ln_b, dc3):
    s, c = c1.shape
    t = min(ROW_TILE, s)

    def body(c1_ref, g_ref, b_ref, d_ref, dc1_ref, dg_ref, db_ref):
        x = c1_ref[...]
        g = g_ref[...]
        mu = jnp.mean(x, axis=-1, keepdims=True)
        xc = x - mu
        rstd = lax.rsqrt(jnp.mean(xc * xc, axis=-1, keepdims=True) + LN_EPS)
        y = xc * rstd
        c2 = y * g + b_ref[...]
        sg = _sigmoid(c2)
        dc2 = d_ref[...] * (sg * (1.0 + c2 * (1.0 - sg)))
        i = pl.program_id(0)
        _acc_rows(db_ref, i, jnp.sum(dc2, axis=0, keepdims=True))
        _acc_rows(dg_ref, i, jnp.sum(dc2 * y, axis=0, keepdims=True))
        dy = dc2 * g
        dc1_ref[...] = rstd * (dy - jnp.mean(dy, axis=-1, keepdims=True)
                               - y * jnp.mean(dy * y, axis=-1, keepdims=True))

    return pl.pallas_call(
        body, grid=(s // t,), in_specs=[_row_spec(t, c), _vec_spec(c), _vec_spec(c), _row_spec(t, c)],
        out_specs=[_row_spec(t, c), _vec_spec(c), _vec_spec(c)],
        out_shape=[_sds((s, c), F32), _sds((1, c), F32), _sds((1, c), F32)], name="conv_ln_bwd",
        compiler_params=_params(("arbitrary",)))(c1, ln_g, ln_b, dc3)


def conv_dw_bwd(p, dc1, dw_w):
    s = p.shape[0]
    t = min(ROW_TILE, s)
    c = D_MODEL
    hl = CONV_HALO
    nt = s // t

    def body(a_ref, b_ref, ah_ref, bh_ref, d_ref, dn_ref, w_ref, dab_ref, dw_ref, dbias_ref, ext_c, ext_d, dc0_s,
             dw_s):
        i = pl.program_id(0)
        ext_c[0, pl.ds(0, hl), :] = jnp.where(i > 0, ah_ref[...] * _sigmoid(bh_ref[...]), 0.0)
        ext_c[0, pl.ds(hl, t), :] = a_ref[...] * _sigmoid(b_ref[...])
        ext_d[0, pl.ds(0, t), :] = d_ref[...]
        ext_d[0, pl.ds(t, hl), :] = jnp.where(i < nt - 1, dn_ref[...], 0.0)
        _shifted_copies(ext_c, t + hl)
        _shifted_copies(ext_d, t + hl)
        first = hl - (CONV_WIDTH - 1)

        def fold8(x):
            rows = [x[k:k + 8] for k in range(0, STRIP_ROWS, 8)]
            while len(rows) > 1:
                rows = [rows[k] + rows[k + 1] for k in range(0, len(rows), 2)]
            return rows[0]

        for lane in range(0, c, STRIP_LANES):
            ls = slice(lane, lane + STRIP_LANES)
            for r0 in range(0, t, STRIP_ROWS):
                accs = [jnp.zeros((STRIP_ROWS, STRIP_LANES), F32) for _ in range(2)]
                for j in range(CONV_WIDTH):
                    accs[j % 2] = accs[j % 2] + w_ref[j:j + 1, ls] * _shifted(ext_d, r0 + CONV_WIDTH - 1 - j, ls)
                dc0_s[r0:r0 + STRIP_ROWS, ls] = accs[0] + accs[1]
            for j0 in range(0, CONV_WIDTH, DW_TAPS):
                taps = range(j0, min(j0 + DW_TAPS, CONV_WIDTH))
                parts = [jnp.zeros((8, STRIP_LANES), F32) for _ in taps]
                for r0 in range(0, t, STRIP_ROWS):
                    d = ext_d[0, r0:r0 + STRIP_ROWS, ls]
                    for k, j in enumerate(taps):
                        parts[k] = parts[k] + fold8(d * _shifted(ext_c, r0 + first + j, ls))
                for k, j in enumerate(taps):
                    dw_s[j:j + 1, ls] = jnp.sum(parts[k], axis=0, keepdims=True)
        _acc_rows(dw_ref, i, dw_s[0:CONV_WIDTH, :])
        d = d_ref[...]
        _acc_rows(dbias_ref, i, jnp.sum(d, axis=0, keepdims=True))
        dc0 = dc0_s[...]
        a = a_ref[...]
        sb = _sigmoid(b_ref[...])
        dab_ref[:, :c] = (dc0 * sb).astype(BF16)
        dab_ref[:, c:] = (dc0 * a * sb * (1.0 - sb)).astype(BF16)

    in_specs = [_row_spec(t, c, A_COL), _row_spec(t, c, B_COL),
                _prev_rows_spec(t, hl, c, A_COL), _prev_rows_spec(t, hl, c, B_COL),
                _row_spec(t, c), _next_rows_spec(t, hl, c, 0, s),
                pl.BlockSpec((CONV_WIDTH, c), lambda i: (0, 0))]
    return pl.pallas_call(
        body, grid=(nt,), in_specs=in_specs,
        out_specs=[_row_spec(t, 2 * c), pl.BlockSpec((CONV_WIDTH, c), lambda i: (0, 0)), _vec_spec(c)],
        out_shape=[_sds((s, 2 * c), BF16), _sds((CONV_WIDTH, c), F32), _sds((1, c), F32)],
        scratch_shapes=[pltpu.VMEM((8, t + hl, c), F32), pltpu.VMEM((8, t + hl, c), F32), pltpu.VMEM((t, c), F32),
                        pltpu.VMEM((CONV_HALO, c), F32)], name="conv_dw_bwd",
        compiler_params=_params(("arbitrary",)))(p, p, p, p, dc1, dc1, dw_w)


GATE_COL = 4


def gate_mix_fwd(p, b_gate, y_a, y_b, b_conv_out):
    s = p.shape[0]
    t = min(ROW_TILE, s)
    c = D_MODEL

    def body(gt_ref, bg_ref, ya_ref, yb_ref, bc_ref, o_ref):
        gs = _sigmoid(gt_ref[...] + bg_ref[...])
        o_ref[...] = (gs[:, :c] * ya_ref[...] + gs[:, c:] * (yb_ref[...] + bc_ref[...])).astype(BF16)

    return pl.pallas_call(
        body, grid=(s // t,),
        in_specs=[_row_spec(t, 2 * c, GATE_COL), _vec_spec(2 * c), _row_spec(t, c), _row_spec(t, c), _vec_spec(c)],
        out_specs=_row_spec(t, c), out_shape=_sds((s, c), BF16), name="gate_mix_fwd",
        compiler_params=_params(("parallel",)))(p, b_gate, y_a, y_b, b_conv_out)


def gate_mix_bwd(p, b_gate, y_a, y_b, b_conv_out, dmix):
    s = p.shape[0]
    t = min(ROW_TILE, s)
    c = D_MODEL

    def body(gt_ref, bg_ref, ya_ref, yb_ref, bc_ref, d_ref, dya_ref, dyb_ref, dgt_ref, dbg_ref, dbc_ref):
        gs = _sigmoid(gt_ref[...] + bg_ref[...])
        ga, gb = gs[:, :c], gs[:, c:]
        d = d_ref[...]
        dya = ga * d
        dyb = gb * d
        dya_ref[...] = dya.astype(BF16)
        dyb_ref[...] = dyb.astype(BF16)
        dga = d * ya_ref[...] * ga * (1.0 - ga)
        dgb = d * (yb_ref[...] + bc_ref[...]) * gb * (1.0 - gb)
        dgt_ref[:, :c] = dga.astype(BF16)
        dgt_ref[:, c:] = dgb.astype(BF16)
        i = pl.program_id(0)
        _acc_rows(dbg_ref, i, jnp.concatenate([jnp.sum(dga, axis=0, keepdims=True),
                                               jnp.sum(dgb, axis=0, keepdims=True)], axis=-1))
        _acc_rows(dbc_ref, i, jnp.sum(dyb, axis=0, keepdims=True))

    return pl.pallas_call(
        body, grid=(s // t,),
        in_specs=[_row_spec(t, 2 * c, GATE_COL), _vec_spec(2 * c), _row_spec(t, c), _row_spec(t, c), _vec_spec(c),
                  _row_spec(t, c)],
        out_specs=[_row_spec(t, c), _row_spec(t, c), _row_spec(t, 2 * c), _vec_spec(2 * c), _vec_spec(c)],
        out_shape=[_sds((s, c), BF16), _sds((s, c), BF16), _sds((s, 2 * c), BF16), _sds((1, 2 * c), F32),
                   _sds((1, c), F32)],
        name="gate_mix_bwd", compiler_params=_params(("arbitrary",)))(p, b_gate, y_a, y_b, b_conv_out, dmix)


_X_SCALE = X_HEAD_DIM ** -0.5


def _softmax_rows(sc):
    m = jnp.max(sc, axis=-1, keepdims=True)
    e = jnp.exp(sc - m)
    return e / jnp.sum(e, axis=-1, keepdims=True)


def attn_fwd(qx, kv):
    s, d = qx.shape
    m = kv.shape[0]
    t = min(ROW_TILE, s)
    hd = X_HEAD_DIM

    def body(q_ref, kv_ref, o_ref):
        for h in range(X_HEADS):
            sl = slice(h * hd, (h + 1) * hd)
            kh = kv_ref[:, sl].astype(BF16)
            vh = kv_ref[:, d + h * hd:d + (h + 1) * hd].astype(BF16)
            pr = _softmax_rows(_nt(q_ref[:, sl], kh) * _X_SCALE)
            o_ref[:, sl] = _nn(pr.astype(BF16), vh).astype(BF16)

    return pl.pallas_call(
        body, grid=(s // t,), in_specs=[_row_spec(t, d), pl.BlockSpec((m, 2 * d), lambda i: (0, 0))],
        out_specs=_row_spec(t, d), out_shape=_sds((s, d), BF16), name="attn_fwd",
        compiler_params=_params(("parallel",)))(qx, kv)


def attn_bwd(qx, kv, dox):
    s, d = qx.shape
    m = kv.shape[0]
    t = min(ROW_TILE, s)
    hd = X_HEAD_DIM

    def body(q_ref, kv_ref, do_ref, dq_ref, dkv_ref):
        dks, dvs = [], []
        for h in range(X_HEADS):
            sl = slice(h * hd, (h + 1) * hd)
            qh = q_ref[:, sl]
            kh = kv_ref[:, sl].astype(BF16)
            vh = kv_ref[:, d + h * hd:d + (h + 1) * hd].astype(BF16)
            pr = _softmax_rows(_nt(qh, kh) * _X_SCALE)
            doh = do_ref[:, sl].astype(BF16)
            dpr = _nt(doh, vh)
            dvs.append(_tn(pr.astype(BF16), doh))
            ds = pr * (dpr - jnp.sum(dpr * pr, axis=-1, keepdims=True))
            dsb = (ds * _X_SCALE).astype(BF16)
            dq_ref[:, sl] = _nn(dsb, kh).astype(BF16)
            dks.append(_tn(dsb, qh))
        _acc_rows(dkv_ref, pl.program_id(0), jnp.concatenate(dks + dvs, axis=-1))

    return pl.pallas_call(
        body, grid=(s // t,),
        in_specs=[_row_spec(t, d), pl.BlockSpec((m, 2 * d), lambda i: (0, 0)), _row_spec(t, d)],
        out_specs=[_row_spec(t, d), pl.BlockSpec((m, 2 * d), lambda i: (0, 0))],
        out_shape=[_sds((s, d), BF16), _sds((m, 2 * d), F32)], name="attn_bwd",
        compiler_params=_params(("arbitrary",)))(qx, kv, dox)


def ffn_act_fwd(up, dw_w, dw_b):
    s = up.shape[0]
    f = FFN_DIM
    t = min(ROW_TILE, s)
    hl = FFN_HALO

    def body(val_ref, gt_ref, gh_ref, w_ref, b_ref, o_ref, ext):
        i = pl.program_id(0)
        ext[pl.ds(0, hl), :] = jnp.where(i > 0, gh_ref[...], 0.0)
        ext[pl.ds(hl, t), :] = gt_ref[...]
        gc = b_ref[...] + w_ref[0:1, :] * ext[pl.ds(hl - 2, t), :] + w_ref[1:2, :] * ext[pl.ds(hl - 1, t), :] \
            + w_ref[2:3, :] * ext[pl.ds(hl, t), :]
        o_ref[...] = (gc * _sigmoid(gc) * val_ref[...]).astype(BF16)

    return pl.pallas_call(
        body, grid=(s // t,),
        in_specs=[_row_spec(t, f, 0), _row_spec(t, f, 1), _prev_rows_spec(t, hl, f, 1),
                  pl.BlockSpec((3, f), lambda i: (0, 0)), _vec_spec(f)],
        out_specs=_row_spec(t, f), out_shape=_sds((s, f), BF16), scratch_shapes=[pltpu.VMEM((t + hl, f), F32)],
        name="ffn_act_fwd", compiler_params=_params(("parallel",)))(up, up, up, dw_w, dw_b)


def ffn_act_bwd(up, dw_w, dw_b, da):
    s = up.shape[0]
    f = FFN_DIM
    t = min(ROW_TILE, s)
    hl = FFN_HALO
    nt = s // t

    def body(val_ref, valn_ref, gt_ref, gp_ref, gn_ref, da_ref, dan_ref, w_ref, b_ref,
             dup_ref, dw_ref, db_ref, ext_g, ext_d):
        i = pl.program_id(0)
        w0, w1, w2 = w_ref[0:1, :], w_ref[1:2, :], w_ref[2:3, :]
        ext_g[pl.ds(0, hl), :] = jnp.where(i > 0, gp_ref[...], 0.0)
        ext_g[pl.ds(hl, t), :] = gt_ref[...]
        ext_g[pl.ds(hl + t, hl), :] = gn_ref[...]

        def dgc_of(rows, off, val, da_rows):
            gc = b_ref[...] + w0 * ext_g[pl.ds(off + hl - 2, rows), :] + w1 * ext_g[pl.ds(off + hl - 1, rows), :] \
                + w2 * ext_g[pl.ds(off + hl, rows), :]
            sg = _sigmoid(gc)
            return da_rows * val * (sg * (1.0 + gc * (1.0 - sg))), gc * sg

        da_t = da_ref[...]
        dgc, act = dgc_of(t, 0, val_ref[...], da_t)
        dgc_n, _ = dgc_of(hl, t, valn_ref[...], dan_ref[...])
        ext_d[pl.ds(0, t), :] = dgc
        ext_d[pl.ds(t, hl), :] = jnp.where(i < nt - 1, dgc_n, 0.0)
        dup_ref[:, :f] = (da_t * act).astype(BF16)
        dup_ref[:, f:] = (w2 * dgc + w1 * ext_d[pl.ds(1, t), :] + w0 * ext_d[pl.ds(2, t), :]).astype(BF16)
        rows = [jnp.sum(dgc * ext_g[pl.ds(hl - 2 + k, t), :], axis=0, keepdims=True) for k in range(3)]
        _acc_rows(dw_ref, i, jnp.concatenate(rows, axis=0))
        _acc_rows(db_ref, i, jnp.sum(dgc, axis=0, keepdims=True))

    in_specs = [_row_spec(t, f, 0), _next_rows_spec(t, hl, f, 0, s),
                _row_spec(t, f, 1), _prev_rows_spec(t, hl, f, 1), _next_rows_spec(t, hl, f, 1, s),
                _row_spec(t, f), _next_rows_spec(t, hl, f, 0, s),
                pl.BlockSpec((3, f), lambda i: (0, 0)), _vec_spec(f)]
    return pl.pallas_call(
        body, grid=(nt,), in_specs=in_specs,
        out_specs=[_row_spec(t, 2 * f), pl.BlockSpec((3, f), lambda i: (0, 0)), _vec_spec(f)],
        out_shape=[_sds((s, 2 * f), BF16), _sds((3, f), F32), _sds((1, f), F32)],
        scratch_shapes=[pltpu.VMEM((t + 2 * hl, f), F32), pltpu.VMEM((t + hl, f), F32)], name="ffn_act_bwd",
        compiler_params=_params(("arbitrary",)))(up, up, up, up, up, da, da, dw_w, dw_b)


def local_step(x, mem, positions, target, gw, sp):
    n_layers = gw["w_in"].shape[0]
    inv_freq = 1.0 / (ROPE_THETA ** (jnp.arange(0, RET_QK_DIM, 2, dtype=F32) / RET_QK_DIM))
    cos, sin = rope_tables(positions, inv_freq[None, :])
    tables = _decay_tables()
    row = lambda name, l: sp[name][l][None, :]

    saved = []
    h = x
    for l in range(n_layers):
        a = {"h0": h}
        a["u"], a["u_t"] = rms_cast(h, row("norm_mix_g", l), with_transpose=True)
        a["p"] = mm_fwd("mm_in", a["u"], gw["w_in"], l, True)
        a["o"], a["z"], a["states"] = retention_fwd(a["p"], cos, sin, row("ret_gn_g", l), tables)
        a["y_a"] = mm_fwd("mm_ret_out", a["z"], gw["w_ret_out"], l, False)
        a["c1"], a["c3"] = conv_fwd(a["p"], sp["conv_dw_w"][l], row("conv_dw_b", l), row("conv_ln_g", l),
                                    row("conv_ln_b", l))
        a["y_b"] = mm_fwd("mm_conv_out", a["c3"], gw["w_conv_out"], l, False)
        a["mixed"] = gate_mix_fwd(a["p"], row("b_gate", l), a["y_a"], a["y_b"], row("b_conv_out", l))
        a["h1"] = mm_fwd("mm_mix_out", a["mixed"], gw["w_mix_out"], l, False, res=h)
        a["hx"] = rms_cast(a["h1"], row("norm_xattn_g", l))
        a["qx"] = mm_fwd("mm_xq", a["hx"], gw["w_xq"], l, False, out_dtype=BF16)
        a["mem_n"] = rms_cast(mem, row("norm_mem_g", l))
        a["kv"] = mm_fwd("mm_xkv", a["mem_n"], gw["w_xkv"], l, True)
        a["ox"] = attn_fwd(a["qx"], a["kv"])
        a["h2"] = mm_fwd("mm_xo", a["ox"], gw["w_xo"], l, False, res=a["h1"])
        a["hf"], a["hf_t"] = rms_cast(a["h2"], row("norm_ffn_g", l), with_transpose=True)
        a["up"] = mm_fwd("mm_up", a["hf"], gw["w_up"], l, True)
        a["act"] = ffn_act_fwd(a["up"], sp["ffn_dw_w"][l], row("ffn_dw_b", l))
        h = mm_fwd("mm_down", a["act"], gw["w_down"], l, False, res=a["h2"])
        saved.append(a)

    dh, d_final_g, loss = loss_head(h, sp["norm_final_g"][None, :], target)

    big = {}

    def dw(name, key, act, dy, col, l):
        big[key] = mm_dw(name, act, dy, col, l, n_layers, big.get(key))

    small = {n: [None] * n_layers for n in SMALL_REPL + SMALL_SHARDED if n != "norm_final_g"}
    for l in range(n_layers - 1, -1, -1):
        a = saved[l]
        d_act = mm_dx("mm_down_dx", dh, gw["w_down"], l, False)
        dw("mm_down_dw", "w_down", a["act"], dh, False, l)
        d_up, small["ffn_dw_w"][l], small["ffn_dw_b"][l] = ffn_act_bwd(a["up"], sp["ffn_dw_w"][l],
                                                                        row("ffn_dw_b", l), d_act)
        d_hf = mm_dx("mm_up_dx", d_up, gw["w_up"], l, True)
        big["w_up"] = mm_dw("mm_up_dw", a["hf_t"], d_up, True, l, n_layers, big.get("w_up"), a_is_t=True)
        dh, small["norm_ffn_g"][l] = rms_bwd(a["h2"], row("norm_ffn_g", l), d_hf, dh)
        d_ox = mm_dx("mm_xo_dx", dh, gw["w_xo"], l, False)
        dw("mm_xo_dw", "w_xo", a["ox"], dh, False, l)
        d_qx, d_kv = attn_bwd(a["qx"], a["kv"], d_ox)
        d_hx = mm_dx("mm_xq_dx", d_qx, gw["w_xq"], l, False)
        dw("mm_xq_dw", "w_xq", a["hx"], d_qx, False, l)
        d_mem_n = mm_dx("mm_xkv_dx", d_kv, gw["w_xkv"], l, True)
        dw("mm_xkv_dw", "w_xkv", a["mem_n"], d_kv, True, l)
        _, small["norm_mem_g"][l] = rms_bwd(mem, row("norm_mem_g", l), d_mem_n)
        dh, small["norm_xattn_g"][l] = rms_bwd(a["h1"], row("norm_xattn_g", l), d_hx, dh)
        d_mixed = mm_dx("mm_mix_out_dx", dh, gw["w_mix_out"], l, False)
        dw("mm_mix_out_dw", "w_mix_out", a["mixed"], dh, False, l)
        d_ya, d_yb, dp_gate, small["b_gate"][l], small["b_conv_out"][l] = gate_mix_bwd(
            a["p"], row("b_gate", l), a["y_a"], a["y_b"], row("b_conv_out", l), d_mixed)
        d_c3 = mm_dx("mm_conv_out_dx", d_yb, gw["w_conv_out"], l, False)
        dw("mm_conv_out_dw", "w_conv_out", a["c3"], d_yb, False, l)
        d_c1, small["conv_ln_g"][l], small["conv_ln_b"][l] = conv_ln_bwd(a["c1"], row("conv_ln_g", l),
                                                                         row("conv_ln_b", l), d_c3)
        dp_conv, small["conv_dw_w"][l], small["conv_dw_b"][l] = conv_dw_bwd(a["p"], d_c1, sp["conv_dw_w"][l])
        d_z = mm_dx("mm_ret_out_dx", d_ya, gw["w_ret_out"], l, False)
        dw("mm_ret_out_dw", "w_ret_out", a["z"], d_ya, False, l)
        d_o, dp_gret, small["ret_gn_g"][l] = gn_gate_bwd(a["o"], a["p"], row("ret_gn_g", l), d_z)
        dp_q, dp_k, dp_v = retention_bwd(a["p"], cos, sin, a["states"], d_o, tables)
        dp = [dp_q, dp_k, dp_v, dp_gret, dp_conv, dp_gate]
        d_u = mm_in_dx(dp, gw["w_in"], l)
        big["w_in"] = mm_in_dw(a["u_t"], dp, l, n_layers, big.get("w_in"))
        dh, small["norm_mix_g"][l] = rms_bwd(a["h0"], row("norm_mix_g", l), d_u, dh)

    small = {n: jnp.stack([g.reshape(sp[n].shape[1:]) for g in v]) for n, v in small.items()}
    small["norm_final_g"] = d_final_g.reshape(-1)
    return loss, dh, big, small


_ANY = pl.BlockSpec(memory_space=pl.ANY)


def _place():
    x, y, c = lax.axis_index("x"), lax.axis_index("y"), lax.axis_index("c")
    return x, y, c


def _other_chips(x, y):
    return [(1 - x, y), (x, 1 - y), (1 - x, 1 - y)]


def place_shard(w, chip):
    lyr, a, b = w.shape
    t = _flat_tile(a, b)

    def body(chip_ref, w_ref, o_ref):
        o_ref[...] = w_ref[...].astype(BF16)

    grid_spec = pltpu.PrefetchScalarGridSpec(
        num_scalar_prefetch=1, grid=(lyr, a // t),
        in_specs=[pl.BlockSpec((None, t, b), lambda l, i, cr: (l, i, 0))],
        out_specs=pl.BlockSpec((None, None, t, b), lambda l, i, cr: (l, cr[0], i, 0)))
    return pl.pallas_call(body, grid_spec=grid_spec, out_shape=_sds((lyr, N_CHIPS, a, b), BF16), name="place_shard",
                          compiler_params=_params(("parallel", "parallel")))(chip, w)


def gather_weights(bufs):
    n = len(bufs)

    def body(*refs):
        outs = refs[n:2 * n]
        ici_send, ici_recv, pair_send, pair_recv = refs[2 * n:]
        x, y, c = _place()
        mine = 2 * x + y
        chips = _other_chips(x, y)

        def part(w, core, slot):
            lh = outs[w].shape[0] // 2
            return outs[w].at[pl.ds(core * lh, lh), slot]

        def over_ici(w, j, slot):
            px, py = chips[j]
            return pltpu.make_async_remote_copy(
                src_ref=part(w, c, slot), dst_ref=part(w, c, slot), send_sem=ici_send.at[w, j],
                recv_sem=ici_recv.at[w, j], device_id=(px, py, c), device_id_type=MESH)

        def to_pair(w, j, core):
            px, py = chips[j]
            return pltpu.make_async_remote_copy(
                src_ref=part(w, core, 2 * px + py), dst_ref=part(w, core, 2 * px + py), send_sem=pair_send.at[w, j],
                recv_sem=pair_recv.at[w, j], device_id=(x, y, 1 - c), device_id_type=MESH)

        for w in range(n):
            for j in range(3):
                over_ici(w, j, mine).start()
        for w in range(n):
            for j, (px, py) in enumerate(chips):
                over_ici(w, j, 2 * px + py).wait_recv()
                to_pair(w, j, c).start()
        for w in range(n):
            for j in range(3):
                to_pair(w, j, 1 - c).wait_recv()
                to_pair(w, j, c).wait_send()
                over_ici(w, j, mine).wait_send()

    return pl.pallas_call(
        body, in_specs=[_ANY] * n, out_specs=[_ANY] * n, out_shape=[_sds(b.shape, b.dtype) for b in bufs],
        input_output_aliases={i: i for i in range(n)},
        scratch_shapes=[pltpu.SemaphoreType.DMA((n, 3))] * 4,
        name="gather_weights", compiler_params=_params())(*bufs)


def pair_exchange(grads):
    n = len(grads)

    def body(*refs):
        ins, outs = refs[:n], refs[n:2 * n]
        send_sems, recv_sems = refs[2 * n:]
        x, y, c = _place()
        cps = []
        for w in range(n):
            lh = ins[w].shape[0] // 2
            cp = pltpu.make_async_remote_copy(
                src_ref=ins[w].at[pl.ds((1 - c) * lh, lh)], dst_ref=outs[w], send_sem=send_sems.at[w],
                recv_sem=recv_sems.at[w], device_id=(x, y, 1 - c), device_id_type=MESH)
            cp.start()
            cps.append(cp)
        for cp in cps:
            cp.wait_send()
            cp.wait_recv()

    out_shape = [_sds((g.shape[0] // 2,) + g.shape[1:], g.dtype) for g in grads]
    return pl.pallas_call(
        body, in_specs=[_ANY] * n, out_specs=[_ANY] * n, out_shape=out_shape,
        scratch_shapes=[pltpu.SemaphoreType.DMA((n,)), pltpu.SemaphoreType.DMA((n,))],
        name="pair_exchange", compiler_params=_params())(*grads)


def chip_exchange(parts):
    n = len(parts)

    def body(*refs):
        ins, outs = refs[:n], refs[n:2 * n]
        send_sems, recv_sems = refs[2 * n:]
        x, y, c = _place()
        mine = 2 * x + y
        chips = _other_chips(x, y)

        def copy(w, j, slot):
            px, py = chips[j]
            return pltpu.make_async_remote_copy(
                src_ref=ins[w].at[:, 2 * px + py], dst_ref=outs[w].at[slot], send_sem=send_sems.at[w, j],
                recv_sem=recv_sems.at[w, j], device_id=(px, py, c), device_id_type=MESH)

        for w in range(n):
            for j in range(3):
                copy(w, j, mine).start()
        for w in range(n):
            for j, (px, py) in enumerate(chips):
                copy(w, j, 2 * px + py).wait_recv()
                copy(w, j, mine).wait_send()

    out_shape = [_sds((N_CHIPS, g.shape[0]) + g.shape[2:], g.dtype) for g in parts]
    return pl.pallas_call(
        body, in_specs=[_ANY] * n, out_specs=[_ANY] * n, out_shape=out_shape,
        scratch_shapes=[pltpu.SemaphoreType.DMA((n, 3)), pltpu.SemaphoreType.DMA((n, 3))],
        name="chip_exchange", compiler_params=_params())(*parts)


def pair_share(halves):
    n = len(halves)

    def body(*refs):
        ins, outs = refs[:n], refs[n:2 * n]
        send_sems, recv_sems = refs[2 * n:]
        x, y, c = _place()
        cps = []
        for w in range(n):
            cp = pltpu.make_async_remote_copy(
                src_ref=ins[w], dst_ref=outs[w], send_sem=send_sems.at[w], recv_sem=recv_sems.at[w],
                device_id=(x, y, 1 - c), device_id_type=MESH)
            cp.start()
            cps.append(cp)
        for cp in cps:
            cp.wait_send()
            cp.wait_recv()

    return pl.pallas_call(
        body, in_specs=[_ANY] * n, out_specs=[_ANY] * n, out_shape=[_sds(g.shape, g.dtype) for g in halves],
        scratch_shapes=[pltpu.SemaphoreType.DMA((n,)), pltpu.SemaphoreType.DMA((n,))],
        name="pair_share", compiler_params=_params())(*halves)


def all_reduce_small(vec):
    r, lanes = vec.shape

    def body(v_ref, o_ref, buf, send_sems, recv_sems):
        x, y, c = _place()
        me = 4 * x + 2 * y + c
        buf[me] = v_ref[...]
        cps = []
        for k in range(1, N_DEV):
            peer = (me + k) % N_DEV
            cp = pltpu.make_async_remote_copy(
                src_ref=v_ref, dst_ref=buf.at[me], send_sem=send_sems.at[k - 1], recv_sem=recv_sems.at[k - 1],
                device_id=(peer // 4, (peer // 2) % 2, peer % 2), device_id_type=MESH)
            cp.start()
            cps.append(cp)
        for k in range(1, N_DEV):
            src = (me + N_DEV - k) % N_DEV
            cps[k - 1].wait_send()
            pltpu.make_async_remote_copy(
                src_ref=v_ref, dst_ref=buf.at[src], send_sem=send_sems.at[k - 1], recv_sem=recv_sems.at[k - 1],
                device_id=(src // 4, (src // 2) % 2, src % 2), device_id_type=MESH).wait_recv()
        acc = buf[0]
        for d in range(1, N_DEV):
            acc = acc + buf[d]
        o_ref[...] = acc

    vm = pl.BlockSpec(memory_space=pltpu.VMEM)
    return pl.pallas_call(
        body, in_specs=[vm], out_specs=vm, out_shape=_sds((r, lanes), F32),
        scratch_shapes=[pltpu.VMEM((N_DEV, r, lanes), F32), pltpu.SemaphoreType.DMA((N_DEV - 1,)),
                        pltpu.SemaphoreType.DMA((N_DEV - 1,))],
        name="all_reduce_small", compiler_params=_params())(vec)


ELEMENTWISE_BLOCK_BYTES = 1 << 20


def _flat_tile(rows, cols):
    for t in (512, 256, 128, 64, 32, 16, 8):
        if rows % t == 0 and t * cols * 4 <= ELEMENTWISE_BLOCK_BYTES:
            return t
    return rows


def add_pair(g, r, half):
    lyr, _, a, b = g.shape
    lh = lyr // 2
    rows = lh * 4 * a
    t = _flat_tile(rows, b)
    nb = rows // t
    g2 = g.reshape(lyr * 4 * a, b)
    r2 = r.reshape(rows, b)

    def body(half_ref, g_ref, r_ref, o_ref):
        o_ref[...] = (g_ref[...] + r_ref[...]).astype(BF16)

    grid_spec = pltpu.PrefetchScalarGridSpec(
        num_scalar_prefetch=1, grid=(nb,),
        in_specs=[pl.BlockSpec((t, b), lambda i, hr: (hr[0] * nb + i, 0)), pl.BlockSpec((t, b), lambda i, hr: (i, 0))],
        out_specs=pl.BlockSpec((t, b), lambda i, hr: (i, 0)))
    out = pl.pallas_call(body, grid_spec=grid_spec, out_shape=_sds((rows, b), BF16), name="add_pair",
                         compiler_params=_params(("parallel",)))(half, g2, r2)
    return out.reshape(lh, 4, a, b)


def sum_chips(own, parts, chip):
    _, lh, a, b = parts.shape
    t = _flat_tile(a, b)

    def body(chip_ref, own_ref, p_ref, o_ref):
        mine = chip_ref[0]

        def term(s):
            return jnp.where(mine == s, own_ref[...], p_ref[s]).astype(F32)

        o_ref[...] = ((term(0) + term(1)) + term(2)) + term(3)

    grid_spec = pltpu.PrefetchScalarGridSpec(
        num_scalar_prefetch=1, grid=(lh, a // t),
        in_specs=[pl.BlockSpec((None, None, t, b), lambda l, i, cr: (l, cr[0], i, 0)),
                  pl.BlockSpec((N_CHIPS, None, t, b), lambda l, i, cr: (0, l, i, 0))],
        out_specs=pl.BlockSpec((None, t, b), lambda l, i, cr: (l, i, 0)))
    return pl.pallas_call(body, grid_spec=grid_spec, out_shape=_sds((lh, a, b), F32), name="sum_chips",
                          compiler_params=_params(("parallel", "parallel")))(chip, own, parts)


def _adamw_math(w, g, m, v):
    mm = ADAM_B1 * m + (1.0 - ADAM_B1) * g
    vv = ADAM_B2 * v + (1.0 - ADAM_B2) * jnp.square(g)
    m_hat = mm / (1.0 - ADAM_B1 ** ADAM_STEP)
    v_hat = vv / (1.0 - ADAM_B2 ** ADAM_STEP)
    return -ADAM_LR * (m_hat / (jnp.sqrt(v_hat) + ADAM_EPS) + ADAM_WD * w), mm, vv


def adamw(w, g, m, v):
    shape = w.shape
    c = shape[-1]
    rows = int(np.prod(shape[:-1])) if len(shape) > 1 else 1
    t = _flat_tile(rows, c)
    flat = lambda z: z.reshape(rows, c)

    def body(w_ref, g_ref, m_ref, v_ref, d_ref, nm_ref, nv_ref):
        d_ref[...], nm_ref[...], nv_ref[...] = _adamw_math(w_ref[...], g_ref[...], m_ref[...], v_ref[...])

    spec = pl.BlockSpec((t, c), lambda i: (i, 0))
    outs = pl.pallas_call(body, grid=(rows // t,), in_specs=[spec] * 4, out_specs=[spec] * 3,
                          out_shape=[_sds((rows, c), F32)] * 3, name="adamw",
                          compiler_params=_params(("parallel",)))(flat(w), flat(g), flat(m), flat(v))
    return tuple(o.reshape(shape) for o in outs)


def adamw_halves(w, g_own, g_other, m, v, core):
    lyr, a, b = w.shape
    lh = lyr // 2
    t = _flat_tile(a, b)

    def body(core_ref, w_ref, go_ref, gs_ref, m_ref, v_ref, g_ref, d_ref, nm_ref, nv_ref):
        own = pl.program_id(0) // lh == core_ref[0]
        g = jnp.where(own, go_ref[...], gs_ref[...])
        g_ref[...] = g
        d_ref[...], nm_ref[...], nv_ref[...] = _adamw_math(w_ref[...], g, m_ref[...], v_ref[...])

    full = pl.BlockSpec((None, t, b), lambda l, i, cr: (l, i, 0))
    own_spec = pl.BlockSpec((None, t, b), lambda l, i, cr: (jnp.clip(l - cr[0] * lh, 0, lh - 1), i, 0))
    other_spec = pl.BlockSpec((None, t, b), lambda l, i, cr: (jnp.clip(l - (1 - cr[0]) * lh, 0, lh - 1), i, 0))
    grid_spec = pltpu.PrefetchScalarGridSpec(
        num_scalar_prefetch=1, grid=(lyr, a // t), in_specs=[full, own_spec, other_spec, full, full],
        out_specs=[full] * 4)
    return pl.pallas_call(body, grid_spec=grid_spec, out_shape=[_sds(w.shape, F32)] * 4, name="adamw_halves",
                          compiler_params=_params(("parallel", "parallel")))(core, w, g_own, g_other, m, v)


def _pack(parts):
    flat = jnp.concatenate([p.reshape(-1) for p in parts])
    pad = (-flat.shape[0]) % 1024
    return jnp.pad(flat, (0, pad)).reshape(-1, 128)


def _unpack(packed, shapes):
    flat = packed.reshape(-1)
    out, off = [], 0
    for shp in shapes:
        size = int(np.prod(shp))
        out.append(flat[off:off + size].reshape(shp))
        off += size
    return out


def kernel(x, mem, positions, norm_mix_g, w_in, b_gate, ret_gn_g, w_ret_out, conv_dw_w, conv_dw_b, conv_ln_g, conv_ln_b, w_conv_out, b_conv_out, w_mix_out, norm_xattn_g, norm_mem_g, w_xq, w_xkv, w_xo, norm_ffn_g, w_up, ffn_dw_w, ffn_dw_b, w_down, norm_final_g, loss_target, m_norm_mix_g, m_w_in, m_b_gate, m_ret_gn_g, m_w_ret_out, m_conv_dw_w, m_conv_dw_b, m_conv_ln_g, m_conv_ln_b, m_w_conv_out, m_b_conv_out, m_w_mix_out, m_norm_xattn_g, m_norm_mem_g, m_w_xq, m_w_xkv, m_w_xo, m_norm_ffn_g, m_w_up, m_ffn_dw_w, m_ffn_dw_b, m_w_down, m_norm_final_g, v_norm_mix_g, v_w_in, v_b_gate, v_ret_gn_g, v_w_ret_out, v_conv_dw_w, v_conv_dw_b, v_conv_ln_g, v_conv_ln_b, v_w_conv_out, v_b_conv_out, v_w_mix_out, v_norm_xattn_g, v_norm_mem_g, v_w_xq, v_w_xkv, v_w_xo, v_norm_ffn_g, v_w_up, v_ffn_dw_w, v_ffn_dw_b, v_w_down, v_norm_final_g):
    args = locals()
    w = {n: args[n] for n in WEIGHTS}
    m = {n: args["m_" + n] for n in WEIGHTS}
    v = {n: args["v_" + n] for n in WEIGHTS}
    chip = 2 * lax.axis_index("x") + lax.axis_index("y")
    core = lax.axis_index("c")

    chip_op = chip.reshape(1).astype(jnp.int32)
    core_op = core.reshape(1).astype(jnp.int32)
    gathered = gather_weights([place_shard(w[n], chip_op) for n in BIG])
    gw = dict(zip(BIG, gathered))

    sp = {n: w[n] for n in SMALL_REPL}
    placed = []
    for n in SMALL_SHARDED:
        cols = w[n].shape[-1]
        full = jnp.zeros(w[n].shape[:-1] + (N_CHIPS * cols,), F32)
        placed.append(lax.dynamic_update_slice_in_dim(full, w[n], chip * cols, axis=2))
    placed_shapes = [p.shape for p in placed]
    gathered_small = all_reduce_small(_pack([jnp.where(core == 0, p, 0.0) for p in placed]))
    for n, arr in zip(SMALL_SHARDED, _unpack(gathered_small, placed_shapes)):
        sp[n] = arr

    loss, grad_x, big, small = local_step(x[0], mem[0], positions.reshape(-1, 1), loss_target[0], gw, sp)

    names = [n for n in SMALL_REPL + SMALL_SHARDED]
    shapes = [small[n].shape for n in names] + [(128,)]
    reduced = _unpack(all_reduce_small(_pack([small[n] for n in names] + [loss.reshape(-1)])), shapes)
    grads = dict(zip(names, reduced[:-1]))
    loss_out = reduced[-1][0]
    for n in SMALL_SHARDED:
        cols = w[n].shape[-1]
        grads[n] = lax.dynamic_slice_in_dim(grads[n], chip * cols, cols, axis=2)

    blist = [big[n] for n in BIG]
    from_pair = pair_exchange(blist)
    pair_sum = [add_pair(g, r, core_op) for g, r in zip(blist, from_pair)]
    from_chips = chip_exchange(pair_sum)
    halves = [sum_chips(own, parts, chip_op) for own, parts in zip(pair_sum, from_chips)]
    other_halves = pair_share(halves)

    delta, new_m, new_v = {}, {}, {}
    for n, g_own, g_other in zip(BIG, halves, other_halves):
        grads[n], delta[n], new_m[n], new_v[n] = adamw_halves(w[n], g_own, g_other, m[n], v[n], core_op)
    for n in WEIGHTS:
        if n not in BIG:
            delta[n], new_m[n], new_v[n] = adamw(w[n], grads[n], m[n], v[n])
    return (loss_out, grad_x[None], *[grads[n] for n in WEIGHTS], *[delta[n] for n in WEIGHTS],
            *[new_m[n] for n in WEIGHTS], *[new_v[n] for n in WEIGHTS])
```

```python
import functools

import jax
import jax.numpy as jnp
import numpy as np
from jax import lax
from jax.experimental import pallas as pl
from jax.experimental.pallas import tpu as pltpu

F32 = jnp.float32
BF16 = jnp.bfloat16
MESH = pl.DeviceIdType.MESH

D_MODEL = 1024
CHUNK = 64
RET_HEADS = 4
RET_QK_DIM = 256
RET_V_DIM = 512
ROPE_THETA = 10000.0
CONV_WIDTH = 31
X_HEADS = 4
X_HEAD_DIM = 256
FFN_DIM = 2816
RMS_EPS = 1e-6
LN_EPS = 1e-5
ADAM_LR = 0.001
ADAM_B1 = 0.9
ADAM_B2 = 0.999
ADAM_EPS = 1e-08
ADAM_WD = 0.01
ADAM_STEP = 10

N_CHIPS = 4
N_DEV = 8
CONV_HALO = 32
FFN_HALO = 8
V7X_VMEM_LIMIT = 56 * 1024 * 1024
ROW_TILE = 256
STRIP_ROWS = 16
STRIP_LANES = 1024
DW_TAPS = 2
FFN_ROWS = 16
FFN_LANES = 256
MM_TILE_M = 1024
RET_TILE = 512

BIG = ("w_in", "w_ret_out", "w_conv_out", "w_mix_out", "w_xq", "w_xkv", "w_xo", "w_up", "w_down")
COL_SHARDED = ("w_in", "w_xkv", "w_up")
SMALL_REPL = ("norm_mix_g", "b_gate", "ret_gn_g", "conv_dw_b", "conv_ln_g", "conv_ln_b", "b_conv_out",
              "norm_xattn_g", "norm_mem_g", "norm_ffn_g", "ffn_dw_b", "norm_final_g")
SMALL_SHARDED = ("conv_dw_w", "ffn_dw_w")
WEIGHTS = ('norm_mix_g', 'w_in', 'b_gate', 'ret_gn_g', 'w_ret_out', 'conv_dw_w', 'conv_dw_b', 'conv_ln_g',
           'conv_ln_b', 'w_conv_out', 'b_conv_out', 'w_mix_out', 'norm_xattn_g', 'norm_mem_g', 'w_xq', 'w_xkv',
           'w_xo', 'norm_ffn_g', 'w_up', 'ffn_dw_w', 'ffn_dw_b', 'w_down', 'norm_final_g')


def _params(sem=None):
    return pltpu.CompilerParams(dimension_semantics=sem, vmem_limit_bytes=V7X_VMEM_LIMIT)


def _sds(shape, dtype):
    return jax.ShapeDtypeStruct(tuple(shape), dtype)


def _sigmoid(x):
    return jax.nn.sigmoid(x)


def _dot(a, b, ca, cb):
    return lax.dot_general(a, b, (((ca,), (cb,)), ((), ())), preferred_element_type=F32)


def _nn(a, b):
    return _dot(a, b, 1, 0)


def _nt(a, b):
    return _dot(a, b, 1, 1)


def _tn(a, b):
    return _dot(a, b, 0, 0)


def _mm(name, dims, grid, in_specs, out_spec, out_sds, nk, operands, with_res=False):
    def body(*refs):
        if with_res:
            a_ref, b_ref, r_ref, o_ref = refs
        else:
            a_ref, b_ref, o_ref = refs
            r_ref = None
        prod = _dot(a_ref[...].astype(BF16), b_ref[...].astype(BF16), *dims)
        if nk == 1:
            if r_ref is not None:
                prod = prod + r_ref[...]
            o_ref[...] = prod.astype(o_ref.dtype)
        else:
            k = pl.program_id(2)

            @pl.when(k == 0)
            def _():
                o_ref[...] = (prod + r_ref[...]) if r_ref is not None else prod

            @pl.when(k > 0)
            def _():
                o_ref[...] += prod

    assert nk == 1 or out_sds.dtype == F32
    return pl.pallas_call(body, grid=grid, in_specs=in_specs, out_specs=out_spec, out_shape=out_sds, name=name,
                          compiler_params=_params(("parallel", "parallel", "arbitrary")))(*operands)


def _div_tile(n, want):
    best = None
    for t in range(128, min(n, want) + 1, 128):
        if n % t == 0:
            best = t
    assert best is not None, (n, want)
    return best


def mm_fwd(name, a, g, l, col, out_dtype=F32, res=None):
    m, k_dim = a.shape
    tm = min(MM_TILE_M, m)
    if col:
        _, _, kk, b = g.shape
        assert kk == k_dim
        tn = _div_tile(b, 1408)
        nps = b // tn
        n = 4 * b
        grid = (m // tm, n // tn, 1)
        in_specs = [pl.BlockSpec((tm, k_dim), lambda i, j, k: (i, 0)),
                    pl.BlockSpec((None, None, k_dim, tn), lambda i, j, k: (l, j // nps, 0, j % nps))]
        nk = 1
        w = g
    else:
        lyr, _, a_rows, n = g.shape
        assert 4 * a_rows == k_dim
        w = g.reshape(lyr, k_dim, n)
        tk = _div_tile(k_dim, 1408)
        tn = n
        nk = k_dim // tk
        grid = (m // tm, 1, nk)
        in_specs = [pl.BlockSpec((tm, tk), lambda i, j, k: (i, k)),
                    pl.BlockSpec((None, tk, tn), lambda i, j, k: (l, k, j))]
    ops = [a, w]
    if res is not None:
        in_specs.append(pl.BlockSpec((tm, tn), lambda i, j, k: (i, j)))
        ops.append(res)
    return _mm(name, (1, 0), grid, in_specs, pl.BlockSpec((tm, tn), lambda i, j, k: (i, j)), _sds((m, n), out_dtype),
               nk, ops, with_res=res is not None)


def mm_dx(name, dy, g, l, col):
    m, n = dy.shape
    tm = min(MM_TILE_M, m)
    if col:
        _, _, k_dim, b = g.shape
        assert 4 * b == n
        tk = _div_tile(b, 1408)
        nps = b // tk
        nk = n // tk
        grid = (m // tm, 1, nk)
        in_specs = [pl.BlockSpec((tm, tk), lambda i, j, k: (i, k)),
                    pl.BlockSpec((None, None, k_dim, tk), lambda i, j, k: (l, k // nps, 0, k % nps))]
        out_spec = pl.BlockSpec((tm, k_dim), lambda i, j, k: (i, 0))
        w = g
    else:
        lyr, _, a_rows, nn_ = g.shape
        assert nn_ == n
        k_dim = 4 * a_rows
        w = g.reshape(lyr, k_dim, n)
        tno = _div_tile(k_dim, 1408)
        nk = 1
        grid = (m // tm, k_dim // tno, 1)
        in_specs = [pl.BlockSpec((tm, n), lambda i, j, k: (i, 0)),
                    pl.BlockSpec((None, tno, n), lambda i, j, k: (l, j, 0))]
        out_spec = pl.BlockSpec((tm, tno), lambda i, j, k: (i, j))
    return _mm(name, (1, 1), grid, in_specs, out_spec, _sds((m, k_dim), F32), nk, [dy, w])


def mm_dw(name, a, dy, col, l, n_layers, into=None):
    m, k_dim = a.shape
    _, n = dy.shape
    ts = min(MM_TILE_M, m)
    ns = m // ts
    tko = _div_tile(k_dim, 1408)
    if col:
        b = n // 4
        tn = _div_tile(b, 1408)
        nps = b // tn
        grid = (k_dim // tko, n // tn, ns)
        out_spec = pl.BlockSpec((None, None, tko, tn), lambda i, j, s: (l, j // nps, i, j % nps))
        shape = (n_layers, 4, k_dim, b)
    else:
        tn = n
        grid = (k_dim // tko, 1, ns)
        out_spec = pl.BlockSpec((None, tko, tn), lambda i, j, s: (l, i, j))
        shape = (n_layers, k_dim, n)
    in_specs = [pl.BlockSpec((ts, tko), lambda i, j, s: (s, i)),
                pl.BlockSpec((ts, tn), lambda i, j, s: (s, j))]
    ops = [a, dy]
    aliases = {}
    if into is not None:
        in_specs.append(_ANY)
        ops.append(into.reshape(shape))
        aliases = {2: 0}

    def body(a_ref, b_ref, *rest):
        o_ref = rest[-1]
        prod = _tn(a_ref[...].astype(BF16), b_ref[...].astype(BF16))
        if ns == 1:
            o_ref[...] = prod
        else:
            s = pl.program_id(2)

            @pl.when(s == 0)
            def _():
                o_ref[...] = prod

            @pl.when(s > 0)
            def _():
                o_ref[...] += prod

    out = pl.pallas_call(body, grid=grid, in_specs=in_specs, out_specs=out_spec, out_shape=_sds(shape, F32),
                         input_output_aliases=aliases, name=name,
                         compiler_params=_params(("parallel", "parallel", "arbitrary")))(*ops)
    return out.reshape(n_layers, 4, k_dim if col else k_dim // 4, shape[-1])


def _row_spec(t, c, col=0):
    return pl.BlockSpec((t, c), lambda i: (i, col))


def _vec_spec(c):
    return pl.BlockSpec((1, c), lambda i: (0, 0))


def _acc_rows(ref, i, val):
    @pl.when(i == 0)
    def _():
        ref[...] = val

    @pl.when(i > 0)
    def _():
        ref[...] += val


def rope_tables(positions, inv_freq):
    s = positions.shape[0]
    t = min(ROW_TILE, s)
    half = inv_freq.shape[1]

    def body(p_ref, f_ref, c_ref, s_ref):
        ang = p_ref[...].astype(F32) * f_ref[...]
        c_ref[...] = jnp.cos(ang)
        s_ref[...] = jnp.sin(ang)

    return pl.pallas_call(
        body, grid=(s // t,), in_specs=[_row_spec(t, 1), _vec_spec(half)],
        out_specs=[_row_spec(t, half), _row_spec(t, half)], out_shape=[_sds((s, half), F32)] * 2, name="rope_tables",
        compiler_params=_params(("parallel",)))(positions, inv_freq)


def rms_cast(h, g):
    s, d = h.shape
    t = min(ROW_TILE, s)

    def body(h_ref, g_ref, o_ref):
        x = h_ref[...]
        r = lax.rsqrt(jnp.mean(x * x, axis=-1, keepdims=True) + RMS_EPS)
        o_ref[...] = (x * r * g_ref[...]).astype(BF16)

    return pl.pallas_call(body, grid=(s // t,), in_specs=[_row_spec(t, d), _vec_spec(d)], out_specs=_row_spec(t, d),
                          out_shape=_sds((s, d), BF16), name="rms_cast", compiler_params=_params(("parallel",)))(h, g)


def _rms_bwd_math(x, g, du):
    r = lax.rsqrt(jnp.mean(x * x, axis=-1, keepdims=True) + RMS_EPS)
    gd = g * du
    dx = r * gd - x * (r * r * r) * jnp.mean(x * gd, axis=-1, keepdims=True)
    dg = jnp.sum(x * r * du, axis=0, keepdims=True)
    return dx, dg


def rms_bwd(h, g, du, dres=None):
    s, d = h.shape
    t = min(ROW_TILE, s)

    def body(*refs):
        if dres is None:
            h_ref, g_ref, du_ref, dh_ref, dg_ref = refs
        else:
            h_ref, g_ref, du_ref, dr_ref, dh_ref, dg_ref = refs
        dx, dg = _rms_bwd_math(h_ref[...], g_ref[...], du_ref[...])
        if dres is not None:
            dx = dx + dr_ref[...]
        dh_ref[...] = dx
        _acc_rows(dg_ref, pl.program_id(0), dg)

    in_specs = [_row_spec(t, d), _vec_spec(d), _row_spec(t, d)]
    ops = [h, g, du]
    if dres is not None:
        in_specs.append(_row_spec(t, d))
        ops.append(dres)
    return pl.pallas_call(body, grid=(s // t,), in_specs=in_specs, out_specs=[_row_spec(t, d), _vec_spec(d)],
                          out_shape=[_sds((s, d), F32), _sds((1, d), F32)], name="rms_bwd",
                          compiler_params=_params(("arbitrary",)))(*ops)


def loss_head(h, g, target):
    s, d = h.shape
    t = min(ROW_TILE, s)

    def body(h_ref, g_ref, t_ref, dh_ref, dg_ref, loss_ref):
        x = h_ref[...]
        gg = g_ref[...]
        r = lax.rsqrt(jnp.mean(x * x, axis=-1, keepdims=True) + RMS_EPS)
        err = x * r * gg - t_ref[...]
        part = 0.5 * jnp.sum(jnp.mean(err * err, axis=-1, keepdims=True), axis=0, keepdims=True)
        dy = err * (1.0 / d)
        dx, dg = _rms_bwd_math(x, gg, dy)
        dh_ref[...] = dx
        i = pl.program_id(0)
        _acc_rows(dg_ref, i, dg)
        _acc_rows(loss_ref, i, jnp.broadcast_to(part, (1, 128)))

    return pl.pallas_call(
        body, grid=(s // t,), in_specs=[_row_spec(t, d), _vec_spec(d), _row_spec(t, d)],
        out_specs=[_row_spec(t, d), _vec_spec(d), _vec_spec(128)],
        out_shape=[_sds((s, d), F32), _sds((1, d), F32), _sds((1, 128), F32)], name="loss_head",
        compiler_params=_params(("arbitrary",)))(h, g, target)


def _rot(x, cos, sin):
    half = x.shape[-1] // 2
    x1, x2 = x[:, :half], x[:, half:]
    return jnp.concatenate([x1 * cos - x2 * sin, x2 * cos + x1 * sin], axis=-1)


def _rot_t(dy, cos, sin):
    half = dy.shape[-1] // 2
    d1, d2 = dy[:, :half], dy[:, half:]
    return jnp.concatenate([d1 * cos + d2 * sin, d2 * cos - d1 * sin], axis=-1)


def _decay_tables(t):
    log_gamma = jnp.log(1.0 - jnp.power(2.0, -5.0 - jnp.arange(RET_HEADS, dtype=F32)))
    idx = jnp.arange(t, dtype=F32)
    dist = jnp.abs(idx[:, None] - idx[None, :])
    chunk = jnp.arange(t) // CHUNK
    seen = chunk[None, :] <= chunk[:, None]
    d_tile = jnp.where(seen[None], jnp.exp(log_gamma[:, None, None] * dist), 0.0)
    decay_q = jnp.exp(log_gamma[:, None] * (idx[None, :] + 1.0))[:, :, None]
    decay_k = jnp.exp(log_gamma[:, None] * (t - 1.0 - idx[None, :]))[:, :, None]
    decay_tile = jnp.exp(log_gamma * t)[:, None, None]
    return d_tile, decay_q, decay_k, decay_tile


_QK_SCALE = RET_QK_DIM ** -0.5


def _retention_specs(t, nt, order):
    dk, dv = RET_QK_DIM, RET_V_DIM
    hmap = lambda h, i: (h, 0, 0)
    qkv = [pl.BlockSpec((t, dk), lambda h, i: (order(i), h)),
           pl.BlockSpec((t, dk), lambda h, i: (order(i), RET_HEADS + h)),
           pl.BlockSpec((t, dv), lambda h, i: (order(i), 4 + h))]
    rope = [pl.BlockSpec((t, dk // 2), lambda h, i: (order(i), 0))] * 2
    tables = [pl.BlockSpec((None, t, t), hmap), pl.BlockSpec((None, t, 1), hmap), pl.BlockSpec((None, t, 1), hmap),
              pl.BlockSpec((None, 1, 1), hmap)]
    return qkv, rope, tables


def retention_fwd(p, cos, sin, gn_g, tables):
    s = p.shape[0]
    t = min(RET_TILE, s)
    nt = s // t
    dk, dv = RET_QK_DIM, RET_V_DIM

    def body(q_ref, k_ref, v_ref, cos_ref, sin_ref, di_ref, dq_ref, dkk_ref, dc_ref, gr_ref, gn_ref,
             o_ref, z_ref, st_ref, state):
        @pl.when(pl.program_id(1) == 0)
        def _():
            state[...] = jnp.zeros_like(state)

        cs, sn = cos_ref[...], sin_ref[...]
        qb = (_rot(q_ref[...], cs, sn) * _QK_SCALE).astype(BF16)
        kr = _rot(k_ref[...], cs, sn)
        vb = v_ref[...].astype(BF16)
        st = state[...].astype(BF16)
        st_ref[...] = st
        scores = _nt(qb, kr.astype(BF16)) * di_ref[...]
        o = _nn(scores.astype(BF16), vb) + _nn(qb, st) * dq_ref[...]
        state[...] = state[...] * dc_ref[...] + _tn((kr * dkk_ref[...]).astype(BF16), vb)
        o_ref[...] = o
        mu = jnp.mean(o, axis=-1, keepdims=True)
        oc = o - mu
        var = jnp.mean(oc * oc, axis=-1, keepdims=True)
        y = oc * lax.rsqrt(var + LN_EPS) * gn_ref[...]
        gr = gr_ref[...]
        z_ref[...] = (gr * _sigmoid(gr) * y).astype(BF16)

    qkv, rope, tabs = _retention_specs(t, nt, lambda i: i)
    in_specs = qkv + rope + tabs + [pl.BlockSpec((t, dv), lambda h, i: (i, 8 + h)),
                                    pl.BlockSpec((1, dv), lambda h, i: (0, h))]
    out_specs = [pl.BlockSpec((t, dv), lambda h, i: (i, h)),
                 pl.BlockSpec((t, dv), lambda h, i: (i, h)),
                 pl.BlockSpec((None, None, dk, dv), lambda h, i: (h, i, 0, 0))]
    out_shape = [_sds((s, RET_HEADS * dv), F32), _sds((s, RET_HEADS * dv), BF16), _sds((RET_HEADS, nt, dk, dv), BF16)]
    return pl.pallas_call(
        body, grid=(RET_HEADS, nt), in_specs=in_specs, out_specs=out_specs, out_shape=out_shape,
        scratch_shapes=[pltpu.VMEM((dk, dv), F32)], name="retention_fwd",
        compiler_params=_params(("parallel", "arbitrary")))(p, p, p, cos, sin, *tables, p, gn_g)


def gn_gate_bwd(o, p, gn_g, dz):
    s = o.shape[0]
    t = min(ROW_TILE, s)
    dv = RET_V_DIM
    w = RET_HEADS * dv

    def body(o_ref, gr_ref, gn_ref, dz_ref, do_ref, dgr_ref, dgn_ref):
        dgn_parts = []
        for h in range(RET_HEADS):
            sl = slice(h * dv, (h + 1) * dv)
            oo = o_ref[:, sl]
            gr = gr_ref[:, sl]
            dz = dz_ref[:, sl]
            gn = gn_ref[:, sl]
            mu = jnp.mean(oo, axis=-1, keepdims=True)
            oc = oo - mu
            rstd = lax.rsqrt(jnp.mean(oc * oc, axis=-1, keepdims=True) + LN_EPS)
            y = oc * rstd
            sg = _sigmoid(gr)
            act = gr * sg
            dyg = dz * act
            dgn_parts.append(jnp.sum(dyg * y, axis=0, keepdims=True))
            dy = dyg * gn
            do_ref[:, sl] = rstd * (dy - jnp.mean(dy, axis=-1, keepdims=True)
                                    - y * jnp.mean(dy * y, axis=-1, keepdims=True))
            dgr_ref[:, sl] = (dz * (y * gn) * (sg * (1.0 + gr * (1.0 - sg)))).astype(BF16)
        _acc_rows(dgn_ref, pl.program_id(0), jnp.concatenate(dgn_parts, axis=-1))

    return pl.pallas_call(
        body, grid=(s // t,), in_specs=[_row_spec(t, w), _row_spec(t, w, 2), _vec_spec(w), _row_spec(t, w)],
        out_specs=[_row_spec(t, w), _row_spec(t, w), _vec_spec(w)],
        out_shape=[_sds((s, w), F32), _sds((s, w), BF16), _sds((1, w), F32)], name="gn_gate_bwd",
        compiler_params=_params(("arbitrary",)))(o, p, gn_g, dz)


def retention_bwd(p, cos, sin, states, do, tables):
    s = p.shape[0]
    t = min(RET_TILE, s)
    nt = s // t
    dk, dv = RET_QK_DIM, RET_V_DIM

    def body(q_ref, k_ref, v_ref, cos_ref, sin_ref, di_ref, dq_ref, dkk_ref, dc_ref, st_ref, do_ref,
             gq_ref, gk_ref, gv_ref, dstate):
        @pl.when(pl.program_id(1) == 0)
        def _():
            dstate[...] = jnp.zeros_like(dstate)

        cs, sn = cos_ref[...], sin_ref[...]
        qb = (_rot(q_ref[...], cs, sn) * _QK_SCALE).astype(BF16)
        kr = _rot(k_ref[...], cs, sn)
        kb = kr.astype(BF16)
        vb = v_ref[...].astype(BF16)
        dmat, dkk = di_ref[...], dkk_ref[...]
        d_o = do_ref[...]
        dob = d_o.astype(BF16)
        dsb = dstate[...].astype(BF16)
        ab = (_nt(qb, kb) * dmat).astype(BF16)
        gv_ref[...] = (_tn(ab, dob) + _nn((kr * dkk).astype(BF16), dsb)).astype(BF16)
        dcb = (d_o * dq_ref[...]).astype(BF16)
        dpb = (_nt(dob, vb) * dmat).astype(BF16)
        dqq = _nt(dcb, st_ref[...]) + _nn(dpb, kb)
        dkv = _tn(dpb, qb) + _nt(vb, dsb) * dkk
        dstate[...] = dstate[...] * dc_ref[...] + _tn(qb, dcb)
        gq_ref[...] = _rot_t(dqq * _QK_SCALE, cs, sn).astype(BF16)
        gk_ref[...] = _rot_t(dkv, cs, sn).astype(BF16)

    rev = lambda i: nt - 1 - i
    qkv, rope, tabs = _retention_specs(t, nt, rev)
    in_specs = qkv + rope + tabs + [pl.BlockSpec((None, None, dk, dv), lambda h, i: (h, rev(i), 0, 0)),
                                    pl.BlockSpec((t, dv), lambda h, i: (rev(i), h))]
    out_specs = [pl.BlockSpec((t, dk), lambda h, i: (rev(i), h)),
                 pl.BlockSpec((t, dk), lambda h, i: (rev(i), h)),
                 pl.BlockSpec((t, dv), lambda h, i: (rev(i), h))]
    out_shape = [_sds((s, RET_HEADS * dk), BF16), _sds((s, RET_HEADS * dk), BF16), _sds((s, RET_HEADS * dv), BF16)]
    return pl.pallas_call(
        body, grid=(RET_HEADS, nt), in_specs=in_specs, out_specs=out_specs, out_shape=out_shape,
        scratch_shapes=[pltpu.VMEM((dk, dv), F32)], name="retention_bwd",
        compiler_params=_params(("parallel", "arbitrary")))(p, p, p, cos, sin, *tables, states, do)


A_COL, B_COL = 6, 7


def _prev_rows_spec(t, halo, width, col):
    per = t // halo
    return pl.BlockSpec((halo, width), lambda i: (jnp.maximum(i * per - 1, 0), col))


def _next_rows_spec(t, halo, width, col, n_rows):
    per = t // halo
    last = n_rows // halo - 1
    return pl.BlockSpec((halo, width), lambda i: (jnp.minimum((i + 1) * per, last), col))


def _shifted_copies(ext, rows):
    for b in range(1, 8):
        ext[b, pl.ds(0, rows - 8), :] = ext[0, pl.ds(b, rows - 8), :]


def _shifted(ext, start, lanes):
    return ext[start % 8, pl.ds(start - start % 8, STRIP_ROWS), lanes]


def conv_fwd(p, dw_w, dw_b, ln_g, ln_b):
    s = p.shape[0]
    t = min(ROW_TILE, s)
    c = D_MODEL
    hl = CONV_HALO

    def body(a_ref, b_ref, ah_ref, bh_ref, w_ref, wb_ref, g_ref, bb_ref, c1_ref, c3_ref, ext):
        i = pl.program_id(0)
        ext[0, pl.ds(0, hl), :] = jnp.where(i > 0, ah_ref[...] * _sigmoid(bh_ref[...]), 0.0)
        ext[0, pl.ds(hl, t), :] = a_ref[...] * _sigmoid(b_ref[...])
        _shifted_copies(ext, t + hl)
        first = hl - (CONV_WIDTH - 1)
        for lane in range(0, c, STRIP_LANES):
            ls = slice(lane, lane + STRIP_LANES)
            for r0 in range(0, t, STRIP_ROWS):
                accs = [jnp.broadcast_to(wb_ref[:, ls], (STRIP_ROWS, STRIP_LANES)),
                        jnp.zeros((STRIP_ROWS, STRIP_LANES), F32)]
                for j in range(CONV_WIDTH):
                    accs[j % 2] = accs[j % 2] + w_ref[j:j + 1, ls] * _shifted(ext, r0 + first + j, ls)
                c1_ref[r0:r0 + STRIP_ROWS, ls] = accs[0] + accs[1]
        acc = c1_ref[...]
        mu = jnp.mean(acc, axis=-1, keepdims=True)
        xc = acc - mu
        var = jnp.mean(xc * xc, axis=-1, keepdims=True)
        c2 = xc * lax.rsqrt(var + LN_EPS) * g_ref[...] + bb_ref[...]
        c3_ref[...] = (c2 * _sigmoid(c2)).astype(BF16)

    in_specs = [_row_spec(t, c, A_COL), _row_spec(t, c, B_COL),
                _prev_rows_spec(t, hl, c, A_COL), _prev_rows_spec(t, hl, c, B_COL),
                pl.BlockSpec((CONV_WIDTH, c), lambda i: (0, 0)), _vec_spec(c), _vec_spec(c), _vec_spec(c)]
    return pl.pallas_call(
        body, grid=(s // t,), in_specs=in_specs, out_specs=[_row_spec(t, c), _row_spec(t, c)],
        out_shape=[_sds((s, c), F32), _sds((s, c), BF16)], scratch_shapes=[pltpu.VMEM((8, t + hl, c), F32)],
        name="conv_fwd", compiler_params=_params(("parallel",)))(p, p, p, p, dw_w, dw_b, ln_g, ln_b)


def conv_ln_bwd(c1, ln_g, ln_b, dc3):
    s, c = c1.shape
    t = min(ROW_TILE, s)

    def body(c1_ref, g_ref, b_ref, d_ref, dc1_ref, dg_ref, db_ref):
        x = c1_ref[...]
        g = g_ref[...]
        mu = jnp.mean(x, axis=-1, keepdims=True)
        xc = x - mu
        rstd = lax.rsqrt(jnp.mean(xc * xc, axis=-1, keepdims=True) + LN_EPS)
        y = xc * rstd
        c2 = y * g + b_ref[...]
        sg = _sigmoid(c2)
        dc2 = d_ref[...] * (sg * (1.0 + c2 * (1.0 - sg)))
        i = pl.program_id(0)
        _acc_rows(db_ref, i, jnp.sum(dc2, axis=0, keepdims=True))
        _acc_rows(dg_ref, i, jnp.sum(dc2 * y, axis=0, keepdims=True))
        dy = dc2 * g
        dc1_ref[...] = rstd * (dy - jnp.mean(dy, axis=-1, keepdims=True)
                               - y * jnp.mean(dy * y, axis=-1, keepdims=True))

    return pl.pallas_call(
        body, grid=(s // t,), in_specs=[_row_spec(t, c), _vec_spec(c), _vec_spec(c), _row_spec(t, c)],
        out_specs=[_row_spec(t, c), _vec_spec(c), _vec_spec(c)],
        out_shape=[_sds((s, c), F32), _sds((1, c), F32), _sds((1, c), F32)], name="conv_ln_bwd",
        compiler_params=_params(("arbitrary",)))(c1, ln_g, ln_b, dc3)


def conv_dw_bwd(p, dc1, dw_w):
    s = p.shape[0]
    t = min(ROW_TILE, s)
    c = D_MODEL
    hl = CONV_HALO
    nt = s // t

    def body(a_ref, b_ref, ah_ref, bh_ref, d_ref, dn_ref, w_ref, dab_ref, dw_ref, dbias_ref, ext_c, ext_d, dc0_s,
             dw_s):
        i = pl.program_id(0)
        ext_c[0, pl.ds(0, hl), :] = jnp.where(i > 0, ah_ref[...] * _sigmoid(bh_ref[...]), 0.0)
        ext_c[0, pl.ds(hl, t), :] = a_ref[...] * _sigmoid(b_ref[...])
        ext_d[0, pl.ds(0, t), :] = d_ref[...]
        ext_d[0, pl.ds(t, hl), :] = jnp.where(i < nt - 1, dn_ref[...], 0.0)
        _shifted_copies(ext_c, t + hl)
        _shifted_copies(ext_d, t + hl)
        first = hl - (CONV_WIDTH - 1)

        def fold8(x):
            rows = [x[k:k + 8] for k in range(0, STRIP_ROWS, 8)]
            while len(rows) > 1:
                rows = [rows[k] + rows[k + 1] for k in range(0, len(rows), 2)]
            return rows[0]

        for lane in range(0, c, STRIP_LANES):
            ls = slice(lane, lane + STRIP_LANES)
            for r0 in range(0, t, STRIP_ROWS):
                accs = [jnp.zeros((STRIP_ROWS, STRIP_LANES), F32) for _ in range(2)]
                for j in range(CONV_WIDTH):
                    accs[j % 2] = accs[j % 2] + w_ref[j:j + 1, ls] * _shifted(ext_d, r0 + CONV_WIDTH - 1 - j, ls)
                dc0_s[r0:r0 + STRIP_ROWS, ls] = accs[0] + accs[1]
            for j0 in range(0, CONV_WIDTH, DW_TAPS):
                taps = range(j0, min(j0 + DW_TAPS, CONV_WIDTH))
                parts = [jnp.zeros((8, STRIP_LANES), F32) for _ in taps]
                for r0 in range(0, t, STRIP_ROWS):
                    d = ext_d[0, r0:r0 + STRIP_ROWS, ls]
                    for k, j in enumerate(taps):
                        parts[k] = parts[k] + fold8(d * _shifted(ext_c, r0 + first + j, ls))
                for k, j in enumerate(taps):
                    dw_s[j:j + 1, ls] = jnp.sum(parts[k], axis=0, keepdims=True)
        _acc_rows(dw_ref, i, dw_s[0:CONV_WIDTH, :])
        d = d_ref[...]
        _acc_rows(dbias_ref, i, jnp.sum(d, axis=0, keepdims=True))
        dc0 = dc0_s[...]
        a = a_ref[...]
        sb = _sigmoid(b_ref[...])
        dab_ref[:, :c] = (dc0 * sb).astype(BF16)
        dab_ref[:, c:] = (dc0 * a * sb * (1.0 - sb)).astype(BF16)

    in_specs = [_row_spec(t, c, A_COL), _row_spec(t, c, B_COL),
                _prev_rows_spec(t, hl, c, A_COL), _prev_rows_spec(t, hl, c, B_COL),
                _row_spec(t, c), _next_rows_spec(t, hl, c, 0, s),
                pl.BlockSpec((CONV_WIDTH, c), lambda i: (0, 0))]
    return pl.pallas_call(
        body, grid=(nt,), in_specs=in_specs,
        out_specs=[_row_spec(t, 2 * c), pl.BlockSpec((CONV_WIDTH, c), lambda i: (0, 0)), _vec_spec(c)],
        out_shape=[_sds((s, 2 * c), BF16), _sds((CONV_WIDTH, c), F32), _sds((1, c), F32)],
        scratch_shapes=[pltpu.VMEM((8, t + hl, c), F32), pltpu.VMEM((8, t + hl, c), F32), pltpu.VMEM((t, c), F32),
                        pltpu.VMEM((CONV_HALO, c), F32)], name="conv_dw_bwd",
        compiler_params=_params(("arbitrary",)))(p, p, p, p, dc1, dc1, dw_w)


GATE_COL = 4


def gate_mix_fwd(p, b_gate, y_a, y_b, b_conv_out):
    s = p.shape[0]
    t = min(ROW_TILE, s)
    c = D_MODEL

    def body(gt_ref, bg_ref, ya_ref, yb_ref, bc_ref, o_ref):
        gs = _sigmoid(gt_ref[...] + bg_ref[...])
        o_ref[...] = (gs[:, :c] * ya_ref[...] + gs[:, c:] * (yb_ref[...] + bc_ref[...])).astype(BF16)

    return pl.pallas_call(
        body, grid=(s // t,),
        in_specs=[_row_spec(t, 2 * c, GATE_COL), _vec_spec(2 * c), _row_spec(t, c), _row_spec(t, c), _vec_spec(c)],
        out_specs=_row_spec(t, c), out_shape=_sds((s, c), BF16), name="gate_mix_fwd",
        compiler_params=_params(("parallel",)))(p, b_gate, y_a, y_b, b_conv_out)


def gate_mix_bwd(p, b_gate, y_a, y_b, b_conv_out, dmix):
    s = p.shape[0]
    t = min(ROW_TILE, s)
    c = D_MODEL

    def body(gt_ref, bg_ref, ya_ref, yb_ref, bc_ref, d_ref, dya_ref, dyb_ref, dgt_ref, dbg_ref, dbc_ref):
        gs = _sigmoid(gt_ref[...] + bg_ref[...])
        ga, gb = gs[:, :c], gs[:, c:]
        d = d_ref[...]
        dya = ga * d
        dyb = gb * d
        dya_ref[...] = dya.astype(BF16)
        dyb_ref[...] = dyb.astype(BF16)
        dga = d * ya_ref[...] * ga * (1.0 - ga)
        dgb = d * (yb_ref[...] + bc_ref[...]) * gb * (1.0 - gb)
        dgt_ref[:, :c] = dga.astype(BF16)
        dgt_ref[:, c:] = dgb.astype(BF16)
        i = pl.program_id(0)
        _acc_rows(dbg_ref, i, jnp.concatenate([jnp.sum(dga, axis=0, keepdims=True),
                                               jnp.sum(dgb, axis=0, keepdims=True)], axis=-1))
        _acc_rows(dbc_ref, i, jnp.sum(dyb, axis=0, keepdims=True))

    return pl.pallas_call(
        body, grid=(s // t,),
        in_specs=[_row_spec(t, 2 * c, GATE_COL), _vec_spec(2 * c), _row_spec(t, c), _row_spec(t, c), _vec_spec(c),
                  _row_spec(t, c)],
        out_specs=[_row_spec(t, c), _row_spec(t, c), _row_spec(t, 2 * c), _vec_spec(2 * c), _vec_spec(c)],
        out_shape=[_sds((s, c), BF16), _sds((s, c), BF16), _sds((s, 2 * c), BF16), _sds((1, 2 * c), F32),
                   _sds((1, c), F32)],
        name="gate_mix_bwd", compiler_params=_params(("arbitrary",)))(p, b_gate, y_a, y_b, b_conv_out, dmix)


_X_SCALE = X_HEAD_DIM ** -0.5


def _softmax_rows(sc):
    m = jnp.max(sc, axis=-1, keepdims=True)
    e = jnp.exp(sc - m)
    return e / jnp.sum(e, axis=-1, keepdims=True)


def attn_fwd(qx, kv):
    s, d = qx.shape
    m = kv.shape[0]
    t = min(ROW_TILE, s)
    hd = X_HEAD_DIM

    def body(q_ref, kv_ref, o_ref):
        for h in range(X_HEADS):
            sl = slice(h * hd, (h + 1) * hd)
            kh = kv_ref[:, sl].astype(BF16)
            vh = kv_ref[:, d + h * hd:d + (h + 1) * hd].astype(BF16)
            pr = _softmax_rows(_nt(q_ref[:, sl], kh) * _X_SCALE)
            o_ref[:, sl] = _nn(pr.astype(BF16), vh).astype(BF16)

    return pl.pallas_call(
        body, grid=(s // t,), in_specs=[_row_spec(t, d), pl.BlockSpec((m, 2 * d), lambda i: (0, 0))],
        out_specs=_row_spec(t, d), out_shape=_sds((s, d), BF16), name="attn_fwd",
        compiler_params=_params(("parallel",)))(qx, kv)


def attn_bwd(qx, kv, dox):
    s, d = qx.shape
    m = kv.shape[0]
    t = min(ROW_TILE, s)
    hd = X_HEAD_DIM

    def body(q_ref, kv_ref, do_ref, dq_ref, dkv_ref):
        dks, dvs = [], []
        for h in range(X_HEADS):
            sl = slice(h * hd, (h + 1) * hd)
            qh = q_ref[:, sl]
            kh = kv_ref[:, sl].astype(BF16)
            vh = kv_ref[:, d + h * hd:d + (h + 1) * hd].astype(BF16)
            pr = _softmax_rows(_nt(qh, kh) * _X_SCALE)
            doh = do_ref[:, sl].astype(BF16)
            dpr = _nt(doh, vh)
            dvs.append(_tn(pr.astype(BF16), doh))
            ds = pr * (dpr - jnp.sum(dpr * pr, axis=-1, keepdims=True))
            dsb = (ds * _X_SCALE).astype(BF16)
            dq_ref[:, sl] = _nn(dsb, kh).astype(BF16)
            dks.append(_tn(dsb, qh))
        _acc_rows(dkv_ref, pl.program_id(0), jnp.concatenate(dks + dvs, axis=-1))

    return pl.pallas_call(
        body, grid=(s // t,),
        in_specs=[_row_spec(t, d), pl.BlockSpec((m, 2 * d), lambda i: (0, 0)), _row_spec(t, d)],
        out_specs=[_row_spec(t, d), pl.BlockSpec((m, 2 * d), lambda i: (0, 0))],
        out_shape=[_sds((s, d), BF16), _sds((m, 2 * d), F32)], name="attn_bwd",
        compiler_params=_params(("arbitrary",)))(qx, kv, dox)


def _offset_copies(ext, offsets, rows):
    for k, off in enumerate(offsets):
        ext[1 + k, pl.ds(0, rows), :] = ext[0, pl.ds(off, rows), :]


def _ffn_blocks(rows, lanes):
    return [(r0, slice(l0, l0 + FFN_LANES)) for r0 in range(0, rows, FFN_ROWS) for l0 in range(0, lanes, FFN_LANES)]


def ffn_act_fwd(up, dw_w, dw_b):
    s = up.shape[0]
    f = FFN_DIM
    t = min(ROW_TILE, s)
    hl = FFN_HALO

    def body(val_ref, gt_ref, gh_ref, w_ref, b_ref, o_ref, ext):
        i = pl.program_id(0)
        ext[0, pl.ds(0, hl), :] = jnp.where(i > 0, gh_ref[...], 0.0)
        ext[0, pl.ds(hl, t), :] = gt_ref[...]
        _offset_copies(ext, (hl - 2, hl - 1), t)
        for r0, ls in _ffn_blocks(t, f):
            gc = b_ref[:, ls] + w_ref[0:1, ls] * ext[1, pl.ds(r0, FFN_ROWS), ls] \
                + w_ref[1:2, ls] * ext[2, pl.ds(r0, FFN_ROWS), ls] + w_ref[2:3, ls] * ext[0, pl.ds(r0 + hl, FFN_ROWS), ls]
            o_ref[r0:r0 + FFN_ROWS, ls] = (gc * _sigmoid(gc) * val_ref[r0:r0 + FFN_ROWS, ls]).astype(BF16)

    return pl.pallas_call(
        body, grid=(s // t,),
        in_specs=[_row_spec(t, f, 0), _row_spec(t, f, 1), _prev_rows_spec(t, hl, f, 1),
                  pl.BlockSpec((3, f), lambda i: (0, 0)), _vec_spec(f)],
        out_specs=_row_spec(t, f), out_shape=_sds((s, f), BF16), scratch_shapes=[pltpu.VMEM((3, t + hl, f), F32)],
        name="ffn_act_fwd", compiler_params=_params(("parallel",)))(up, up, up, dw_w, dw_b)


def ffn_act_bwd(up, dw_w, dw_b, da):
    s = up.shape[0]
    f = FFN_DIM
    t = min(ROW_TILE, s)
    hl = FFN_HALO
    nt = s // t

    def body(val_ref, valn_ref, gt_ref, gp_ref, gn_ref, da_ref, dan_ref, w_ref, b_ref,
             dup_ref, dw_ref, db_ref, ext_g, ext_d, sums):
        i = pl.program_id(0)
        ext_g[0, pl.ds(0, hl), :] = jnp.where(i > 0, gp_ref[...], 0.0)
        ext_g[0, pl.ds(hl, t), :] = gt_ref[...]
        ext_g[0, pl.ds(hl + t, hl), :] = gn_ref[...]
        _offset_copies(ext_g, (hl - 2, hl - 1), t + hl)
        sums[...] = jnp.zeros_like(sums)

        def fold8(x):
            out = x[0:8]
            for k in range(8, x.shape[0], 8):
                out = out + x[k:k + 8]
            return out

        def gate_block(rows, off, ls, val, da_rows):
            taps = [ext_g[1, pl.ds(off, rows), ls], ext_g[2, pl.ds(off, rows), ls], ext_g[0, pl.ds(off + hl, rows), ls]]
            gc = b_ref[:, ls] + w_ref[0:1, ls] * taps[0] + w_ref[1:2, ls] * taps[1] + w_ref[2:3, ls] * taps[2]
            sg = _sigmoid(gc)
            return taps, gc * sg, da_rows * val * (sg * (1.0 + gc * (1.0 - sg)))

        for r0, ls in _ffn_blocks(t, f):
            rs = slice(r0, r0 + FFN_ROWS)
            da_rows = da_ref[rs, ls]
            taps, act, dgc = gate_block(FFN_ROWS, r0, ls, val_ref[rs, ls], da_rows)
            ext_d[0, rs, ls] = dgc
            dup_ref[rs, ls] = (da_rows * act).astype(BF16)
            for k in range(3):
                sums[8 * k:8 * k + 8, ls] += fold8(dgc * taps[k])
            sums[24:32, ls] += fold8(dgc)
        _, _, dgc_next = gate_block(hl, t, slice(None), valn_ref[...], dan_ref[...])
        ext_d[0, pl.ds(t, hl), :] = jnp.where(i < nt - 1, dgc_next, 0.0)
        _offset_copies(ext_d, (1, 2), t)
        for r0, ls in _ffn_blocks(t, f):
            gate_lanes = slice(f + ls.start, f + ls.stop)
            rs = pl.ds(r0, FFN_ROWS)
            dup_ref[r0:r0 + FFN_ROWS, gate_lanes] = (
                w_ref[2:3, ls] * ext_d[0, rs, ls] + w_ref[1:2, ls] * ext_d[1, rs, ls]
                + w_ref[0:1, ls] * ext_d[2, rs, ls]).astype(BF16)
        rows = [jnp.sum(sums[8 * k:8 * k + 8, :], axis=0, keepdims=True) for k in range(4)]
        _acc_rows(dw_ref, i, jnp.concatenate(rows[:3], axis=0))
        _acc_rows(db_ref, i, rows[3])

    in_specs = [_row_spec(t, f, 0), _next_rows_spec(t, hl, f, 0, s),
                _row_spec(t, f, 1), _prev_rows_spec(t, hl, f, 1), _next_rows_spec(t, hl, f, 1, s),
                _row_spec(t, f), _next_rows_spec(t, hl, f, 0, s),
                pl.BlockSpec((3, f), lambda i: (0, 0)), _vec_spec(f)]
    return pl.pallas_call(
        body, grid=(nt,), in_specs=in_specs,
        out_specs=[_row_spec(t, 2 * f), pl.BlockSpec((3, f), lambda i: (0, 0)), _vec_spec(f)],
        out_shape=[_sds((s, 2 * f), BF16), _sds((3, f), F32), _sds((1, f), F32)],
        scratch_shapes=[pltpu.VMEM((3, t + 2 * hl, f), F32), pltpu.VMEM((3, t + hl, f), F32),
                        pltpu.VMEM((32, f), F32)],
        name="ffn_act_bwd",
        compiler_params=_params(("arbitrary",)))(up, up, up, up, up, da, da, dw_w, dw_b)


def local_step(x, mem, positions, target, gw, sp):
    n_layers = gw["w_in"].shape[0]
    inv_freq = 1.0 / (ROPE_THETA ** (jnp.arange(0, RET_QK_DIM, 2, dtype=F32) / RET_QK_DIM))
    cos, sin = rope_tables(positions, inv_freq[None, :])
    tables = _decay_tables(min(RET_TILE, x.shape[0]))
    row = lambda name, l: sp[name][l][None, :]

    saved = []
    h = x
    for l in range(n_layers):
        a = {"h0": h}
        a["u"] = rms_cast(h, row("norm_mix_g", l))
        a["p"] = mm_fwd("mm_in", a["u"], gw["w_in"], l, True)
        a["o"], a["z"], a["states"] = retention_fwd(a["p"], cos, sin, row("ret_gn_g", l), tables)
        a["y_a"] = mm_fwd("mm_ret_out", a["z"], gw["w_ret_out"], l, False)
        a["c1"], a["c3"] = conv_fwd(a["p"], sp["conv_dw_w"][l], row("conv_dw_b", l), row("conv_ln_g", l),
                                    row("conv_ln_b", l))
        a["y_b"] = mm_fwd("mm_conv_out", a["c3"], gw["w_conv_out"], l, False)
        a["mixed"] = gate_mix_fwd(a["p"], row("b_gate", l), a["y_a"], a["y_b"], row("b_conv_out", l))
        a["h1"] = mm_fwd("mm_mix_out", a["mixed"], gw["w_mix_out"], l, False, res=h)
        a["hx"] = rms_cast(a["h1"], row("norm_xattn_g", l))
        a["qx"] = mm_fwd("mm_xq", a["hx"], gw["w_xq"], l, False, out_dtype=BF16)
        a["mem_n"] = rms_cast(mem, row("norm_mem_g", l))
        a["kv"] = mm_fwd("mm_xkv", a["mem_n"], gw["w_xkv"], l, True)
        a["ox"] = attn_fwd(a["qx"], a["kv"])
        a["h2"] = mm_fwd("mm_xo", a["ox"], gw["w_xo"], l, False, res=a["h1"])
        a["hf"] = rms_cast(a["h2"], row("norm_ffn_g", l))
        a["up"] = mm_fwd("mm_up", a["hf"], gw["w_up"], l, True)
        a["act"] = ffn_act_fwd(a["up"], sp["ffn_dw_w"][l], row("ffn_dw_b", l))
        h = mm_fwd("mm_down", a["act"], gw["w_down"], l, False, res=a["h2"])
        saved.append(a)

    dh, d_final_g, loss = loss_head(h, sp["norm_final_g"][None, :], target)

    big = {}

    def dw(name, key, act, dy, col, l):
        big[key] = mm_dw(name, act, dy, col, l, n_layers, big.get(key))

    small = {n: [None] * n_layers for n in SMALL_REPL + SMALL_SHARDED if n != "norm_final_g"}
    for l in range(n_layers - 1, -1, -1):
        a = saved[l]
        d_act = mm_dx("mm_down_dx", dh, gw["w_down"], l, False)
        dw("mm_down_dw", "w_down", a["act"], dh, False, l)
        d_up, small["ffn_dw_w"][l], small["ffn_dw_b"][l] = ffn_act_bwd(a["up"], sp["ffn_dw_w"][l],
                                                                        row("ffn_dw_b", l), d_act)
        d_hf = mm_dx("mm_up_dx", d_up, gw["w_up"], l, True)
        dw("mm_up_dw", "w_up", a["hf"], d_up, True, l)
        dh, small["norm_ffn_g"][l] = rms_bwd(a["h2"], row("norm_ffn_g", l), d_hf, dh)
        d_ox = mm_dx("mm_xo_dx", dh, gw["w_xo"], l, False)
        dw("mm_xo_dw", "w_xo", a["ox"], dh, False, l)
        d_qx, d_kv = attn_bwd(a["qx"], a["kv"], d_ox)
        d_hx = mm_dx("mm_xq_dx", d_qx, gw["w_xq"], l, False)
        dw("mm_xq_dw", "w_xq", a["hx"], d_qx, False, l)
        d_mem_n = mm_dx("mm_xkv_dx", d_kv, gw["w_xkv"], l, True)
        dw("mm_xkv_dw", "w_xkv", a["mem_n"], d_kv, True, l)
        _, small["norm_mem_g"][l] = rms_bwd(mem, row("norm_mem_g", l), d_mem_n)
        dh, small["norm_xattn_g"][l] = rms_bwd(a["h1"], row("norm_xattn_g", l), d_hx, dh)
        d_mixed = mm_dx("mm_mix_out_dx", dh, gw["w_mix_out"], l, False)
        dw("mm_mix_out_dw", "w_mix_out", a["mixed"], dh, False, l)
        d_ya, d_yb, dp_gate, small["b_gate"][l], small["b_conv_out"][l] = gate_mix_bwd(
            a["p"], row("b_gate", l), a["y_a"], a["y_b"], row("b_conv_out", l), d_mixed)
        d_c3 = mm_dx("mm_conv_out_dx", d_yb, gw["w_conv_out"], l, False)
        dw("mm_conv_out_dw", "w_conv_out", a["c3"], d_yb, False, l)
        d_c1, small["conv_ln_g"][l], small["conv_ln_b"][l] = conv_ln_bwd(a["c1"], row("conv_ln_g", l),
                                                                         row("conv_ln_b", l), d_c3)
        dp_conv, small["conv_dw_w"][l], small["conv_dw_b"][l] = conv_dw_bwd(a["p"], d_c1, sp["conv_dw_w"][l])
        d_z = mm_dx("mm_ret_out_dx", d_ya, gw["w_ret_out"], l, False)
        dw("mm_ret_out_dw", "w_ret_out", a["z"], d_ya, False, l)
        d_o, dp_gret, small["ret_gn_g"][l] = gn_gate_bwd(a["o"], a["p"], row("ret_gn_g", l), d_z)
        dp_q, dp_k, dp_v = retention_bwd(a["p"], cos, sin, a["states"], d_o, tables)
        dp = jnp.concatenate([dp_q, dp_k, dp_v, dp_gret, dp_conv, dp_gate], axis=1)
        d_u = mm_dx("mm_in_dx", dp, gw["w_in"], l, True)
        dw("mm_in_dw", "w_in", a["u"], dp, True, l)
        dh, small["norm_mix_g"][l] = rms_bwd(a["h0"], row("norm_mix_g", l), d_u, dh)

    small = {n: jnp.stack([g.reshape(sp[n].shape[1:]) for g in v]) for n, v in small.items()}
    small["norm_final_g"] = d_final_g.reshape(-1)
    return loss, dh, big, small


_ANY = pl.BlockSpec(memory_space=pl.ANY)


def _place():
    x, y, c = lax.axis_index("x"), lax.axis_index("y"), lax.axis_index("c")
    return x, y, c


def _other_chips(x, y):
    return [(1 - x, y), (x, 1 - y), (1 - x, 1 - y)]


def place_shard(w, chip):
    lyr, a, b = w.shape
    t = _flat_tile(a, b)

    def body(chip_ref, w_ref, o_ref):
        o_ref[...] = w_ref[...].astype(BF16)

    grid_spec = pltpu.PrefetchScalarGridSpec(
        num_scalar_prefetch=1, grid=(lyr, a // t),
        in_specs=[pl.BlockSpec((None, t, b), lambda l, i, cr: (l, i, 0))],
        out_specs=pl.BlockSpec((None, None, t, b), lambda l, i, cr: (l, cr[0], i, 0)))
    return pl.pallas_call(body, grid_spec=grid_spec, out_shape=_sds((lyr, N_CHIPS, a, b), BF16), name="place_shard",
                          compiler_params=_params(("parallel", "parallel")))(chip, w)


def gather_weights(bufs):
    n = len(bufs)

    def body(*refs):
        outs = refs[n:2 * n]
        ici_send, ici_recv, pair_send, pair_recv = refs[2 * n:]
        x, y, c = _place()
        mine = 2 * x + y
        chips = _other_chips(x, y)

        def part(w, core, slot):
            lh = outs[w].shape[0] // 2
            return outs[w].at[pl.ds(core * lh, lh), slot]

        def over_ici(w, j, slot):
            px, py = chips[j]
            return pltpu.make_async_remote_copy(
                src_ref=part(w, c, slot), dst_ref=part(w, c, slot), send_sem=ici_send.at[w, j],
                recv_sem=ici_recv.at[w, j], device_id=(px, py, c), device_id_type=MESH)

        def to_pair(w, j, core):
            px, py = chips[j]
            return pltpu.make_async_remote_copy(
                src_ref=part(w, core, 2 * px + py), dst_ref=part(w, core, 2 * px + py), send_sem=pair_send.at[w, j],
                recv_sem=pair_recv.at[w, j], device_id=(x, y, 1 - c), device_id_type=MESH)

        for w in range(n):
            for j in range(3):
                over_ici(w, j, mine).start()
        for w in range(n):
            for j, (px, py) in enumerate(chips):
                over_ici(w, j, 2 * px + py).wait_recv()
                to_pair(w, j, c).start()
        for w in range(n):
            for j in range(3):
                to_pair(w, j, 1 - c).wait_recv()
                to_pair(w, j, c).wait_send()
                over_ici(w, j, mine).wait_send()

    return pl.pallas_call(
        body, in_specs=[_ANY] * n, out_specs=[_ANY] * n, out_shape=[_sds(b.shape, b.dtype) for b in bufs],
        input_output_aliases={i: i for i in range(n)},
        scratch_shapes=[pltpu.SemaphoreType.DMA((n, 3))] * 4,
        name="gather_weights", compiler_params=_params())(*bufs)


def pair_exchange(grads):
    n = len(grads)

    def body(*refs):
        ins, outs = refs[:n], refs[n:2 * n]
        send_sems, recv_sems = refs[2 * n:]
        x, y, c = _place()
        cps = []
        for w in range(n):
            lh = ins[w].shape[0] // 2
            cp = pltpu.make_async_remote_copy(
                src_ref=ins[w].at[pl.ds((1 - c) * lh, lh)], dst_ref=outs[w], send_sem=send_sems.at[w],
                recv_sem=recv_sems.at[w], device_id=(x, y, 1 - c), device_id_type=MESH)
            cp.start()
            cps.append(cp)
        for cp in cps:
            cp.wait_send()
            cp.wait_recv()

    out_shape = [_sds((g.shape[0] // 2,) + g.shape[1:], g.dtype) for g in grads]
    return pl.pallas_call(
        body, in_specs=[_ANY] * n, out_specs=[_ANY] * n, out_shape=out_shape,
        scratch_shapes=[pltpu.SemaphoreType.DMA((n,)), pltpu.SemaphoreType.DMA((n,))],
        name="pair_exchange", compiler_params=_params())(*grads)


def chip_exchange(parts):
    n = len(parts)

    def body(*refs):
        ins, outs = refs[:n], refs[n:2 * n]
        send_sems, recv_sems = refs[2 * n:]
        x, y, c = _place()
        mine = 2 * x + y
        chips = _other_chips(x, y)

        def copy(w, j, slot):
            px, py = chips[j]
            return pltpu.make_async_remote_copy(
                src_ref=ins[w].at[:, 2 * px + py], dst_ref=outs[w].at[slot], send_sem=send_sems.at[w, j],
                recv_sem=recv_sems.at[w, j], device_id=(px, py, c), device_id_type=MESH)

        for w in range(n):
            for j in range(3):
                copy(w, j, mine).start()
        for w in range(n):
            for j, (px, py) in enumerate(chips):
                copy(w, j, 2 * px + py).wait_recv()
                copy(w, j, mine).wait_send()

    out_shape = [_sds((N_CHIPS, g.shape[0]) + g.shape[2:], g.dtype) for g in parts]
    return pl.pallas_call(
        body, in_specs=[_ANY] * n, out_specs=[_ANY] * n, out_shape=out_shape,
        scratch_shapes=[pltpu.SemaphoreType.DMA((n, 3)), pltpu.SemaphoreType.DMA((n, 3))],
        name="chip_exchange", compiler_params=_params())(*parts)


def pair_share(halves):
    n = len(halves)

    def body(*refs):
        ins, outs = refs[:n], refs[n:2 * n]
        send_sems, recv_sems = refs[2 * n:]
        x, y, c = _place()
        cps = []
        for w in range(n):
            cp = pltpu.make_async_remote_copy(
                src_ref=ins[w], dst_ref=outs[w], send_sem=send_sems.at[w], recv_sem=recv_sems.at[w],
                device_id=(x, y, 1 - c), device_id_type=MESH)
            cp.start()
            cps.append(cp)
        for cp in cps:
            cp.wait_send()
            cp.wait_recv()

    return pl.pallas_call(
        body, in_specs=[_ANY] * n, out_specs=[_ANY] * n, out_shape=[_sds(g.shape, g.dtype) for g in halves],
        scratch_shapes=[pltpu.SemaphoreType.DMA((n,)), pltpu.SemaphoreType.DMA((n,))],
        name="pair_share", compiler_params=_params())(*halves)


def all_reduce_small(vec):
    r, lanes = vec.shape

    def body(v_ref, o_ref, buf, send_sems, recv_sems):
        x, y, c = _place()
        me = 4 * x + 2 * y + c
        buf[me] = v_ref[...]
        cps = []
        for k in range(1, N_DEV):
            peer = (me + k) % N_DEV
            cp = pltpu.make_async_remote_copy(
                src_ref=v_ref, dst_ref=buf.at[me], send_sem=send_sems.at[k - 1], recv_sem=recv_sems.at[k - 1],
                device_id=(peer // 4, (peer // 2) % 2, peer % 2), device_id_type=MESH)
            cp.start()
            cps.append(cp)
        for k in range(1, N_DEV):
            src = (me + N_DEV - k) % N_DEV
            cps[k - 1].wait_send()
            pltpu.make_async_remote_copy(
                src_ref=v_ref, dst_ref=buf.at[src], send_sem=send_sems.at[k - 1], recv_sem=recv_sems.at[k - 1],
                device_id=(src // 4, (src // 2) % 2, src % 2), device_id_type=MESH).wait_recv()
        acc = buf[0]
        for d in range(1, N_DEV):
            acc = acc + buf[d]
        o_ref[...] = acc

    vm = pl.BlockSpec(memory_space=pltpu.VMEM)
    return pl.pallas_call(
        body, in_specs=[vm], out_specs=vm, out_shape=_sds((r, lanes), F32),
        scratch_shapes=[pltpu.VMEM((N_DEV, r, lanes), F32), pltpu.SemaphoreType.DMA((N_DEV - 1,)),
                        pltpu.SemaphoreType.DMA((N_DEV - 1,))],
        name="all_reduce_small", compiler_params=_params())(vec)


ELEMENTWISE_BLOCK_BYTES = 1 << 20


def _flat_tile(rows, cols):
    for t in (512, 256, 128, 64, 32, 16, 8):
        if rows % t == 0 and t * cols * 4 <= ELEMENTWISE_BLOCK_BYTES:
            return t
    return rows


def add_pair(g, r, half):
    lyr, _, a, b = g.shape
    lh = lyr // 2
    rows = lh * 4 * a
    t = _flat_tile(rows, b)
    nb = rows // t
    g2 = g.reshape(lyr * 4 * a, b)
    r2 = r.reshape(rows, b)

    def body(half_ref, g_ref, r_ref, o_ref):
        o_ref[...] = (g_ref[...] + r_ref[...]).astype(BF16)

    grid_spec = pltpu.PrefetchScalarGridSpec(
        num_scalar_prefetch=1, grid=(nb,),
        in_specs=[pl.BlockSpec((t, b), lambda i, hr: (hr[0] * nb + i, 0)), pl.BlockSpec((t, b), lambda i, hr: (i, 0))],
        out_specs=pl.BlockSpec((t, b), lambda i, hr: (i, 0)))
    out = pl.pallas_call(body, grid_spec=grid_spec, out_shape=_sds((rows, b), BF16), name="add_pair",
                         compiler_params=_params(("parallel",)))(half, g2, r2)
    return out.reshape(lh, 4, a, b)


def sum_chips(own, parts, chip):
    _, lh, a, b = parts.shape
    t = _flat_tile(a, b)

    def body(chip_ref, own_ref, p_ref, o_ref):
        mine = chip_ref[0]

        def term(s):
            return jnp.where(mine == s, own_ref[...], p_ref[s]).astype(F32)

        o_ref[...] = ((term(0) + term(1)) + term(2)) + term(3)

    grid_spec = pltpu.PrefetchScalarGridSpec(
        num_scalar_prefetch=1, grid=(lh, a // t),
        in_specs=[pl.BlockSpec((None, None, t, b), lambda l, i, cr: (l, cr[0], i, 0)),
                  pl.BlockSpec((N_CHIPS, None, t, b), lambda l, i, cr: (0, l, i, 0))],
        out_specs=pl.BlockSpec((None, t, b), lambda l, i, cr: (l, i, 0)))
    return pl.pallas_call(body, grid_spec=grid_spec, out_shape=_sds((lh, a, b), F32), name="sum_chips",
                          compiler_params=_params(("parallel", "parallel")))(chip, own, parts)


def _adamw_math(w, g, m, v):
    mm = ADAM_B1 * m + (1.0 - ADAM_B1) * g
    vv = ADAM_B2 * v + (1.0 - ADAM_B2) * jnp.square(g)
    m_hat = mm / (1.0 - ADAM_B1 ** ADAM_STEP)
    v_hat = vv / (1.0 - ADAM_B2 ** ADAM_STEP)
    return -ADAM_LR * (m_hat / (jnp.sqrt(v_hat) + ADAM_EPS) + ADAM_WD * w), mm, vv


def adamw(w, g, m, v):
    shape = w.shape
    c = shape[-1]
    rows = int(np.prod(shape[:-1])) if len(shape) > 1 else 1
    t = _flat_tile(rows, c)
    flat = lambda z: z.reshape(rows, c)

    def body(w_ref, g_ref, m_ref, v_ref, d_ref, nm_ref, nv_ref):
        d_ref[...], nm_ref[...], nv_ref[...] = _adamw_math(w_ref[...], g_ref[...], m_ref[...], v_ref[...])

    spec = pl.BlockSpec((t, c), lambda i: (i, 0))
    outs = pl.pallas_call(body, grid=(rows // t,), in_specs=[spec] * 4, out_specs=[spec] * 3,
                          out_shape=[_sds((rows, c), F32)] * 3, name="adamw",
                          compiler_params=_params(("parallel",)))(flat(w), flat(g), flat(m), flat(v))
    return tuple(o.reshape(shape) for o in outs)


def adamw_halves(w, g_own, g_other, m, v, core):
    lyr, a, b = w.shape
    lh = lyr // 2
    t = _flat_tile(a, b)

    def body(core_ref, w_ref, go_ref, gs_ref, m_ref, v_ref, g_ref, d_ref, nm_ref, nv_ref):
        own = pl.program_id(0) // lh == core_ref[0]
        g = jnp.where(own, go_ref[...], gs_ref[...])
        g_ref[...] = g
        d_ref[...], nm_ref[...], nv_ref[...] = _adamw_math(w_ref[...], g, m_ref[...], v_ref[...])

    full = pl.BlockSpec((None, t, b), lambda l, i, cr: (l, i, 0))
    own_spec = pl.BlockSpec((None, t, b), lambda l, i, cr: (jnp.clip(l - cr[0] * lh, 0, lh - 1), i, 0))
    other_spec = pl.BlockSpec((None, t, b), lambda l, i, cr: (jnp.clip(l - (1 - cr[0]) * lh, 0, lh - 1), i, 0))
    grid_spec = pltpu.PrefetchScalarGridSpec(
        num_scalar_prefetch=1, grid=(lyr, a // t), in_specs=[full, own_spec, other_spec, full, full],
        out_specs=[full] * 4)
    return pl.pallas_call(body, grid_spec=grid_spec, out_shape=[_sds(w.shape, F32)] * 4, name="adamw_halves",
                          compiler_params=_params(("parallel", "parallel")))(core, w, g_own, g_other, m, v)


def _pack(parts):
    flat = jnp.concatenate([p.reshape(-1) for p in parts])
    pad = (-flat.shape[0]) % 1024
    return jnp.pad(flat, (0, pad)).reshape(-1, 128)


def _unpack(packed, shapes):
    flat = packed.reshape(-1)
    out, off = [], 0
    for shp in shapes:
        size = int(np.prod(shp))
        out.append(flat[off:off + size].reshape(shp))
        off += size
    return out


def kernel(x, mem, positions, norm_mix_g, w_in, b_gate, ret_gn_g, w_ret_out, conv_dw_w, conv_dw_b, conv_ln_g, conv_ln_b, w_conv_out, b_conv_out, w_mix_out, norm_xattn_g, norm_mem_g, w_xq, w_xkv, w_xo, norm_ffn_g, w_up, ffn_dw_w, ffn_dw_b, w_down, norm_final_g, loss_target, m_norm_mix_g, m_w_in, m_b_gate, m_ret_gn_g, m_w_ret_out, m_conv_dw_w, m_conv_dw_b, m_conv_ln_g, m_conv_ln_b, m_w_conv_out, m_b_conv_out, m_w_mix_out, m_norm_xattn_g, m_norm_mem_g, m_w_xq, m_w_xkv, m_w_xo, m_norm_ffn_g, m_w_up, m_ffn_dw_w, m_ffn_dw_b, m_w_down, m_norm_final_g, v_norm_mix_g, v_w_in, v_b_gate, v_ret_gn_g, v_w_ret_out, v_conv_dw_w, v_conv_dw_b, v_conv_ln_g, v_conv_ln_b, v_w_conv_out, v_b_conv_out, v_w_mix_out, v_norm_xattn_g, v_norm_mem_g, v_w_xq, v_w_xkv, v_w_xo, v_norm_ffn_g, v_w_up, v_ffn_dw_w, v_ffn_dw_b, v_w_down, v_norm_final_g):
    args = locals()
    w = {n: args[n] for n in WEIGHTS}
    m = {n: args["m_" + n] for n in WEIGHTS}
    v = {n: args["v_" + n] for n in WEIGHTS}
    chip = 2 * lax.axis_index("x") + lax.axis_index("y")
    core = lax.axis_index("c")

    chip_op = chip.reshape(1).astype(jnp.int32)
    core_op = core.reshape(1).astype(jnp.int32)
    gathered = gather_weights([place_shard(w[n], chip_op) for n in BIG])
    gw = dict(zip(BIG, gathered))

    sp = {n: w[n] for n in SMALL_REPL}
    placed = []
    for n in SMALL_SHARDED:
        cols = w[n].shape[-1]
        full = jnp.zeros(w[n].shape[:-1] + (N_CHIPS * cols,), F32)
        placed.append(lax.dynamic_update_slice_in_dim(full, w[n], chip * cols, axis=2))
    placed_shapes = [p.shape for p in placed]
    gathered_small = all_reduce_small(_pack([jnp.where(core == 0, p, 0.0) for p in placed]))
    for n, arr in zip(SMALL_SHARDED, _unpack(gathered_small, placed_shapes)):
        sp[n] = arr

    loss, grad_x, big, small = local_step(x[0], mem[0], positions.reshape(-1, 1), loss_target[0], gw, sp)

    names = [n for n in SMALL_REPL + SMALL_SHARDED]
    shapes = [small[n].shape for n in names] + [(128,)]
    reduced = _unpack(all_reduce_small(_pack([small[n] for n in names] + [loss.reshape(-1)])), shapes)
    grads = dict(zip(names, reduced[:-1]))
    loss_out = reduced[-1][0]
    for n in SMALL_SHARDED:
        cols = w[n].shape[-1]
        grads[n] = lax.dynamic_slice_in_dim(grads[n], chip * cols, cols, axis=2)

    blist = [big[n] for n in BIG]
    from_pair = pair_exchange(blist)
    pair_sum = [add_pair(g, r, core_op) for g, r in zip(blist, from_pair)]
    from_chips = chip_exchange(pair_sum)
    halves = [sum_chips(own, parts, chip_op) for own, parts in zip(pair_sum, from_chips)]
    other_halves = pair_share(halves)

    delta, new_m, new_v = {}, {}, {}
    for n, g_own, g_other in zip(BIG, halves, other_halves):
        grads[n], delta[n], new_m[n], new_v[n] = adamw_halves(w[n], g_own, g_other, m[n], v[n], core_op)
    for n in WEIGHTS:
        if n not in BIG:
            delta[n], new_m[n], new_v[n] = adamw(w[n], grads[n], m[n], v[n])
    return (loss_out, grad_x[None], *[grads[n] for n in WEIGHTS], *[delta[n] for n in WEIGHTS],
            *[new_m[n] for n in WEIGHTS], *[new_v[n] for n in WEIGHTS])
```

```python
import functools

import jax
import jax.numpy as jnp
import numpy as np
from jax import lax
from jax.experimental import pallas as pl
from jax.experimental.pallas import tpu as pltpu

F32 = jnp.float32
BF16 = jnp.bfloat16
MESH = pl.DeviceIdType.MESH

D_MODEL = 1024
CHUNK = 64
RET_HEADS = 4
RET_QK_DIM = 256
RET_V_DIM = 512
ROPE_THETA = 10000.0
CONV_WIDTH = 31
X_HEADS = 4
X_HEAD_DIM = 256
FFN_DIM = 2816
RMS_EPS = 1e-6
LN_EPS = 1e-5
ADAM_LR = 0.001
ADAM_B1 = 0.9
ADAM_B2 = 0.999
ADAM_EPS = 1e-08
ADAM_WD = 0.01
ADAM_STEP = 10

N_CHIPS = 4
N_DEV = 8
CONV_HALO = 32
FFN_HALO = 8
V7X_VMEM_LIMIT = 56 * 1024 * 1024
ROW_TILE = 256
STRIP_ROWS = 16
STRIP_LANES = 1024
DW_TAPS = 2
FFN_ROWS = 16
FFN_LANES = 256
MM_TILE_M = 1024
RET_TILE = 512
ATTN_TILE = 512

BIG = ("w_in", "w_ret_out", "w_conv_out", "w_mix_out", "w_xq", "w_xkv", "w_xo", "w_up", "w_down")
COL_SHARDED = ("w_in", "w_xkv", "w_up")
SMALL_REPL = ("norm_mix_g", "b_gate", "ret_gn_g", "conv_dw_b", "conv_ln_g", "conv_ln_b", "b_conv_out",
              "norm_xattn_g", "norm_mem_g", "norm_ffn_g", "ffn_dw_b", "norm_final_g")
SMALL_SHARDED = ("conv_dw_w", "ffn_dw_w")
WEIGHTS = ('norm_mix_g', 'w_in', 'b_gate', 'ret_gn_g', 'w_ret_out', 'conv_dw_w', 'conv_dw_b', 'conv_ln_g',
           'conv_ln_b', 'w_conv_out', 'b_conv_out', 'w_mix_out', 'norm_xattn_g', 'norm_mem_g', 'w_xq', 'w_xkv',
           'w_xo', 'norm_ffn_g', 'w_up', 'ffn_dw_w', 'ffn_dw_b', 'w_down', 'norm_final_g')


def _params(sem=None):
    return pltpu.CompilerParams(dimension_semantics=sem, vmem_limit_bytes=V7X_VMEM_LIMIT)


def _sds(shape, dtype):
    return jax.ShapeDtypeStruct(tuple(shape), dtype)


def _sigmoid(x):
    return jax.nn.sigmoid(x)


def _dot(a, b, ca, cb):
    return lax.dot_general(a, b, (((ca,), (cb,)), ((), ())), preferred_element_type=F32)


def _nn(a, b):
    return _dot(a, b, 1, 0)


def _nt(a, b):
    return _dot(a, b, 1, 1)


def _tn(a, b):
    return _dot(a, b, 0, 0)


def _mm(name, dims, grid, in_specs, out_spec, out_sds, nk, operands, res=False, norm=False, rms_bwd=False):
    n_in = 2 + res + norm + 3 * rms_bwd

    def body(*refs):
        ins, outs = refs[:n_in], refs[n_in:]
        a_ref, b_ref = ins[:2]
        extra = list(ins[2:])
        r_ref = extra.pop(0) if res else None
        g_ref = extra.pop(0) if norm else None
        o_ref = outs[0]
        row_tile = pl.program_id(0)
        prod = _dot(a_ref[...].astype(BF16), b_ref[...].astype(BF16), *dims)

        def finish(total):
            if rms_bwd:
                h_ref, gain_ref, dres_ref = extra
                dx, dg = _rms_bwd_math(h_ref[...], gain_ref[...], total)
                o_ref[...] = dx + dres_ref[...]
                _acc_rows(outs[1], row_tile, dg)
                return
            o_ref[...] = total.astype(o_ref.dtype)
            if norm:
                r = lax.rsqrt(jnp.mean(total * total, axis=-1, keepdims=True) + RMS_EPS)
                outs[1][...] = (total * r * g_ref[...]).astype(BF16)

        if nk == 1:
            finish(prod + r_ref[...] if res else prod)
        else:
            k = pl.program_id(2)

            @pl.when(k == 0)
            def _():
                o_ref[...] = (prod + r_ref[...]) if res else prod

            @pl.when((k > 0) & (k < nk - 1))
            def _():
                o_ref[...] += prod

            @pl.when(k == nk - 1)
            def _():
                finish(o_ref[...] + prod)

    assert nk == 1 or out_sds.dtype == F32
    out_specs, out_shape = [out_spec], [out_sds]
    if norm:
        out_specs.append(out_spec)
        out_shape.append(_sds(out_sds.shape, BF16))
    if rms_bwd:
        n = out_sds.shape[1]
        out_specs.append(pl.BlockSpec((1, n), lambda i, j, k: (0, 0)))
        out_shape.append(_sds((1, n), F32))
    sem = ("arbitrary",) * 3 if rms_bwd else ("parallel", "parallel", "arbitrary")
    out = pl.pallas_call(body, grid=grid, in_specs=in_specs, out_specs=out_specs, out_shape=out_shape, name=name,
                         compiler_params=_params(sem))(*operands)
    return out if (norm or rms_bwd) else out[0]


def _div_tile(n, want):
    best = None
    for t in range(128, min(n, want) + 1, 128):
        if n % t == 0:
            best = t
    assert best is not None, (n, want)
    return best


def mm_fwd(name, a, g, l, col, out_dtype=F32, res=None, norm_g=None):
    m, k_dim = a.shape
    tm = min(MM_TILE_M, m)
    if col:
        _, _, kk, b = g.shape
        assert kk == k_dim
        tn = _div_tile(b, 1408)
        nps = b // tn
        n = 4 * b
        grid = (m // tm, n // tn, 1)
        in_specs = [pl.BlockSpec((tm, k_dim), lambda i, j, k: (i, 0)),
                    pl.BlockSpec((None, None, k_dim, tn), lambda i, j, k: (l, j // nps, 0, j % nps))]
        nk = 1
        w = g
    else:
        lyr, _, a_rows, n = g.shape
        assert 4 * a_rows == k_dim
        w = g.reshape(lyr, k_dim, n)
        tk = _div_tile(k_dim, 1408)
        tn = n
        nk = k_dim // tk
        grid = (m // tm, 1, nk)
        in_specs = [pl.BlockSpec((tm, tk), lambda i, j, k: (i, k)),
                    pl.BlockSpec((None, tk, tn), lambda i, j, k: (l, k, j))]
    ops = [a, w]
    if res is not None:
        in_specs.append(pl.BlockSpec((tm, tn), lambda i, j, k: (i, j)))
        ops.append(res)
    if norm_g is not None:
        assert tn == n
        in_specs.append(pl.BlockSpec((1, n), lambda i, j, k: (0, 0)))
        ops.append(norm_g)
    return _mm(name, (1, 0), grid, in_specs, pl.BlockSpec((tm, tn), lambda i, j, k: (i, j)), _sds((m, n), out_dtype),
               nk, ops, res=res is not None, norm=norm_g is not None)


def mm_dx(name, dy, g, l, col, rms=None):
    m, n = dy.shape
    tm = min(MM_TILE_M, m)
    if col:
        _, _, k_dim, b = g.shape
        assert 4 * b == n
        tk = _div_tile(b, 1408)
        nps = b // tk
        nk = n // tk
        grid = (m // tm, 1, nk)
        in_specs = [pl.BlockSpec((tm, tk), lambda i, j, k: (i, k)),
                    pl.BlockSpec((None, None, k_dim, tk), lambda i, j, k: (l, k // nps, 0, k % nps))]
        out_spec = pl.BlockSpec((tm, k_dim), lambda i, j, k: (i, 0))
        w = g
        tno = k_dim
    else:
        lyr, _, a_rows, nn_ = g.shape
        assert nn_ == n
        k_dim = 4 * a_rows
        w = g.reshape(lyr, k_dim, n)
        tno = _div_tile(k_dim, 1408)
        nk = 1
        grid = (m // tm, k_dim // tno, 1)
        in_specs = [pl.BlockSpec((tm, n), lambda i, j, k: (i, 0)),
                    pl.BlockSpec((None, tno, n), lambda i, j, k: (l, j, 0))]
        out_spec = pl.BlockSpec((tm, tno), lambda i, j, k: (i, j))
    ops = [dy, w]
    if rms is not None:
        assert tno == k_dim
        h, gain, dres = rms
        rows = pl.BlockSpec((tm, k_dim), lambda i, j, k: (i, 0))
        in_specs += [rows, pl.BlockSpec((1, k_dim), lambda i, j, k: (0, 0)), rows]
        ops += [h, gain, dres]
    return _mm(name, (1, 1), grid, in_specs, out_spec, _sds((m, k_dim), F32), nk, ops, rms_bwd=rms is not None)


def mm_dw(name, a, dy, col, l, n_layers, into=None):
    m, k_dim = a.shape
    _, n = dy.shape
    ts = min(MM_TILE_M, m)
    ns = m // ts
    tko = _div_tile(k_dim, 1408)
    if col:
        b = n // 4
        tn = _div_tile(b, 1408)
        nps = b // tn
        grid = (k_dim // tko, n // tn, ns)
        out_spec = pl.BlockSpec((None, None, tko, tn), lambda i, j, s: (l, j // nps, i, j % nps))
        shape = (n_layers, 4, k_dim, b)
    else:
        tn = n
        grid = (k_dim // tko, 1, ns)
        out_spec = pl.BlockSpec((None, tko, tn), lambda i, j, s: (l, i, j))
        shape = (n_layers, k_dim, n)
    in_specs = [pl.BlockSpec((ts, tko), lambda i, j, s: (s, i)),
                pl.BlockSpec((ts, tn), lambda i, j, s: (s, j))]
    ops = [a, dy]
    aliases = {}
    if into is not None:
        in_specs.append(_ANY)
        ops.append(into.reshape(shape))
        aliases = {2: 0}

    def body(a_ref, b_ref, *rest):
        o_ref = rest[-1]
        prod = _tn(a_ref[...].astype(BF16), b_ref[...].astype(BF16))
        if ns == 1:
            o_ref[...] = prod
        else:
            s = pl.program_id(2)

            @pl.when(s == 0)
            def _():
                o_ref[...] = prod

            @pl.when(s > 0)
            def _():
                o_ref[...] += prod

    out = pl.pallas_call(body, grid=grid, in_specs=in_specs, out_specs=out_spec, out_shape=_sds(shape, F32),
                         input_output_aliases=aliases, name=name,
                         compiler_params=_params(("parallel", "parallel", "arbitrary")))(*ops)
    return out.reshape(n_layers, 4, k_dim if col else k_dim // 4, shape[-1])


def _row_spec(t, c, col=0):
    return pl.BlockSpec((t, c), lambda i: (i, col))


def _vec_spec(c):
    return pl.BlockSpec((1, c), lambda i: (0, 0))


def _acc_rows(ref, i, val):
    @pl.when(i == 0)
    def _():
        ref[...] = val

    @pl.when(i > 0)
    def _():
        ref[...] += val


def rope_tables(positions, inv_freq):
    s = positions.shape[0]
    t = min(ROW_TILE, s)
    half = inv_freq.shape[1]

    def body(p_ref, f_ref, c_ref, s_ref):
        ang = p_ref[...].astype(F32) * f_ref[...]
        c_ref[...] = jnp.cos(ang)
        s_ref[...] = jnp.sin(ang)

    return pl.pallas_call(
        body, grid=(s // t,), in_specs=[_row_spec(t, 1), _vec_spec(half)],
        out_specs=[_row_spec(t, half), _row_spec(t, half)], out_shape=[_sds((s, half), F32)] * 2, name="rope_tables",
        compiler_params=_params(("parallel",)))(positions, inv_freq)


def rms_cast(h, g):
    s, d = h.shape
    t = min(ROW_TILE, s)

    def body(h_ref, g_ref, o_ref):
        x = h_ref[...]
        r = lax.rsqrt(jnp.mean(x * x, axis=-1, keepdims=True) + RMS_EPS)
        o_ref[...] = (x * r * g_ref[...]).astype(BF16)

    return pl.pallas_call(body, grid=(s // t,), in_specs=[_row_spec(t, d), _vec_spec(d)], out_specs=_row_spec(t, d),
                          out_shape=_sds((s, d), BF16), name="rms_cast", compiler_params=_params(("parallel",)))(h, g)


def _rms_bwd_math(x, g, du):
    r = lax.rsqrt(jnp.mean(x * x, axis=-1, keepdims=True) + RMS_EPS)
    gd = g * du
    dx = r * gd - x * (r * r * r) * jnp.mean(x * gd, axis=-1, keepdims=True)
    dg = jnp.sum(x * r * du, axis=0, keepdims=True)
    return dx, dg


def rms_bwd(h, g, du, dres=None):
    s, d = h.shape
    t = min(ROW_TILE, s)

    def body(*refs):
        if dres is None:
            h_ref, g_ref, du_ref, dh_ref, dg_ref = refs
        else:
            h_ref, g_ref, du_ref, dr_ref, dh_ref, dg_ref = refs
        dx, dg = _rms_bwd_math(h_ref[...], g_ref[...], du_ref[...])
        if dres is not None:
            dx = dx + dr_ref[...]
        dh_ref[...] = dx
        _acc_rows(dg_ref, pl.program_id(0), dg)

    in_specs = [_row_spec(t, d), _vec_spec(d), _row_spec(t, d)]
    ops = [h, g, du]
    if dres is not None:
        in_specs.append(_row_spec(t, d))
        ops.append(dres)
    return pl.pallas_call(body, grid=(s // t,), in_specs=in_specs, out_specs=[_row_spec(t, d), _vec_spec(d)],
                          out_shape=[_sds((s, d), F32), _sds((1, d), F32)], name="rms_bwd",
                          compiler_params=_params(("arbitrary",)))(*ops)


def loss_head(h, g, target):
    s, d = h.shape
    t = min(ROW_TILE, s)

    def body(h_ref, g_ref, t_ref, dh_ref, dg_ref, loss_ref):
        x = h_ref[...]
        gg = g_ref[...]
        r = lax.rsqrt(jnp.mean(x * x, axis=-1, keepdims=True) + RMS_EPS)
        err = x * r * gg - t_ref[...]
        part = 0.5 * jnp.sum(jnp.mean(err * err, axis=-1, keepdims=True), axis=0, keepdims=True)
        dy = err * (1.0 / d)
        dx, dg = _rms_bwd_math(x, gg, dy)
        dh_ref[...] = dx
        i = pl.program_id(0)
        _acc_rows(dg_ref, i, dg)
        _acc_rows(loss_ref, i, jnp.broadcast_to(part, (1, 128)))

    return pl.pallas_call(
        body, grid=(s // t,), in_specs=[_row_spec(t, d), _vec_spec(d), _row_spec(t, d)],
        out_specs=[_row_spec(t, d), _vec_spec(d), _vec_spec(128)],
        out_shape=[_sds((s, d), F32), _sds((1, d), F32), _sds((1, 128), F32)], name="loss_head",
        compiler_params=_params(("arbitrary",)))(h, g, target)


def _rot(x, cos, sin):
    half = x.shape[-1] // 2
    x1, x2 = x[:, :half], x[:, half:]
    return jnp.concatenate([x1 * cos - x2 * sin, x2 * cos + x1 * sin], axis=-1)


def _rot_t(dy, cos, sin):
    half = dy.shape[-1] // 2
    d1, d2 = dy[:, :half], dy[:, half:]
    return jnp.concatenate([d1 * cos + d2 * sin, d2 * cos - d1 * sin], axis=-1)


def _decay_tables(t):
    log_gamma = jnp.log(1.0 - jnp.power(2.0, -5.0 - jnp.arange(RET_HEADS, dtype=F32)))
    idx = jnp.arange(t, dtype=F32)
    dist = jnp.abs(idx[:, None] - idx[None, :])
    chunk = jnp.arange(t) // CHUNK
    seen = chunk[None, :] <= chunk[:, None]
    d_tile = jnp.where(seen[None], jnp.exp(log_gamma[:, None, None] * dist), 0.0)
    decay_q = jnp.exp(log_gamma[:, None] * (idx[None, :] + 1.0))[:, :, None]
    decay_k = jnp.exp(log_gamma[:, None] * (t - 1.0 - idx[None, :]))[:, :, None]
    decay_tile = jnp.exp(log_gamma * t)[:, None, None]
    return d_tile, decay_q, decay_k, decay_tile


_QK_SCALE = RET_QK_DIM ** -0.5


def _retention_specs(t, nt, order):
    dk, dv = RET_QK_DIM, RET_V_DIM
    hmap = lambda h, i: (h, 0, 0)
    qkv = [pl.BlockSpec((t, dk), lambda h, i: (order(i), h)),
           pl.BlockSpec((t, dk), lambda h, i: (order(i), RET_HEADS + h)),
           pl.BlockSpec((t, dv), lambda h, i: (order(i), 4 + h))]
    rope = [pl.BlockSpec((t, dk // 2), lambda h, i: (order(i), 0))] * 2
    tables = [pl.BlockSpec((None, t, t), hmap), pl.BlockSpec((None, t, 1), hmap), pl.BlockSpec((None, t, 1), hmap),
              pl.BlockSpec((None, 1, 1), hmap)]
    return qkv, rope, tables


def retention_fwd(p, cos, sin, gn_g, tables):
    s = p.shape[0]
    t = min(RET_TILE, s)
    nt = s // t
    dk, dv = RET_QK_DIM, RET_V_DIM

    def body(q_ref, k_ref, v_ref, cos_ref, sin_ref, di_ref, dq_ref, dkk_ref, dc_ref, gr_ref, gn_ref,
             o_ref, z_ref, st_ref, state):
        @pl.when(pl.program_id(1) == 0)
        def _():
            state[...] = jnp.zeros_like(state)

        cs, sn = cos_ref[...], sin_ref[...]
        qb = (_rot(q_ref[...], cs, sn) * _QK_SCALE).astype(BF16)
        kr = _rot(k_ref[...], cs, sn)
        vb = v_ref[...].astype(BF16)
        st = state[...].astype(BF16)
        st_ref[...] = st
        scores = _nt(qb, kr.astype(BF16)) * di_ref[...]
        o = _nn(scores.astype(BF16), vb) + _nn(qb, st) * dq_ref[...]
        state[...] = state[...] * dc_ref[...] + _tn((kr * dkk_ref[...]).astype(BF16), vb)
        o_ref[...] = o
        mu = jnp.mean(o, axis=-1, keepdims=True)
        oc = o - mu
        var = jnp.mean(oc * oc, axis=-1, keepdims=True)
        y = oc * lax.rsqrt(var + LN_EPS) * gn_ref[...]
        gr = gr_ref[...]
        z_ref[...] = (gr * _sigmoid(gr) * y).astype(BF16)

    qkv, rope, tabs = _retention_specs(t, nt, lambda i: i)
    in_specs = qkv + rope + tabs + [pl.BlockSpec((t, dv), lambda h, i: (i, 8 + h)),
                                    pl.BlockSpec((1, dv), lambda h, i: (0, h))]
    out_specs = [pl.BlockSpec((t, dv), lambda h, i: (i, h)),
                 pl.BlockSpec((t, dv), lambda h, i: (i, h)),
                 pl.BlockSpec((None, None, dk, dv), lambda h, i: (h, i, 0, 0))]
    out_shape = [_sds((s, RET_HEADS * dv), F32), _sds((s, RET_HEADS * dv), BF16), _sds((RET_HEADS, nt, dk, dv), BF16)]
    return pl.pallas_call(
        body, grid=(RET_HEADS, nt), in_specs=in_specs, out_specs=out_specs, out_shape=out_shape,
        scratch_shapes=[pltpu.VMEM((dk, dv), F32)], name="retention_fwd",
        compiler_params=_params(("parallel", "arbitrary")))(p, p, p, cos, sin, *tables, p, gn_g)


def gn_gate_bwd(o, p, gn_g, dz):
    s = o.shape[0]
    t = min(ROW_TILE, s)
    dv = RET_V_DIM
    w = RET_HEADS * dv

    def body(o_ref, gr_ref, gn_ref, dz_ref, do_ref, dgr_ref, dgn_ref):
        dgn_parts = []
        for h in range(RET_HEADS):
            sl = slice(h * dv, (h + 1) * dv)
            oo = o_ref[:, sl]
            gr = gr_ref[:, sl]
            dz = dz_ref[:, sl]
            gn = gn_ref[:, sl]
            mu = jnp.mean(oo, axis=-1, keepdims=True)
            oc = oo - mu
            rstd = lax.rsqrt(jnp.mean(oc * oc, axis=-1, keepdims=True) + LN_EPS)
            y = oc * rstd
            sg = _sigmoid(gr)
            act = gr * sg
            dyg = dz * act
            dgn_parts.append(jnp.sum(dyg * y, axis=0, keepdims=True))
            dy = dyg * gn
            do_ref[:, sl] = rstd * (dy - jnp.mean(dy, axis=-1, keepdims=True)
                                    - y * jnp.mean(dy * y, axis=-1, keepdims=True))
            dgr_ref[:, sl] = (dz * (y * gn) * (sg * (1.0 + gr * (1.0 - sg)))).astype(BF16)
        _acc_rows(dgn_ref, pl.program_id(0), jnp.concatenate(dgn_parts, axis=-1))

    return pl.pallas_call(
        body, grid=(s // t,), in_specs=[_row_spec(t, w), _row_spec(t, w, 2), _vec_spec(w), _row_spec(t, w)],
        out_specs=[_row_spec(t, w), _row_spec(t, w), _vec_spec(w)],
        out_shape=[_sds((s, w), F32), _sds((s, w), BF16), _sds((1, w), F32)], name="gn_gate_bwd",
        compiler_params=_params(("arbitrary",)))(o, p, gn_g, dz)


def retention_bwd(p, cos, sin, states, do, tables):
    s = p.shape[0]
    t = min(RET_TILE, s)
    nt = s // t
    dk, dv = RET_QK_DIM, RET_V_DIM

    def body(q_ref, k_ref, v_ref, cos_ref, sin_ref, di_ref, dq_ref, dkk_ref, dc_ref, st_ref, do_ref,
             gq_ref, gk_ref, gv_ref, dstate):
        @pl.when(pl.program_id(1) == 0)
        def _():
            dstate[...] = jnp.zeros_like(dstate)

        cs, sn = cos_ref[...], sin_ref[...]
        qb = (_rot(q_ref[...], cs, sn) * _QK_SCALE).astype(BF16)
        kr = _rot(k_ref[...], cs, sn)
        kb = kr.astype(BF16)
        vb = v_ref[...].astype(BF16)
        dmat, dkk = di_ref[...], dkk_ref[...]
        d_o = do_ref[...]
        dob = d_o.astype(BF16)
        dsb = dstate[...].astype(BF16)
        ab = (_nt(qb, kb) * dmat).astype(BF16)
        gv_ref[...] = (_tn(ab, dob) + _nn((kr * dkk).astype(BF16), dsb)).astype(BF16)
        dcb = (d_o * dq_ref[...]).astype(BF16)
        dpb = (_nt(dob, vb) * dmat).astype(BF16)
        dqq = _nt(dcb, st_ref[...]) + _nn(dpb, kb)
        dkv = _tn(dpb, qb) + _nt(vb, dsb) * dkk
        dstate[...] = dstate[...] * dc_ref[...] + _tn(qb, dcb)
        gq_ref[...] = _rot_t(dqq * _QK_SCALE, cs, sn).astype(BF16)
        gk_ref[...] = _rot_t(dkv, cs, sn).astype(BF16)

    rev = lambda i: nt - 1 - i
    qkv, rope, tabs = _retention_specs(t, nt, rev)
    in_specs = qkv + rope + tabs + [pl.BlockSpec((None, None, dk, dv), lambda h, i: (h, rev(i), 0, 0)),
                                    pl.BlockSpec((t, dv), lambda h, i: (rev(i), h))]
    out_specs = [pl.BlockSpec((t, dk), lambda h, i: (rev(i), h)),
                 pl.BlockSpec((t, dk), lambda h, i: (rev(i), h)),
                 pl.BlockSpec((t, dv), lambda h, i: (rev(i), h))]
    out_shape = [_sds((s, RET_HEADS * dk), BF16), _sds((s, RET_HEADS * dk), BF16), _sds((s, RET_HEADS * dv), BF16)]
    return pl.pallas_call(
        body, grid=(RET_HEADS, nt), in_specs=in_specs, out_specs=out_specs, out_shape=out_shape,
        scratch_shapes=[pltpu.VMEM((dk, dv), F32)], name="retention_bwd",
        compiler_params=_params(("parallel", "arbitrary")))(p, p, p, cos, sin, *tables, states, do)


A_COL, B_COL = 6, 7


def _prev_rows_spec(t, halo, width, col):
    per = t // halo
    return pl.BlockSpec((halo, width), lambda i: (jnp.maximum(i * per - 1, 0), col))


def _next_rows_spec(t, halo, width, col, n_rows):
    per = t // halo
    last = n_rows // halo - 1
    return pl.BlockSpec((halo, width), lambda i: (jnp.minimum((i + 1) * per, last), col))


def _shifted_copies(ext, rows):
    for b in range(1, 8):
        ext[b, pl.ds(0, rows - 8), :] = ext[0, pl.ds(b, rows - 8), :]


def _shifted(ext, start, lanes):
    return ext[start % 8, pl.ds(start - start % 8, STRIP_ROWS), lanes]


def conv_fwd(p, dw_w, dw_b, ln_g, ln_b):
    s = p.shape[0]
    t = min(ROW_TILE, s)
    c = D_MODEL
    hl = CONV_HALO

    def body(a_ref, b_ref, ah_ref, bh_ref, w_ref, wb_ref, g_ref, bb_ref, c1_ref, c3_ref, ext):
        i = pl.program_id(0)
        ext[0, pl.ds(0, hl), :] = jnp.where(i > 0, ah_ref[...] * _sigmoid(bh_ref[...]), 0.0)
        ext[0, pl.ds(hl, t), :] = a_ref[...] * _sigmoid(b_ref[...])
        _shifted_copies(ext, t + hl)
        first = hl - (CONV_WIDTH - 1)
        for lane in range(0, c, STRIP_LANES):
            ls = slice(lane, lane + STRIP_LANES)
            for r0 in range(0, t, STRIP_ROWS):
                accs = [jnp.broadcast_to(wb_ref[:, ls], (STRIP_ROWS, STRIP_LANES)),
                        jnp.zeros((STRIP_ROWS, STRIP_LANES), F32)]
                for j in range(CONV_WIDTH):
                    accs[j % 2] = accs[j % 2] + w_ref[j:j + 1, ls] * _shifted(ext, r0 + first + j, ls)
                c1_ref[r0:r0 + STRIP_ROWS, ls] = accs[0] + accs[1]
        acc = c1_ref[...]
        mu = jnp.mean(acc, axis=-1, keepdims=True)
        xc = acc - mu
        var = jnp.mean(xc * xc, axis=-1, keepdims=True)
        c2 = xc * lax.rsqrt(var + LN_EPS) * g_ref[...] + bb_ref[...]
        c3_ref[...] = (c2 * _sigmoid(c2)).astype(BF16)

    in_specs = [_row_spec(t, c, A_COL), _row_spec(t, c, B_COL),
                _prev_rows_spec(t, hl, c, A_COL), _prev_rows_spec(t, hl, c, B_COL),
                pl.BlockSpec((CONV_WIDTH, c), lambda i: (0, 0)), _vec_spec(c), _vec_spec(c), _vec_spec(c)]
    return pl.pallas_call(
        body, grid=(s // t,), in_specs=in_specs, out_specs=[_row_spec(t, c), _row_spec(t, c)],
        out_shape=[_sds((s, c), F32), _sds((s, c), BF16)], scratch_shapes=[pltpu.VMEM((8, t + hl, c), F32)],
        name="conv_fwd", compiler_params=_params(("parallel",)))(p, p, p, p, dw_w, dw_b, ln_g, ln_b)


def conv_ln_bwd(c1, ln_g, ln_b, dc3):
    s, c = c1.shape
    t = min(ROW_TILE, s)

    def body(c1_ref, g_ref, b_ref, d_ref, dc1_ref, dg_ref, db_ref):
        x = c1_ref[...]
        g = g_ref[...]
        mu = jnp.mean(x, axis=-1, keepdims=True)
        xc = x - mu
        rstd = lax.rsqrt(jnp.mean(xc * xc, axis=-1, keepdims=True) + LN_EPS)
        y = xc * rstd
        c2 = y * g + b_ref[...]
        sg = _sigmoid(c2)
        dc2 = d_ref[...] * (sg * (1.0 + c2 * (1.0 - sg)))
        i = pl.program_id(0)
        _acc_rows(db_ref, i, jnp.sum(dc2, axis=0, keepdims=True))
        _acc_rows(dg_ref, i, jnp.sum(dc2 * y, axis=0, keepdims=True))
        dy = dc2 * g
        dc1_ref[...] = rstd * (dy - jnp.mean(dy, axis=-1, keepdims=True)
                               - y * jnp.mean(dy * y, axis=-1, keepdims=True))

    return pl.pallas_call(
        body, grid=(s // t,), in_specs=[_row_spec(t, c), _vec_spec(c), _vec_spec(c), _row_spec(t, c)],
        out_specs=[_row_spec(t, c), _vec_spec(c), _vec_spec(c)],
        out_shape=[_sds((s, c), F32), _sds((1, c), F32), _sds((1, c), F32)], name="conv_ln_bwd",
        compiler_params=_params(("arbitrary",)))(c1, ln_g, ln_b, dc3)


def conv_dw_bwd(p, dc1, dw_w):
    s = p.shape[0]
    t = min(ROW_TILE, s)
    c = D_MODEL
    hl = CONV_HALO
    nt = s // t

    def body(a_ref, b_ref, ah_ref, bh_ref, d_ref, dn_ref, w_ref, dab_ref, dw_ref, dbias_ref, ext_c, ext_d, dc0_s,
             dw_s):
        i = pl.program_id(0)
        ext_c[0, pl.ds(0, hl), :] = jnp.where(i > 0, ah_ref[...] * _sigmoid(bh_ref[...]), 0.0)
        ext_c[0, pl.ds(hl, t), :] = a_ref[...] * _sigmoid(b_ref[...])
        ext_d[0, pl.ds(0, t), :] = d_ref[...]
        ext_d[0, pl.ds(t, hl), :] = jnp.where(i < nt - 1, dn_ref[...], 0.0)
        _shifted_copies(ext_c, t + hl)
        _shifted_copies(ext_d, t + hl)
        first = hl - (CONV_WIDTH - 1)

        def fold8(x):
            rows = [x[k:k + 8] for k in range(0, STRIP_ROWS, 8)]
            while len(rows) > 1:
                rows = [rows[k] + rows[k + 1] for k in range(0, len(rows), 2)]
            return rows[0]

        for lane in range(0, c, STRIP_LANES):
            ls = slice(lane, lane + STRIP_LANES)
            for r0 in range(0, t, STRIP_ROWS):
                accs = [jnp.zeros((STRIP_ROWS, STRIP_LANES), F32) for _ in range(2)]
                for j in range(CONV_WIDTH):
                    accs[j % 2] = accs[j % 2] + w_ref[j:j + 1, ls] * _shifted(ext_d, r0 + CONV_WIDTH - 1 - j, ls)
                dc0_s[r0:r0 + STRIP_ROWS, ls] = accs[0] + accs[1]
            for j0 in range(0, CONV_WIDTH, DW_TAPS):
                taps = range(j0, min(j0 + DW_TAPS, CONV_WIDTH))
                parts = [jnp.zeros((8, STRIP_LANES), F32) for _ in taps]
                for r0 in range(0, t, STRIP_ROWS):
                    d = ext_d[0, r0:r0 + STRIP_ROWS, ls]
                    for k, j in enumerate(taps):
                        parts[k] = parts[k] + fold8(d * _shifted(ext_c, r0 + first + j, ls))
                for k, j in enumerate(taps):
                    dw_s[j:j + 1, ls] = jnp.sum(parts[k], axis=0, keepdims=True)
        _acc_rows(dw_ref, i, dw_s[0:CONV_WIDTH, :])
        d = d_ref[...]
        _acc_rows(dbias_ref, i, jnp.sum(d, axis=0, keepdims=True))
        dc0 = dc0_s[...]
        a = a_ref[...]
        sb = _sigmoid(b_ref[...])
        dab_ref[:, :c] = (dc0 * sb).astype(BF16)
        dab_ref[:, c:] = (dc0 * a * sb * (1.0 - sb)).astype(BF16)

    in_specs = [_row_spec(t, c, A_COL), _row_spec(t, c, B_COL),
                _prev_rows_spec(t, hl, c, A_COL), _prev_rows_spec(t, hl, c, B_COL),
                _row_spec(t, c), _next_rows_spec(t, hl, c, 0, s),
                pl.BlockSpec((CONV_WIDTH, c), lambda i: (0, 0))]
    return pl.pallas_call(
        body, grid=(nt,), in_specs=in_specs,
        out_specs=[_row_spec(t, 2 * c), pl.BlockSpec((CONV_WIDTH, c), lambda i: (0, 0)), _vec_spec(c)],
        out_shape=[_sds((s, 2 * c), BF16), _sds((CONV_WIDTH, c), F32), _sds((1, c), F32)],
        scratch_shapes=[pltpu.VMEM((8, t + hl, c), F32), pltpu.VMEM((8, t + hl, c), F32), pltpu.VMEM((t, c), F32),
                        pltpu.VMEM((CONV_HALO, c), F32)], name="conv_dw_bwd",
        compiler_params=_params(("arbitrary",)))(p, p, p, p, dc1, dc1, dw_w)


GATE_COL = 4


def gate_mix_fwd(p, b_gate, y_a, y_b, b_conv_out):
    s = p.shape[0]
    t = min(ROW_TILE, s)
    c = D_MODEL

    def body(gt_ref, bg_ref, ya_ref, yb_ref, bc_ref, o_ref):
        gs = _sigmoid(gt_ref[...] + bg_ref[...])
        o_ref[...] = (gs[:, :c] * ya_ref[...] + gs[:, c:] * (yb_ref[...] + bc_ref[...])).astype(BF16)

    return pl.pallas_call(
        body, grid=(s // t,),
        in_specs=[_row_spec(t, 2 * c, GATE_COL), _vec_spec(2 * c), _row_spec(t, c), _row_spec(t, c), _vec_spec(c)],
        out_specs=_row_spec(t, c), out_shape=_sds((s, c), BF16), name="gate_mix_fwd",
        compiler_params=_params(("parallel",)))(p, b_gate, y_a, y_b, b_conv_out)


def gate_mix_bwd(p, b_gate, y_a, y_b, b_conv_out, dmix):
    s = p.shape[0]
    t = min(ROW_TILE, s)
    c = D_MODEL

    def body(gt_ref, bg_ref, ya_ref, yb_ref, bc_ref, d_ref, dya_ref, dyb_ref, dgt_ref, dbg_ref, dbc_ref):
        gs = _sigmoid(gt_ref[...] + bg_ref[...])
        ga, gb = gs[:, :c], gs[:, c:]
        d = d_ref[...]
        dya = ga * d
        dyb = gb * d
        dya_ref[...] = dya.astype(BF16)
        dyb_ref[...] = dyb.astype(BF16)
        dga = d * ya_ref[...] * ga * (1.0 - ga)
        dgb = d * (yb_ref[...] + bc_ref[...]) * gb * (1.0 - gb)
        dgt_ref[:, :c] = dga.astype(BF16)
        dgt_ref[:, c:] = dgb.astype(BF16)
        i = pl.program_id(0)
        _acc_rows(dbg_ref, i, jnp.concatenate([jnp.sum(dga, axis=0, keepdims=True),
                                               jnp.sum(dgb, axis=0, keepdims=True)], axis=-1))
        _acc_rows(dbc_ref, i, jnp.sum(dyb, axis=0, keepdims=True))

    return pl.pallas_call(
        body, grid=(s // t,),
        in_specs=[_row_spec(t, 2 * c, GATE_COL), _vec_spec(2 * c), _row_spec(t, c), _row_spec(t, c), _vec_spec(c),
                  _row_spec(t, c)],
        out_specs=[_row_spec(t, c), _row_spec(t, c), _row_spec(t, 2 * c), _vec_spec(2 * c), _vec_spec(c)],
        out_shape=[_sds((s, c), BF16), _sds((s, c), BF16), _sds((s, 2 * c), BF16), _sds((1, 2 * c), F32),
                   _sds((1, c), F32)],
        name="gate_mix_bwd", compiler_params=_params(("arbitrary",)))(p, b_gate, y_a, y_b, b_conv_out, dmix)


_X_SCALE = X_HEAD_DIM ** -0.5


def _softmax_rows(sc):
    m = jnp.max(sc, axis=-1, keepdims=True)
    e = jnp.exp(sc - m)
    return e / jnp.sum(e, axis=-1, keepdims=True)


def attn_fwd(qx, kv):
    s, d = qx.shape
    m = kv.shape[0]
    t = min(ATTN_TILE, s)
    hd = X_HEAD_DIM

    def body(q_ref, kv_ref, o_ref):
        for h in range(X_HEADS):
            sl = slice(h * hd, (h + 1) * hd)
            kh = kv_ref[:, sl].astype(BF16)
            vh = kv_ref[:, d + h * hd:d + (h + 1) * hd].astype(BF16)
            pr = _softmax_rows(_nt(q_ref[:, sl], kh) * _X_SCALE)
            o_ref[:, sl] = _nn(pr.astype(BF16), vh).astype(BF16)

    return pl.pallas_call(
        body, grid=(s // t,), in_specs=[_row_spec(t, d), pl.BlockSpec((m, 2 * d), lambda i: (0, 0))],
        out_specs=_row_spec(t, d), out_shape=_sds((s, d), BF16), name="attn_fwd",
        compiler_params=_params(("parallel",)))(qx, kv)


def attn_bwd(qx, kv, dox):
    s, d = qx.shape
    m = kv.shape[0]
    t = min(ATTN_TILE, s)
    hd = X_HEAD_DIM

    def body(q_ref, kv_ref, do_ref, dq_ref, dkv_ref):
        dks, dvs = [], []
        for h in range(X_HEADS):
            sl = slice(h * hd, (h + 1) * hd)
            qh = q_ref[:, sl]
            kh = kv_ref[:, sl].astype(BF16)
            vh = kv_ref[:, d + h * hd:d + (h + 1) * hd].astype(BF16)
            pr = _softmax_rows(_nt(qh, kh) * _X_SCALE)
            doh = do_ref[:, sl].astype(BF16)
            dpr = _nt(doh, vh)
            dvs.append(_tn(pr.astype(BF16), doh))
            ds = pr * (dpr - jnp.sum(dpr * pr, axis=-1, keepdims=True))
            dsb = (ds * _X_SCALE).astype(BF16)
            dq_ref[:, sl] = _nn(dsb, kh).astype(BF16)
            dks.append(_tn(dsb, qh))
        _acc_rows(dkv_ref, pl.program_id(0), jnp.concatenate(dks + dvs, axis=-1))

    return pl.pallas_call(
        body, grid=(s // t,),
        in_specs=[_row_spec(t, d), pl.BlockSpec((m, 2 * d), lambda i: (0, 0)), _row_spec(t, d)],
        out_specs=[_row_spec(t, d), pl.BlockSpec((m, 2 * d), lambda i: (0, 0))],
        out_shape=[_sds((s, d), BF16), _sds((m, 2 * d), F32)], name="attn_bwd",
        compiler_params=_params(("arbitrary",)))(qx, kv, dox)


def _offset_copies(ext, offsets, rows):
    for k, off in enumerate(offsets):
        ext[1 + k, pl.ds(0, rows), :] = ext[0, pl.ds(off, rows), :]


def _ffn_blocks(rows, lanes):
    return [(r0, slice(l0, l0 + FFN_LANES)) for r0 in range(0, rows, FFN_ROWS) for l0 in range(0, lanes, FFN_LANES)]


def ffn_act_fwd(up, dw_w, dw_b):
    s = up.shape[0]
    f = FFN_DIM
    t = min(ROW_TILE, s)
    hl = FFN_HALO

    def body(val_ref, gt_ref, gh_ref, w_ref, b_ref, o_ref, ext):
        i = pl.program_id(0)
        ext[0, pl.ds(0, hl), :] = jnp.where(i > 0, gh_ref[...], 0.0)
        ext[0, pl.ds(hl, t), :] = gt_ref[...]
        _offset_copies(ext, (hl - 2, hl - 1), t)
        for r0, ls in _ffn_blocks(t, f):
            gc = b_ref[:, ls] + w_ref[0:1, ls] * ext[1, pl.ds(r0, FFN_ROWS), ls] \
                + w_ref[1:2, ls] * ext[2, pl.ds(r0, FFN_ROWS), ls] + w_ref[2:3, ls] * ext[0, pl.ds(r0 + hl, FFN_ROWS), ls]
            o_ref[r0:r0 + FFN_ROWS, ls] = (gc * _sigmoid(gc) * val_ref[r0:r0 + FFN_ROWS, ls]).astype(BF16)

    return pl.pallas_call(
        body, grid=(s // t,),
        in_specs=[_row_spec(t, f, 0), _row_spec(t, f, 1), _prev_rows_spec(t, hl, f, 1),
                  pl.BlockSpec((3, f), lambda i: (0, 0)), _vec_spec(f)],
        out_specs=_row_spec(t, f), out_shape=_sds((s, f), BF16), scratch_shapes=[pltpu.VMEM((3, t + hl, f), F32)],
        name="ffn_act_fwd", compiler_params=_params(("parallel",)))(up, up, up, dw_w, dw_b)


def ffn_act_bwd(up, dw_w, dw_b, da):
    s = up.shape[0]
    f = FFN_DIM
    t = min(ROW_TILE, s)
    hl = FFN_HALO
    nt = s // t

    def body(val_ref, valn_ref, gt_ref, gp_ref, gn_ref, da_ref, dan_ref, w_ref, b_ref,
             dup_ref, dw_ref, db_ref, ext_g, ext_d, sums):
        i = pl.program_id(0)
        ext_g[0, pl.ds(0, hl), :] = jnp.where(i > 0, gp_ref[...], 0.0)
        ext_g[0, pl.ds(hl, t), :] = gt_ref[...]
        ext_g[0, pl.ds(hl + t, hl), :] = gn_ref[...]
        _offset_copies(ext_g, (hl - 2, hl - 1), t + hl)
        sums[...] = jnp.zeros_like(sums)

        def fold8(x):
            out = x[0:8]
            for k in range(8, x.shape[0], 8):
                out = out + x[k:k + 8]
            return out

        def gate_block(rows, off, ls, val, da_rows):
            taps = [ext_g[1, pl.ds(off, rows), ls], ext_g[2, pl.ds(off, rows), ls], ext_g[0, pl.ds(off + hl, rows), ls]]
            gc = b_ref[:, ls] + w_ref[0:1, ls] * taps[0] + w_ref[1:2, ls] * taps[1] + w_ref[2:3, ls] * taps[2]
            sg = _sigmoid(gc)
            return taps, gc * sg, da_rows * val * (sg * (1.0 + gc * (1.0 - sg)))

        for r0, ls in _ffn_blocks(t, f):
            rs = slice(r0, r0 + FFN_ROWS)
            da_rows = da_ref[rs, ls]
            taps, act, dgc = gate_block(FFN_ROWS, r0, ls, val_ref[rs, ls], da_rows)
            ext_d[0, rs, ls] = dgc
            dup_ref[rs, ls] = (da_rows * act).astype(BF16)
            for k in range(3):
                sums[8 * k:8 * k + 8, ls] += fold8(dgc * taps[k])
            sums[24:32, ls] += fold8(dgc)
        _, _, dgc_next = gate_block(hl, t, slice(None), valn_ref[...], dan_ref[...])
        ext_d[0, pl.ds(t, hl), :] = jnp.where(i < nt - 1, dgc_next, 0.0)
        _offset_copies(ext_d, (1, 2), t)
        for r0, ls in _ffn_blocks(t, f):
            gate_lanes = slice(f + ls.start, f + ls.stop)
            rs = pl.ds(r0, FFN_ROWS)
            dup_ref[r0:r0 + FFN_ROWS, gate_lanes] = (
                w_ref[2:3, ls] * ext_d[0, rs, ls] + w_ref[1:2, ls] * ext_d[1, rs, ls]
                + w_ref[0:1, ls] * ext_d[2, rs, ls]).astype(BF16)
        rows = [jnp.sum(sums[8 * k:8 * k + 8, :], axis=0, keepdims=True) for k in range(4)]
        _acc_rows(dw_ref, i, jnp.concatenate(rows[:3], axis=0))
        _acc_rows(db_ref, i, rows[3])

    in_specs = [_row_spec(t, f, 0), _next_rows_spec(t, hl, f, 0, s),
                _row_spec(t, f, 1), _prev_rows_spec(t, hl, f, 1), _next_rows_spec(t, hl, f, 1, s),
                _row_spec(t, f), _next_rows_spec(t, hl, f, 0, s),
                pl.BlockSpec((3, f), lambda i: (0, 0)), _vec_spec(f)]
    return pl.pallas_call(
        body, grid=(nt,), in_specs=in_specs,
        out_specs=[_row_spec(t, 2 * f), pl.BlockSpec((3, f), lambda i: (0, 0)), _vec_spec(f)],
        out_shape=[_sds((s, 2 * f), BF16), _sds((3, f), F32), _sds((1, f), F32)],
        scratch_shapes=[pltpu.VMEM((3, t + 2 * hl, f), F32), pltpu.VMEM((3, t + hl, f), F32),
                        pltpu.VMEM((32, f), F32)],
        name="ffn_act_bwd",
        compiler_params=_params(("arbitrary",)))(up, up, up, up, up, da, da, dw_w, dw_b)


def local_step(x, mem, positions, target, gw, sp):
    n_layers = gw["w_in"].shape[0]
    inv_freq = 1.0 / (ROPE_THETA ** (jnp.arange(0, RET_QK_DIM, 2, dtype=F32) / RET_QK_DIM))
    cos, sin = rope_tables(positions, inv_freq[None, :])
    tables = _decay_tables(min(RET_TILE, x.shape[0]))
    row = lambda name, l: sp[name][l][None, :]

    saved = []
    h = x
    u = rms_cast(x, row("norm_mix_g", 0))
    for l in range(n_layers):
        a = {"h0": h}
        a["u"] = u
        a["p"] = mm_fwd("mm_in", a["u"], gw["w_in"], l, True)
        a["o"], a["z"], a["states"] = retention_fwd(a["p"], cos, sin, row("ret_gn_g", l), tables)
        a["y_a"] = mm_fwd("mm_ret_out", a["z"], gw["w_ret_out"], l, False)
        a["c1"], a["c3"] = conv_fwd(a["p"], sp["conv_dw_w"][l], row("conv_dw_b", l), row("conv_ln_g", l),
                                    row("conv_ln_b", l))
        a["y_b"] = mm_fwd("mm_conv_out", a["c3"], gw["w_conv_out"], l, False)
        a["mixed"] = gate_mix_fwd(a["p"], row("b_gate", l), a["y_a"], a["y_b"], row("b_conv_out", l))
        a["h1"], a["hx"] = mm_fwd("mm_mix_out", a["mixed"], gw["w_mix_out"], l, False, res=h,
                                  norm_g=row("norm_xattn_g", l))
        a["qx"] = mm_fwd("mm_xq", a["hx"], gw["w_xq"], l, False, out_dtype=BF16)
        a["mem_n"] = rms_cast(mem, row("norm_mem_g", l))
        a["kv"] = mm_fwd("mm_xkv", a["mem_n"], gw["w_xkv"], l, True)
        a["ox"] = attn_fwd(a["qx"], a["kv"])
        a["h2"], a["hf"] = mm_fwd("mm_xo", a["ox"], gw["w_xo"], l, False, res=a["h1"], norm_g=row("norm_ffn_g", l))
        a["up"] = mm_fwd("mm_up", a["hf"], gw["w_up"], l, True)
        a["act"] = ffn_act_fwd(a["up"], sp["ffn_dw_w"][l], row("ffn_dw_b", l))
        if l + 1 < n_layers:
            h, u = mm_fwd("mm_down", a["act"], gw["w_down"], l, False, res=a["h2"], norm_g=row("norm_mix_g", l + 1))
        else:
            h = mm_fwd("mm_down", a["act"], gw["w_down"], l, False, res=a["h2"])
        saved.append(a)

    dh, d_final_g, loss = loss_head(h, sp["norm_final_g"][None, :], target)

    big = {}

    def dw(name, key, act, dy, col, l):
        big[key] = mm_dw(name, act, dy, col, l, n_layers, big.get(key))

    small = {n: [None] * n_layers for n in SMALL_REPL + SMALL_SHARDED if n != "norm_final_g"}
    for l in range(n_layers - 1, -1, -1):
        a = saved[l]
        d_act = mm_dx("mm_down_dx", dh, gw["w_down"], l, False)
        dw("mm_down_dw", "w_down", a["act"], dh, False, l)
        d_up, small["ffn_dw_w"][l], small["ffn_dw_b"][l] = ffn_act_bwd(a["up"], sp["ffn_dw_w"][l],
                                                                        row("ffn_dw_b", l), d_act)
        dh, small["norm_ffn_g"][l] = mm_dx("mm_up_dx", d_up, gw["w_up"], l, True,
                                           rms=(a["h2"], row("norm_ffn_g", l), dh))
        dw("mm_up_dw", "w_up", a["hf"], d_up, True, l)
        d_ox = mm_dx("mm_xo_dx", dh, gw["w_xo"], l, False)
        dw("mm_xo_dw", "w_xo", a["ox"], dh, False, l)
        d_qx, d_kv = attn_bwd(a["qx"], a["kv"], d_ox)
        dw("mm_xq_dw", "w_xq", a["hx"], d_qx, False, l)
        d_mem_n = mm_dx("mm_xkv_dx", d_kv, gw["w_xkv"], l, True)
        dw("mm_xkv_dw", "w_xkv", a["mem_n"], d_kv, True, l)
        _, small["norm_mem_g"][l] = rms_bwd(mem, row("norm_mem_g", l), d_mem_n)
        dh, small["norm_xattn_g"][l] = mm_dx("mm_xq_dx", d_qx, gw["w_xq"], l, False,
                                             rms=(a["h1"], row("norm_xattn_g", l), dh))
        d_mixed = mm_dx("mm_mix_out_dx", dh, gw["w_mix_out"], l, False)
        dw("mm_mix_out_dw", "w_mix_out", a["mixed"], dh, False, l)
        d_ya, d_yb, dp_gate, small["b_gate"][l], small["b_conv_out"][l] = gate_mix_bwd(
            a["p"], row("b_gate", l), a["y_a"], a["y_b"], row("b_conv_out", l), d_mixed)
        d_c3 = mm_dx("mm_conv_out_dx", d_yb, gw["w_conv_out"], l, False)
        dw("mm_conv_out_dw", "w_conv_out", a["c3"], d_yb, False, l)
        d_c1, small["conv_ln_g"][l], small["conv_ln_b"][l] = conv_ln_bwd(a["c1"], row("conv_ln_g", l),
                                                                         row("conv_ln_b", l), d_c3)
        dp_conv, small["conv_dw_w"][l], small["conv_dw_b"][l] = conv_dw_bwd(a["p"], d_c1, sp["conv_dw_w"][l])
        d_z = mm_dx("mm_ret_out_dx", d_ya, gw["w_ret_out"], l, False)
        dw("mm_ret_out_dw", "w_ret_out", a["z"], d_ya, False, l)
        d_o, dp_gret, small["ret_gn_g"][l] = gn_gate_bwd(a["o"], a["p"], row("ret_gn_g", l), d_z)
        dp_q, dp_k, dp_v = retention_bwd(a["p"], cos, sin, a["states"], d_o, tables)
        dp = jnp.concatenate([dp_q, dp_k, dp_v, dp_gret, dp_conv, dp_gate], axis=1)
        dw("mm_in_dw", "w_in", a["u"], dp, True, l)
        dh, small["norm_mix_g"][l] = mm_dx("mm_in_dx", dp, gw["w_in"], l, True,
                                           rms=(a["h0"], row("norm_mix_g", l), dh))

    small = {n: jnp.stack([g.reshape(sp[n].shape[1:]) for g in v]) for n, v in small.items()}
    small["norm_final_g"] = d_final_g.reshape(-1)
    return loss, dh, big, small


_ANY = pl.BlockSpec(memory_space=pl.ANY)


def _place():
    x, y, c = lax.axis_index("x"), lax.axis_index("y"), lax.axis_index("c")
    return x, y, c


def _other_chips(x, y):
    return [(1 - x, y), (x, 1 - y), (1 - x, 1 - y)]


def place_shard(w, chip):
    lyr, a, b = w.shape
    t = _flat_tile(a, b)

    def body(chip_ref, w_ref, o_ref):
        o_ref[...] = w_ref[...].astype(BF16)

    grid_spec = pltpu.PrefetchScalarGridSpec(
        num_scalar_prefetch=1, grid=(lyr, a // t),
        in_specs=[pl.BlockSpec((None, t, b), lambda l, i, cr: (l, i, 0))],
        out_specs=pl.BlockSpec((None, None, t, b), lambda l, i, cr: (l, cr[0], i, 0)))
    return pl.pallas_call(body, grid_spec=grid_spec, out_shape=_sds((lyr, N_CHIPS, a, b), BF16), name="place_shard",
                          compiler_params=_params(("parallel", "parallel")))(chip, w)


def gather_weights(bufs):
    n = len(bufs)

    def body(*refs):
        outs = refs[n:2 * n]
        ici_send, ici_recv, pair_send, pair_recv = refs[2 * n:]
        x, y, c = _place()
        mine = 2 * x + y
        chips = _other_chips(x, y)

        def part(w, core, slot):
            lh = outs[w].shape[0] // 2
            return outs[w].at[pl.ds(core * lh, lh), slot]

        def over_ici(w, j, slot):
            px, py = chips[j]
            return pltpu.make_async_remote_copy(
                src_ref=part(w, c, slot), dst_ref=part(w, c, slot), send_sem=ici_send.at[w, j],
                recv_sem=ici_recv.at[w, j], device_id=(px, py, c), device_id_type=MESH)

        def to_pair(w, j, core):
            px, py = chips[j]
            return pltpu.make_async_remote_copy(
                src_ref=part(w, core, 2 * px + py), dst_ref=part(w, core, 2 * px + py), send_sem=pair_send.at[w, j],
                recv_sem=pair_recv.at[w, j], device_id=(x, y, 1 - c), device_id_type=MESH)

        for w in range(n):
            for j in range(3):
                over_ici(w, j, mine).start()
        for w in range(n):
            for j, (px, py) in enumerate(chips):
                over_ici(w, j, 2 * px + py).wait_recv()
                to_pair(w, j, c).start()
        for w in range(n):
            for j in range(3):
                to_pair(w, j, 1 - c).wait_recv()
                to_pair(w, j, c).wait_send()
                over_ici(w, j, mine).wait_send()

    return pl.pallas_call(
        body, in_specs=[_ANY] * n, out_specs=[_ANY] * n, out_shape=[_sds(b.shape, b.dtype) for b in bufs],
        input_output_aliases={i: i for i in range(n)},
        scratch_shapes=[pltpu.SemaphoreType.DMA((n, 3))] * 4,
        name="gather_weights", compiler_params=_params())(*bufs)


def pair_exchange(grads):
    n = len(grads)

    def body(*refs):
        ins, outs = refs[:n], refs[n:2 * n]
        send_sems, recv_sems = refs[2 * n:]
        x, y, c = _place()
        cps = []
        for w in range(n):
            lh = ins[w].shape[0] // 2
            cp = pltpu.make_async_remote_copy(
                src_ref=ins[w].at[pl.ds((1 - c) * lh, lh)], dst_ref=outs[w], send_sem=send_sems.at[w],
                recv_sem=recv_sems.at[w], device_id=(x, y, 1 - c), device_id_type=MESH)
            cp.start()
            cps.append(cp)
        for cp in cps:
            cp.wait_send()
            cp.wait_recv()

    out_shape = [_sds((g.shape[0] // 2,) + g.shape[1:], g.dtype) for g in grads]
    return pl.pallas_call(
        body, in_specs=[_ANY] * n, out_specs=[_ANY] * n, out_shape=out_shape,
        scratch_shapes=[pltpu.SemaphoreType.DMA((n,)), pltpu.SemaphoreType.DMA((n,))],
        name="pair_exchange", compiler_params=_params())(*grads)


def chip_exchange(parts):
    n = len(parts)

    def body(*refs):
        ins, outs = refs[:n], refs[n:2 * n]
        send_sems, recv_sems = refs[2 * n:]
        x, y, c = _place()
        mine = 2 * x + y
        chips = _other_chips(x, y)

        def copy(w, j, slot):
            px, py = chips[j]
            return pltpu.make_async_remote_copy(
                src_ref=ins[w].at[:, 2 * px + py], dst_ref=outs[w].at[slot], send_sem=send_sems.at[w, j],
                recv_sem=recv_sems.at[w, j], device_id=(px, py, c), device_id_type=MESH)

        for w in range(n):
            for j in range(3):
                copy(w, j, mine).start()
        for w in range(n):
            for j, (px, py) in enumerate(chips):
                copy(w, j, 2 * px + py).wait_recv()
                copy(w, j, mine).wait_send()

    out_shape = [_sds((N_CHIPS, g.shape[0]) + g.shape[2:], g.dtype) for g in parts]
    return pl.pallas_call(
        body, in_specs=[_ANY] * n, out_specs=[_ANY] * n, out_shape=out_shape,
        scratch_shapes=[pltpu.SemaphoreType.DMA((n, 3)), pltpu.SemaphoreType.DMA((n, 3))],
        name="chip_exchange", compiler_params=_params())(*parts)


def pair_share(halves):
    n = len(halves)

    def body(*refs):
        ins, outs = refs[:n], refs[n:2 * n]
        send_sems, recv_sems = refs[2 * n:]
        x, y, c = _place()
        cps = []
        for w in range(n):
            cp = pltpu.make_async_remote_copy(
                src_ref=ins[w], dst_ref=outs[w], send_sem=send_sems.at[w], recv_sem=recv_sems.at[w],
                device_id=(x, y, 1 - c), device_id_type=MESH)
            cp.start()
            cps.append(cp)
        for cp in cps:
            cp.wait_send()
            cp.wait_recv()

    return pl.pallas_call(
        body, in_specs=[_ANY] * n, out_specs=[_ANY] * n, out_shape=[_sds(g.shape, g.dtype) for g in halves],
        scratch_shapes=[pltpu.SemaphoreType.DMA((n,)), pltpu.SemaphoreType.DMA((n,))],
        name="pair_share", compiler_params=_params())(*halves)


def all_reduce_small(vec):
    r, lanes = vec.shape

    def body(v_ref, o_ref, buf, send_sems, recv_sems):
        x, y, c = _place()
        me = 4 * x + 2 * y + c
        buf[me] = v_ref[...]
        cps = []
        for k in range(1, N_DEV):
            peer = (me + k) % N_DEV
            cp = pltpu.make_async_remote_copy(
                src_ref=v_ref, dst_ref=buf.at[me], send_sem=send_sems.at[k - 1], recv_sem=recv_sems.at[k - 1],
                device_id=(peer // 4, (peer // 2) % 2, peer % 2), device_id_type=MESH)
            cp.start()
            cps.append(cp)
        for k in range(1, N_DEV):
            src = (me + N_DEV - k) % N_DEV
            cps[k - 1].wait_send()
            pltpu.make_async_remote_copy(
                src_ref=v_ref, dst_ref=buf.at[src], send_sem=send_sems.at[k - 1], recv_sem=recv_sems.at[k - 1],
                device_id=(src // 4, (src // 2) % 2, src % 2), device_id_type=MESH).wait_recv()
        acc = buf[0]
        for d in range(1, N_DEV):
            acc = acc + buf[d]
        o_ref[...] = acc

    vm = pl.BlockSpec(memory_space=pltpu.VMEM)
    return pl.pallas_call(
        body, in_specs=[vm], out_specs=vm, out_shape=_sds((r, lanes), F32),
        scratch_shapes=[pltpu.VMEM((N_DEV, r, lanes), F32), pltpu.SemaphoreType.DMA((N_DEV - 1,)),
                        pltpu.SemaphoreType.DMA((N_DEV - 1,))],
        name="all_reduce_small", compiler_params=_params())(vec)


ELEMENTWISE_BLOCK_BYTES = 1 << 20


def _flat_tile(rows, cols):
    for t in (512, 256, 128, 64, 32, 16, 8):
        if rows % t == 0 and t * cols * 4 <= ELEMENTWISE_BLOCK_BYTES:
            return t
    return rows


def add_pair(g, r, half):
    lyr, _, a, b = g.shape
    lh = lyr // 2
    rows = lh * 4 * a
    t = _flat_tile(rows, b)
    nb = rows // t
    g2 = g.reshape(lyr * 4 * a, b)
    r2 = r.reshape(rows, b)

    def body(half_ref, g_ref, r_ref, o_ref):
        o_ref[...] = (g_ref[...] + r_ref[...]).astype(BF16)

    grid_spec = pltpu.PrefetchScalarGridSpec(
        num_scalar_prefetch=1, grid=(nb,),
        in_specs=[pl.BlockSpec((t, b), lambda i, hr: (hr[0] * nb + i, 0)), pl.BlockSpec((t, b), lambda i, hr: (i, 0))],
        out_specs=pl.BlockSpec((t, b), lambda i, hr: (i, 0)))
    out = pl.pallas_call(body, grid_spec=grid_spec, out_shape=_sds((rows, b), BF16), name="add_pair",
                         compiler_params=_params(("parallel",)))(half, g2, r2)
    return out.reshape(lh, 4, a, b)


def sum_chips(own, parts, chip):
    _, lh, a, b = parts.shape
    t = _flat_tile(a, b)

    def body(chip_ref, own_ref, p_ref, o_ref):
        mine = chip_ref[0]

        def term(s):
            return jnp.where(mine == s, own_ref[...], p_ref[s]).astype(F32)

        o_ref[...] = ((term(0) + term(1)) + term(2)) + term(3)

    grid_spec = pltpu.PrefetchScalarGridSpec(
        num_scalar_prefetch=1, grid=(lh, a // t),
        in_specs=[pl.BlockSpec((None, None, t, b), lambda l, i, cr: (l, cr[0], i, 0)),
                  pl.BlockSpec((N_CHIPS, None, t, b), lambda l, i, cr: (0, l, i, 0))],
        out_specs=pl.BlockSpec((None, t, b), lambda l, i, cr: (l, i, 0)))
    return pl.pallas_call(body, grid_spec=grid_spec, out_shape=_sds((lh, a, b), F32), name="sum_chips",
                          compiler_params=_params(("parallel", "parallel")))(chip, own, parts)


def _adamw_math(w, g, m, v):
    mm = ADAM_B1 * m + (1.0 - ADAM_B1) * g
    vv = ADAM_B2 * v + (1.0 - ADAM_B2) * jnp.square(g)
    m_hat = mm / (1.0 - ADAM_B1 ** ADAM_STEP)
    v_hat = vv / (1.0 - ADAM_B2 ** ADAM_STEP)
    return -ADAM_LR * (m_hat / (jnp.sqrt(v_hat) + ADAM_EPS) + ADAM_WD * w), mm, vv


def adamw(w, g, m, v):
    shape = w.shape
    c = shape[-1]
    rows = int(np.prod(shape[:-1])) if len(shape) > 1 else 1
    t = _flat_tile(rows, c)
    flat = lambda z: z.reshape(rows, c)

    def body(w_ref, g_ref, m_ref, v_ref, d_ref, nm_ref, nv_ref):
        d_ref[...], nm_ref[...], nv_ref[...] = _adamw_math(w_ref[...], g_ref[...], m_ref[...], v_ref[...])

    spec = pl.BlockSpec((t, c), lambda i: (i, 0))
    outs = pl.pallas_call(body, grid=(rows // t,), in_specs=[spec] * 4, out_specs=[spec] * 3,
                          out_shape=[_sds((rows, c), F32)] * 3, name="adamw",
                          compiler_params=_params(("parallel",)))(flat(w), flat(g), flat(m), flat(v))
    return tuple(o.reshape(shape) for o in outs)


def adamw_halves(w, g_own, g_other, m, v, core):
    lyr, a, b = w.shape
    lh = lyr // 2
    t = _flat_tile(a, b)

    def body(core_ref, w_ref, go_ref, gs_ref, m_ref, v_ref, g_ref, d_ref, nm_ref, nv_ref):
        own = pl.program_id(0) // lh == core_ref[0]
        g = jnp.where(own, go_ref[...], gs_ref[...])
        g_ref[...] = g
        d_ref[...], nm_ref[...], nv_ref[...] = _adamw_math(w_ref[...], g, m_ref[...], v_ref[...])

    full = pl.BlockSpec((None, t, b), lambda l, i, cr: (l, i, 0))
    own_spec = pl.BlockSpec((None, t, b), lambda l, i, cr: (jnp.clip(l - cr[0] * lh, 0, lh - 1), i, 0))
    other_spec = pl.BlockSpec((None, t, b), lambda l, i, cr: (jnp.clip(l - (1 - cr[0]) * lh, 0, lh - 1), i, 0))
    grid_spec = pltpu.PrefetchScalarGridSpec(
        num_scalar_prefetch=1, grid=(lyr, a // t), in_specs=[full, own_spec, other_spec, full, full],
        out_specs=[full] * 4)
    return pl.pallas_call(body, grid_spec=grid_spec, out_shape=[_sds(w.shape, F32)] * 4, name="adamw_halves",
                          compiler_params=_params(("parallel", "parallel")))(core, w, g_own, g_other, m, v)


def _pack(parts):
    flat = jnp.concatenate([p.reshape(-1) for p in parts])
    pad = (-flat.shape[0]) % 1024
    return jnp.pad(flat, (0, pad)).reshape(-1, 128)


def _unpack(packed, shapes):
    flat = packed.reshape(-1)
    out, off = [], 0
    for shp in shapes:
        size = int(np.prod(shp))
        out.append(flat[off:off + size].reshape(shp))
        off += size
    return out


def kernel(x, mem, positions, norm_mix_g, w_in, b_gate, ret_gn_g, w_ret_out, conv_dw_w, conv_dw_b, conv_ln_g, conv_ln_b, w_conv_out, b_conv_out, w_mix_out, norm_xattn_g, norm_mem_g, w_xq, w_xkv, w_xo, norm_ffn_g, w_up, ffn_dw_w, ffn_dw_b, w_down, norm_final_g, loss_target, m_norm_mix_g, m_w_in, m_b_gate, m_ret_gn_g, m_w_ret_out, m_conv_dw_w, m_conv_dw_b, m_conv_ln_g, m_conv_ln_b, m_w_conv_out, m_b_conv_out, m_w_mix_out, m_norm_xattn_g, m_norm_mem_g, m_w_xq, m_w_xkv, m_w_xo, m_norm_ffn_g, m_w_up, m_ffn_dw_w, m_ffn_dw_b, m_w_down, m_norm_final_g, v_norm_mix_g, v_w_in, v_b_gate, v_ret_gn_g, v_w_ret_out, v_conv_dw_w, v_conv_dw_b, v_conv_ln_g, v_conv_ln_b, v_w_conv_out, v_b_conv_out, v_w_mix_out, v_norm_xattn_g, v_norm_mem_g, v_w_xq, v_w_xkv, v_w_xo, v_norm_ffn_g, v_w_up, v_ffn_dw_w, v_ffn_dw_b, v_w_down, v_norm_final_g):
    args = locals()
    w = {n: args[n] for n in WEIGHTS}
    m = {n: args["m_" + n] for n in WEIGHTS}
    v = {n: args["v_" + n] for n in WEIGHTS}
    chip = 2 * lax.axis_index("x") + lax.axis_index("y")
    core = lax.axis_index("c")

    chip_op = chip.reshape(1).astype(jnp.int32)
    core_op = core.reshape(1).astype(jnp.int32)
    gathered = gather_weights([place_shard(w[n], chip_op) for n in BIG])
    gw = dict(zip(BIG, gathered))

    sp = {n: w[n] for n in SMALL_REPL}
    placed = []
    for n in SMALL_SHARDED:
        cols = w[n].shape[-1]
        full = jnp.zeros(w[n].shape[:-1] + (N_CHIPS * cols,), F32)
        placed.append(lax.dynamic_update_slice_in_dim(full, w[n], chip * cols, axis=2))
    placed_shapes = [p.shape for p in placed]
    gathered_small = all_reduce_small(_pack([jnp.where(core == 0, p, 0.0) for p in placed]))
    for n, arr in zip(SMALL_SHARDED, _unpack(gathered_small, placed_shapes)):
        sp[n] = arr

    loss, grad_x, big, small = local_step(x[0], mem[0], positions.reshape(-1, 1), loss_target[0], gw, sp)

    names = [n for n in SMALL_REPL + SMALL_SHARDED]
    shapes = [small[n].shape for n in names] + [(128,)]
    reduced = _unpack(all_reduce_small(_pack([small[n] for n in names] + [loss.reshape(-1)])), shapes)
    grads = dict(zip(names, reduced[:-1]))
    loss_out = reduced[-1][0]
    for n in SMALL_SHARDED:
        cols = w[n].shape[-1]
        grads[n] = lax.dynamic_slice_in_dim(grads[n], chip * cols, cols, axis=2)

    blist = [big[n] for n in BIG]
    from_pair = pair_exchange(blist)
    pair_sum = [add_pair(g, r, core_op) for g, r in zip(blist, from_pair)]
    from_chips = chip_exchange(pair_sum)
    halves = [sum_chips(own, parts, chip_op) for own, parts in zip(pair_sum, from_chips)]
    other_halves = pair_share(halves)

    delta, new_m, new_v = {}, {}, {}
    for n, g_own, g_other in zip(BIG, halves, other_halves):
        grads[n], delta[n], new_m[n], new_v[n] = adamw_halves(w[n], g_own, g_other, m[n], v[n], core_op)
    for n in WEIGHTS:
        if n not in BIG:
            delta[n], new_m[n], new_v[n] = adamw(w[n], grads[n], m[n], v[n])
    return (loss_out, grad_x[None], *[grads[n] for n in WEIGHTS], *[delta[n] for n in WEIGHTS],
            *[new_m[n] for n in WEIGHTS], *[new_v[n] for n in WEIGHTS])
```

```python
import functools

import jax
import jax.numpy as jnp
import numpy as np
from jax import lax
from jax.experimental import pallas as pl
from jax.experimental.pallas import tpu as pltpu

F32 = jnp.float32
BF16 = jnp.bfloat16
MESH = pl.DeviceIdType.MESH

D_MODEL = 1024
CHUNK = 64
RET_HEADS = 4
RET_QK_DIM = 256
RET_V_DIM = 512
ROPE_THETA = 10000.0
CONV_WIDTH = 31
X_HEADS = 4
X_HEAD_DIM = 256
FFN_DIM = 2816
RMS_EPS = 1e-6
LN_EPS = 1e-5
ADAM_LR = 0.001
ADAM_B1 = 0.9
ADAM_B2 = 0.999
ADAM_EPS = 1e-08
ADAM_WD = 0.01
ADAM_STEP = 10

N_CHIPS = 4
N_DEV = 8
CONV_HALO = 32
FFN_HALO = 8
V7X_VMEM_LIMIT = 56 * 1024 * 1024
ROW_TILE = 256
STRIP_ROWS = 16
STRIP_LANES = 1024
DW_TAPS = 2
FFN_ROWS = 16
FFN_LANES = 256
MM_TILE_M = 1024
RET_TILE = 512
ATTN_TILE = 512
MIX_TILE = 512

BIG = ("w_in", "w_ret_out", "w_conv_out", "w_mix_out", "w_xq", "w_xkv", "w_xo", "w_up", "w_down")
COL_SHARDED = ("w_in", "w_xkv", "w_up")
SMALL_REPL = ("norm_mix_g", "b_gate", "ret_gn_g", "conv_dw_b", "conv_ln_g", "conv_ln_b", "b_conv_out",
              "norm_xattn_g", "norm_mem_g", "norm_ffn_g", "ffn_dw_b", "norm_final_g")
SMALL_SHARDED = ("conv_dw_w", "ffn_dw_w")
WEIGHTS = ('norm_mix_g', 'w_in', 'b_gate', 'ret_gn_g', 'w_ret_out', 'conv_dw_w', 'conv_dw_b', 'conv_ln_g',
           'conv_ln_b', 'w_conv_out', 'b_conv_out', 'w_mix_out', 'norm_xattn_g', 'norm_mem_g', 'w_xq', 'w_xkv',
           'w_xo', 'norm_ffn_g', 'w_up', 'ffn_dw_w', 'ffn_dw_b', 'w_down', 'norm_final_g')


def _params(sem=None):
    return pltpu.CompilerParams(dimension_semantics=sem, vmem_limit_bytes=V7X_VMEM_LIMIT)


def _sds(shape, dtype):
    return jax.ShapeDtypeStruct(tuple(shape), dtype)


def _sigmoid(x):
    return jax.nn.sigmoid(x)


def _dot(a, b, ca, cb):
    return lax.dot_general(a, b, (((ca,), (cb,)), ((), ())), preferred_element_type=F32)


def _nn(a, b):
    return _dot(a, b, 1, 0)


def _nt(a, b):
    return _dot(a, b, 1, 1)


def _tn(a, b):
    return _dot(a, b, 0, 0)


def _mm(name, dims, grid, in_specs, out_spec, out_sds, nk, operands, res=False, norm=False, rms_bwd=False,
        ln_bwd=False):
    n_in = 2 + res + norm + 3 * rms_bwd + 3 * ln_bwd

    def body(*refs):
        ins, outs = refs[:n_in], refs[n_in:]
        a_ref, b_ref = ins[:2]
        extra = list(ins[2:])
        r_ref = extra.pop(0) if res else None
        g_ref = extra.pop(0) if norm else None
        o_ref = outs[0]
        row_tile = pl.program_id(0)
        prod = _dot(a_ref[...].astype(BF16), b_ref[...].astype(BF16), *dims)

        def finish(total):
            if rms_bwd:
                h_ref, gain_ref, dres_ref = extra
                dx, dg = _rms_bwd_math(h_ref[...], gain_ref[...], total)
                o_ref[...] = dx + dres_ref[...]
                _acc_rows(outs[1], row_tile, dg)
                return
            if ln_bwd:
                x_ref, gain_ref, bias_ref = extra
                o_ref[...], dg, db = _ln_silu_bwd_math(x_ref[...], gain_ref[...], bias_ref[...], total)
                _acc_rows(outs[1], row_tile, dg)
                _acc_rows(outs[2], row_tile, db)
                return
            o_ref[...] = total.astype(o_ref.dtype)
            if norm:
                r = lax.rsqrt(jnp.mean(total * total, axis=-1, keepdims=True) + RMS_EPS)
                outs[1][...] = (total * r * g_ref[...]).astype(BF16)

        if nk == 1:
            finish(prod + r_ref[...] if res else prod)
        else:
            k = pl.program_id(2)

            @pl.when(k == 0)
            def _():
                o_ref[...] = (prod + r_ref[...]) if res else prod

            @pl.when((k > 0) & (k < nk - 1))
            def _():
                o_ref[...] += prod

            @pl.when(k == nk - 1)
            def _():
                finish(o_ref[...] + prod)

    assert nk == 1 or out_sds.dtype == F32
    out_specs, out_shape = [out_spec], [out_sds]
    if norm:
        out_specs.append(out_spec)
        out_shape.append(_sds(out_sds.shape, BF16))
    for _ in range(rms_bwd + 2 * ln_bwd):
        n = out_sds.shape[1]
        out_specs.append(pl.BlockSpec((1, n), lambda i, j, k: (0, 0)))
        out_shape.append(_sds((1, n), F32))
    sem = ("arbitrary",) * 3 if (rms_bwd or ln_bwd) else ("parallel", "parallel", "arbitrary")
    out = pl.pallas_call(body, grid=grid, in_specs=in_specs, out_specs=out_specs, out_shape=out_shape, name=name,
                         compiler_params=_params(sem))(*operands)
    return out if (norm or rms_bwd or ln_bwd) else out[0]


def _div_tile(n, want):
    best = None
    for t in range(128, min(n, want) + 1, 128):
        if n % t == 0:
            best = t
    assert best is not None, (n, want)
    return best


def mm_fwd(name, a, g, l, col, out_dtype=F32, res=None, norm_g=None):
    m, k_dim = a.shape
    tm = min(MM_TILE_M, m)
    if col:
        _, _, kk, b = g.shape
        assert kk == k_dim
        tn = _div_tile(b, 1408)
        nps = b // tn
        n = 4 * b
        grid = (m // tm, n // tn, 1)
        in_specs = [pl.BlockSpec((tm, k_dim), lambda i, j, k: (i, 0)),
                    pl.BlockSpec((None, None, k_dim, tn), lambda i, j, k: (l, j // nps, 0, j % nps))]
        nk = 1
        w = g
    else:
        lyr, _, a_rows, n = g.shape
        assert 4 * a_rows == k_dim
        w = g.reshape(lyr, k_dim, n)
        tk = _div_tile(k_dim, 1408)
        tn = n
        nk = k_dim // tk
        grid = (m // tm, 1, nk)
        in_specs = [pl.BlockSpec((tm, tk), lambda i, j, k: (i, k)),
                    pl.BlockSpec((None, tk, tn), lambda i, j, k: (l, k, j))]
    ops = [a, w]
    if res is not None:
        in_specs.append(pl.BlockSpec((tm, tn), lambda i, j, k: (i, j)))
        ops.append(res)
    if norm_g is not None:
        assert tn == n
        in_specs.append(pl.BlockSpec((1, n), lambda i, j, k: (0, 0)))
        ops.append(norm_g)
    return _mm(name, (1, 0), grid, in_specs, pl.BlockSpec((tm, tn), lambda i, j, k: (i, j)), _sds((m, n), out_dtype),
               nk, ops, res=res is not None, norm=norm_g is not None)


def mm_dx(name, dy, g, l, col, rms=None, ln=None):
    m, n = dy.shape
    tm = min(MM_TILE_M, m)
    if col:
        _, _, k_dim, b = g.shape
        assert 4 * b == n
        tk = _div_tile(b, 1408)
        nps = b // tk
        nk = n // tk
        grid = (m // tm, 1, nk)
        in_specs = [pl.BlockSpec((tm, tk), lambda i, j, k: (i, k)),
                    pl.BlockSpec((None, None, k_dim, tk), lambda i, j, k: (l, k // nps, 0, k % nps))]
        out_spec = pl.BlockSpec((tm, k_dim), lambda i, j, k: (i, 0))
        w = g
        tno = k_dim
    else:
        lyr, _, a_rows, nn_ = g.shape
        assert nn_ == n
        k_dim = 4 * a_rows
        w = g.reshape(lyr, k_dim, n)
        tno = _div_tile(k_dim, 1408)
        nk = 1
        grid = (m // tm, k_dim // tno, 1)
        in_specs = [pl.BlockSpec((tm, n), lambda i, j, k: (i, 0)),
                    pl.BlockSpec((None, tno, n), lambda i, j, k: (l, j, 0))]
        out_spec = pl.BlockSpec((tm, tno), lambda i, j, k: (i, j))
    ops = [dy, w]
    if rms is not None:
        assert tno == k_dim
        h, gain, dres = rms
        rows = pl.BlockSpec((tm, k_dim), lambda i, j, k: (i, 0))
        in_specs += [rows, pl.BlockSpec((1, k_dim), lambda i, j, k: (0, 0)), rows]
        ops += [h, gain, dres]
    if ln is not None:
        assert tno == k_dim
        vec = pl.BlockSpec((1, k_dim), lambda i, j, k: (0, 0))
        in_specs += [pl.BlockSpec((tm, k_dim), lambda i, j, k: (i, 0)), vec, vec]
        ops += list(ln)
    return _mm(name, (1, 1), grid, in_specs, out_spec, _sds((m, k_dim), F32), nk, ops, rms_bwd=rms is not None,
               ln_bwd=ln is not None)


def mm_dw(name, a, dy, col, l, n_layers, into=None):
    m, k_dim = a.shape
    _, n = dy.shape
    ts = min(MM_TILE_M, m)
    ns = m // ts
    tko = _div_tile(k_dim, 1408)
    if col:
        b = n // 4
        tn = _div_tile(b, 1408)
        nps = b // tn
        grid = (k_dim // tko, n // tn, ns)
        out_spec = pl.BlockSpec((None, None, tko, tn), lambda i, j, s: (l, j // nps, i, j % nps))
        shape = (n_layers, 4, k_dim, b)
    else:
        tn = n
        grid = (k_dim // tko, 1, ns)
        out_spec = pl.BlockSpec((None, tko, tn), lambda i, j, s: (l, i, j))
        shape = (n_layers, k_dim, n)
    in_specs = [pl.BlockSpec((ts, tko), lambda i, j, s: (s, i)),
                pl.BlockSpec((ts, tn), lambda i, j, s: (s, j))]
    ops = [a, dy]
    aliases = {}
    if into is not None:
        in_specs.append(_ANY)
        ops.append(into.reshape(shape))
        aliases = {2: 0}

    def body(a_ref, b_ref, *rest):
        o_ref = rest[-1]
        prod = _tn(a_ref[...].astype(BF16), b_ref[...].astype(BF16))
        if ns == 1:
            o_ref[...] = prod
        else:
            s = pl.program_id(2)

            @pl.when(s == 0)
            def _():
                o_ref[...] = prod

            @pl.when(s > 0)
            def _():
                o_ref[...] += prod

    out = pl.pallas_call(body, grid=grid, in_specs=in_specs, out_specs=out_spec, out_shape=_sds(shape, F32),
                         input_output_aliases=aliases, name=name,
                         compiler_params=_params(("parallel", "parallel", "arbitrary")))(*ops)
    return out.reshape(n_layers, 4, k_dim if col else k_dim // 4, shape[-1])


def _row_spec(t, c, col=0):
    return pl.BlockSpec((t, c), lambda i: (i, col))


def _vec_spec(c):
    return pl.BlockSpec((1, c), lambda i: (0, 0))


def _acc_rows(ref, i, val):
    @pl.when(i == 0)
    def _():
        ref[...] = val

    @pl.when(i > 0)
    def _():
        ref[...] += val


def rope_tables(positions, inv_freq):
    s = positions.shape[0]
    t = min(ROW_TILE, s)
    half = inv_freq.shape[1]

    def body(p_ref, f_ref, c_ref, s_ref):
        ang = p_ref[...].astype(F32) * f_ref[...]
        c_ref[...] = jnp.cos(ang)
        s_ref[...] = jnp.sin(ang)

    return pl.pallas_call(
        body, grid=(s // t,), in_specs=[_row_spec(t, 1), _vec_spec(half)],
        out_specs=[_row_spec(t, half), _row_spec(t, half)], out_shape=[_sds((s, half), F32)] * 2, name="rope_tables",
        compiler_params=_params(("parallel",)))(positions, inv_freq)


def rms_cast(h, g):
    s, d = h.shape
    t = min(ROW_TILE, s)

    def body(h_ref, g_ref, o_ref):
        x = h_ref[...]
        r = lax.rsqrt(jnp.mean(x * x, axis=-1, keepdims=True) + RMS_EPS)
        o_ref[...] = (x * r * g_ref[...]).astype(BF16)

    return pl.pallas_call(body, grid=(s // t,), in_specs=[_row_spec(t, d), _vec_spec(d)], out_specs=_row_spec(t, d),
                          out_shape=_sds((s, d), BF16), name="rms_cast", compiler_params=_params(("parallel",)))(h, g)


def _rms_bwd_math(x, g, du):
    r = lax.rsqrt(jnp.mean(x * x, axis=-1, keepdims=True) + RMS_EPS)
    gd = g * du
    dx = r * gd - x * (r * r * r) * jnp.mean(x * gd, axis=-1, keepdims=True)
    dg = jnp.sum(x * r * du, axis=0, keepdims=True)
    return dx, dg


def rms_bwd(h, g, du, dres=None):
    s, d = h.shape
    t = min(ROW_TILE, s)

    def body(*refs):
        if dres is None:
            h_ref, g_ref, du_ref, dh_ref, dg_ref = refs
        else:
            h_ref, g_ref, du_ref, dr_ref, dh_ref, dg_ref = refs
        dx, dg = _rms_bwd_math(h_ref[...], g_ref[...], du_ref[...])
        if dres is not None:
            dx = dx + dr_ref[...]
        dh_ref[...] = dx
        _acc_rows(dg_ref, pl.program_id(0), dg)

    in_specs = [_row_spec(t, d), _vec_spec(d), _row_spec(t, d)]
    ops = [h, g, du]
    if dres is not None:
        in_specs.append(_row_spec(t, d))
        ops.append(dres)
    return pl.pallas_call(body, grid=(s // t,), in_specs=in_specs, out_specs=[_row_spec(t, d), _vec_spec(d)],
                          out_shape=[_sds((s, d), F32), _sds((1, d), F32)], name="rms_bwd",
                          compiler_params=_params(("arbitrary",)))(*ops)


def loss_head(h, g, target):
    s, d = h.shape
    t = min(ROW_TILE, s)

    def body(h_ref, g_ref, t_ref, dh_ref, dg_ref, loss_ref):
        x = h_ref[...]
        gg = g_ref[...]
        r = lax.rsqrt(jnp.mean(x * x, axis=-1, keepdims=True) + RMS_EPS)
        err = x * r * gg - t_ref[...]
        part = 0.5 * jnp.sum(jnp.mean(err * err, axis=-1, keepdims=True), axis=0, keepdims=True)
        dy = err * (1.0 / d)
        dx, dg = _rms_bwd_math(x, gg, dy)
        dh_ref[...] = dx
        i = pl.program_id(0)
        _acc_rows(dg_ref, i, dg)
        _acc_rows(loss_ref, i, jnp.broadcast_to(part, (1, 128)))

    return pl.pallas_call(
        body, grid=(s // t,), in_specs=[_row_spec(t, d), _vec_spec(d), _row_spec(t, d)],
        out_specs=[_row_spec(t, d), _vec_spec(d), _vec_spec(128)],
        out_shape=[_sds((s, d), F32), _sds((1, d), F32), _sds((1, 128), F32)], name="loss_head",
        compiler_params=_params(("arbitrary",)))(h, g, target)


def _rot(x, cos, sin):
    half = x.shape[-1] // 2
    x1, x2 = x[:, :half], x[:, half:]
    return jnp.concatenate([x1 * cos - x2 * sin, x2 * cos + x1 * sin], axis=-1)


def _rot_t(dy, cos, sin):
    half = dy.shape[-1] // 2
    d1, d2 = dy[:, :half], dy[:, half:]
    return jnp.concatenate([d1 * cos + d2 * sin, d2 * cos - d1 * sin], axis=-1)


def _decay_tables(t):
    log_gamma = jnp.log(1.0 - jnp.power(2.0, -5.0 - jnp.arange(RET_HEADS, dtype=F32)))
    idx = jnp.arange(t, dtype=F32)
    dist = jnp.abs(idx[:, None] - idx[None, :])
    chunk = jnp.arange(t) // CHUNK
    seen = chunk[None, :] <= chunk[:, None]
    d_tile = jnp.where(seen[None], jnp.exp(log_gamma[:, None, None] * dist), 0.0)
    decay_q = jnp.exp(log_gamma[:, None] * (idx[None, :] + 1.0))[:, :, None]
    decay_k = jnp.exp(log_gamma[:, None] * (t - 1.0 - idx[None, :]))[:, :, None]
    decay_tile = jnp.exp(log_gamma * t)[:, None, None]
    return d_tile, decay_q, decay_k, decay_tile


_QK_SCALE = RET_QK_DIM ** -0.5


def _retention_specs(t, nt, order):
    dk, dv = RET_QK_DIM, RET_V_DIM
    hmap = lambda h, i: (h, 0, 0)
    qkv = [pl.BlockSpec((t, dk), lambda h, i: (order(i), h)),
           pl.BlockSpec((t, dk), lambda h, i: (order(i), RET_HEADS + h)),
           pl.BlockSpec((t, dv), lambda h, i: (order(i), 4 + h))]
    rope = [pl.BlockSpec((t, dk // 2), lambda h, i: (order(i), 0))] * 2
    tables = [pl.BlockSpec((None, t, t), hmap), pl.BlockSpec((None, t, 1), hmap), pl.BlockSpec((None, t, 1), hmap),
              pl.BlockSpec((None, 1, 1), hmap)]
    return qkv, rope, tables


def retention_fwd(p, cos, sin, gn_g, tables):
    s = p.shape[0]
    t = min(RET_TILE, s)
    nt = s // t
    dk, dv = RET_QK_DIM, RET_V_DIM

    def body(q_ref, k_ref, v_ref, cos_ref, sin_ref, di_ref, dq_ref, dkk_ref, dc_ref, gr_ref, gn_ref,
             o_ref, z_ref, st_ref, state):
        @pl.when(pl.program_id(1) == 0)
        def _():
            state[...] = jnp.zeros_like(state)

        cs, sn = cos_ref[...], sin_ref[...]
        qb = (_rot(q_ref[...], cs, sn) * _QK_SCALE).astype(BF16)
        kr = _rot(k_ref[...], cs, sn)
        vb = v_ref[...].astype(BF16)
        st = state[...].astype(BF16)
        st_ref[...] = st
        scores = _nt(qb, kr.astype(BF16)) * di_ref[...]
        o = _nn(scores.astype(BF16), vb) + _nn(qb, st) * dq_ref[...]
        state[...] = state[...] * dc_ref[...] + _tn((kr * dkk_ref[...]).astype(BF16), vb)
        o_ref[...] = o
        mu = jnp.mean(o, axis=-1, keepdims=True)
        oc = o - mu
        var = jnp.mean(oc * oc, axis=-1, keepdims=True)
        y = oc * lax.rsqrt(var + LN_EPS) * gn_ref[...]
        gr = gr_ref[...]
        z_ref[...] = (gr * _sigmoid(gr) * y).astype(BF16)

    qkv, rope, tabs = _retention_specs(t, nt, lambda i: i)
    in_specs = qkv + rope + tabs + [pl.BlockSpec((t, dv), lambda h, i: (i, 8 + h)),
                                    pl.BlockSpec((1, dv), lambda h, i: (0, h))]
    out_specs = [pl.BlockSpec((t, dv), lambda h, i: (i, h)),
                 pl.BlockSpec((t, dv), lambda h, i: (i, h)),
                 pl.BlockSpec((None, None, dk, dv), lambda h, i: (h, i, 0, 0))]
    out_shape = [_sds((s, RET_HEADS * dv), F32), _sds((s, RET_HEADS * dv), BF16), _sds((RET_HEADS, nt, dk, dv), BF16)]
    return pl.pallas_call(
        body, grid=(RET_HEADS, nt), in_specs=in_specs, out_specs=out_specs, out_shape=out_shape,
        scratch_shapes=[pltpu.VMEM((dk, dv), F32)], name="retention_fwd",
        compiler_params=_params(("parallel", "arbitrary")))(p, p, p, cos, sin, *tables, p, gn_g)


def gn_gate_bwd(o, p, gn_g, dz):
    s = o.shape[0]
    t = min(ROW_TILE, s)
    dv = RET_V_DIM
    w = RET_HEADS * dv

    def body(o_ref, gr_ref, gn_ref, dz_ref, do_ref, dgr_ref, dgn_ref):
        dgn_parts = []
        for h in range(RET_HEADS):
            sl = slice(h * dv, (h + 1) * dv)
            oo = o_ref[:, sl]
            gr = gr_ref[:, sl]
            dz = dz_ref[:, sl]
            gn = gn_ref[:, sl]
            mu = jnp.mean(oo, axis=-1, keepdims=True)
            oc = oo - mu
            rstd = lax.rsqrt(jnp.mean(oc * oc, axis=-1, keepdims=True) + LN_EPS)
            y = oc * rstd
            sg = _sigmoid(gr)
            act = gr * sg
            dyg = dz * act
            dgn_parts.append(jnp.sum(dyg * y, axis=0, keepdims=True))
            dy = dyg * gn
            do_ref[:, sl] = rstd * (dy - jnp.mean(dy, axis=-1, keepdims=True)
                                    - y * jnp.mean(dy * y, axis=-1, keepdims=True))
            dgr_ref[:, sl] = (dz * (y * gn) * (sg * (1.0 + gr * (1.0 - sg)))).astype(BF16)
        _acc_rows(dgn_ref, pl.program_id(0), jnp.concatenate(dgn_parts, axis=-1))

    return pl.pallas_call(
        body, grid=(s // t,), in_specs=[_row_spec(t, w), _row_spec(t, w, 2), _vec_spec(w), _row_spec(t, w)],
        out_specs=[_row_spec(t, w), _row_spec(t, w), _vec_spec(w)],
        out_shape=[_sds((s, w), F32), _sds((s, w), BF16), _sds((1, w), F32)], name="gn_gate_bwd",
        compiler_params=_params(("arbitrary",)))(o, p, gn_g, dz)


def retention_bwd(p, cos, sin, states, do, tables):
    s = p.shape[0]
    t = min(RET_TILE, s)
    nt = s // t
    dk, dv = RET_QK_DIM, RET_V_DIM

    def body(q_ref, k_ref, v_ref, cos_ref, sin_ref, di_ref, dq_ref, dkk_ref, dc_ref, st_ref, do_ref,
             gq_ref, gk_ref, gv_ref, dstate):
        @pl.when(pl.program_id(1) == 0)
        def _():
            dstate[...] = jnp.zeros_like(dstate)

        cs, sn = cos_ref[...], sin_ref[...]
        qb = (_rot(q_ref[...], cs, sn) * _QK_SCALE).astype(BF16)
        kr = _rot(k_ref[...], cs, sn)
        kb = kr.astype(BF16)
        vb = v_ref[...].astype(BF16)
        dmat, dkk = di_ref[...], dkk_ref[...]
        d_o = do_ref[...]
        dob = d_o.astype(BF16)
        dsb = dstate[...].astype(BF16)
        ab = (_nt(qb, kb) * dmat).astype(BF16)
        gv_ref[...] = (_tn(ab, dob) + _nn((kr * dkk).astype(BF16), dsb)).astype(BF16)
        dcb = (d_o * dq_ref[...]).astype(BF16)
        dpb = (_nt(dob, vb) * dmat).astype(BF16)
        dqq = _nt(dcb, st_ref[...]) + _nn(dpb, kb)
        dkv = _tn(dpb, qb) + _nt(vb, dsb) * dkk
        dstate[...] = dstate[...] * dc_ref[...] + _tn(qb, dcb)
        gq_ref[...] = _rot_t(dqq * _QK_SCALE, cs, sn).astype(BF16)
        gk_ref[...] = _rot_t(dkv, cs, sn).astype(BF16)

    rev = lambda i: nt - 1 - i
    qkv, rope, tabs = _retention_specs(t, nt, rev)
    in_specs = qkv + rope + tabs + [pl.BlockSpec((None, None, dk, dv), lambda h, i: (h, rev(i), 0, 0)),
                                    pl.BlockSpec((t, dv), lambda h, i: (rev(i), h))]
    out_specs = [pl.BlockSpec((t, dk), lambda h, i: (rev(i), h)),
                 pl.BlockSpec((t, dk), lambda h, i: (rev(i), h)),
                 pl.BlockSpec((t, dv), lambda h, i: (rev(i), h))]
    out_shape = [_sds((s, RET_HEADS * dk), BF16), _sds((s, RET_HEADS * dk), BF16), _sds((s, RET_HEADS * dv), BF16)]
    return pl.pallas_call(
        body, grid=(RET_HEADS, nt), in_specs=in_specs, out_specs=out_specs, out_shape=out_shape,
        scratch_shapes=[pltpu.VMEM((dk, dv), F32)], name="retention_bwd",
        compiler_params=_params(("parallel", "arbitrary")))(p, p, p, cos, sin, *tables, states, do)


A_COL, B_COL = 6, 7


def _prev_rows_spec(t, halo, width, col):
    per = t // halo
    return pl.BlockSpec((halo, width), lambda i: (jnp.maximum(i * per - 1, 0), col))


def _next_rows_spec(t, halo, width, col, n_rows):
    per = t // halo
    last = n_rows // halo - 1
    return pl.BlockSpec((halo, width), lambda i: (jnp.minimum((i + 1) * per, last), col))


def _shifted_copies(ext, rows):
    for b in range(1, 8):
        ext[b, pl.ds(0, rows - 8), :] = ext[0, pl.ds(b, rows - 8), :]


def _shifted(ext, start, lanes):
    return ext[start % 8, pl.ds(start - start % 8, STRIP_ROWS), lanes]


def conv_fwd(p, dw_w, dw_b, ln_g, ln_b):
    s = p.shape[0]
    t = min(ROW_TILE, s)
    c = D_MODEL
    hl = CONV_HALO

    def body(a_ref, b_ref, ah_ref, bh_ref, w_ref, wb_ref, g_ref, bb_ref, c1_ref, c3_ref, ext):
        i = pl.program_id(0)
        ext[0, pl.ds(0, hl), :] = jnp.where(i > 0, ah_ref[...] * _sigmoid(bh_ref[...]), 0.0)
        ext[0, pl.ds(hl, t), :] = a_ref[...] * _sigmoid(b_ref[...])
        _shifted_copies(ext, t + hl)
        first = hl - (CONV_WIDTH - 1)
        for lane in range(0, c, STRIP_LANES):
            ls = slice(lane, lane + STRIP_LANES)
            for r0 in range(0, t, STRIP_ROWS):
                accs = [jnp.broadcast_to(wb_ref[:, ls], (STRIP_ROWS, STRIP_LANES)),
                        jnp.zeros((STRIP_ROWS, STRIP_LANES), F32)]
                for j in range(CONV_WIDTH):
                    accs[j % 2] = accs[j % 2] + w_ref[j:j + 1, ls] * _shifted(ext, r0 + first + j, ls)
                c1_ref[r0:r0 + STRIP_ROWS, ls] = accs[0] + accs[1]
        acc = c1_ref[...]
        mu = jnp.mean(acc, axis=-1, keepdims=True)
        xc = acc - mu
        var = jnp.mean(xc * xc, axis=-1, keepdims=True)
        c2 = xc * lax.rsqrt(var + LN_EPS) * g_ref[...] + bb_ref[...]
        c3_ref[...] = (c2 * _sigmoid(c2)).astype(BF16)

    in_specs = [_row_spec(t, c, A_COL), _row_spec(t, c, B_COL),
                _prev_rows_spec(t, hl, c, A_COL), _prev_rows_spec(t, hl, c, B_COL),
                pl.BlockSpec((CONV_WIDTH, c), lambda i: (0, 0)), _vec_spec(c), _vec_spec(c), _vec_spec(c)]
    return pl.pallas_call(
        body, grid=(s // t,), in_specs=in_specs, out_specs=[_row_spec(t, c), _row_spec(t, c)],
        out_shape=[_sds((s, c), F32), _sds((s, c), BF16)], scratch_shapes=[pltpu.VMEM((8, t + hl, c), F32)],
        name="conv_fwd", compiler_params=_params(("parallel",)))(p, p, p, p, dw_w, dw_b, ln_g, ln_b)


def _ln_silu_bwd_math(x, g, b, d):
    mu = jnp.mean(x, axis=-1, keepdims=True)
    xc = x - mu
    rstd = lax.rsqrt(jnp.mean(xc * xc, axis=-1, keepdims=True) + LN_EPS)
    y = xc * rstd
    c2 = y * g + b
    sg = _sigmoid(c2)
    dc2 = d * (sg * (1.0 + c2 * (1.0 - sg)))
    dy = dc2 * g
    dx = rstd * (dy - jnp.mean(dy, axis=-1, keepdims=True) - y * jnp.mean(dy * y, axis=-1, keepdims=True))
    return dx, jnp.sum(dc2 * y, axis=0, keepdims=True), jnp.sum(dc2, axis=0, keepdims=True)


def conv_dw_bwd(p, dc1, dw_w):
    s = p.shape[0]
    t = min(ROW_TILE, s)
    c = D_MODEL
    hl = CONV_HALO
    nt = s // t

    def body(a_ref, b_ref, ah_ref, bh_ref, d_ref, dn_ref, w_ref, dab_ref, dw_ref, dbias_ref, ext_c, ext_d, dc0_s,
             dw_s):
        i = pl.program_id(0)
        ext_c[0, pl.ds(0, hl), :] = jnp.where(i > 0, ah_ref[...] * _sigmoid(bh_ref[...]), 0.0)
        ext_c[0, pl.ds(hl, t), :] = a_ref[...] * _sigmoid(b_ref[...])
        ext_d[0, pl.ds(0, t), :] = d_ref[...]
        ext_d[0, pl.ds(t, hl), :] = jnp.where(i < nt - 1, dn_ref[...], 0.0)
        _shifted_copies(ext_c, t + hl)
        _shifted_copies(ext_d, t + hl)
        first = hl - (CONV_WIDTH - 1)

        def fold8(x):
            rows = [x[k:k + 8] for k in range(0, STRIP_ROWS, 8)]
            while len(rows) > 1:
                rows = [rows[k] + rows[k + 1] for k in range(0, len(rows), 2)]
            return rows[0]

        for lane in range(0, c, STRIP_LANES):
            ls = slice(lane, lane + STRIP_LANES)
            for r0 in range(0, t, STRIP_ROWS):
                accs = [jnp.zeros((STRIP_ROWS, STRIP_LANES), F32) for _ in range(2)]
                for j in range(CONV_WIDTH):
                    accs[j % 2] = accs[j % 2] + w_ref[j:j + 1, ls] * _shifted(ext_d, r0 + CONV_WIDTH - 1 - j, ls)
                dc0_s[r0:r0 + STRIP_ROWS, ls] = accs[0] + accs[1]
            for j0 in range(0, CONV_WIDTH, DW_TAPS):
                taps = range(j0, min(j0 + DW_TAPS, CONV_WIDTH))
                parts = [jnp.zeros((8, STRIP_LANES), F32) for _ in taps]
                for r0 in range(0, t, STRIP_ROWS):
                    d = ext_d[0, r0:r0 + STRIP_ROWS, ls]
                    for k, j in enumerate(taps):
                        parts[k] = parts[k] + fold8(d * _shifted(ext_c, r0 + first + j, ls))
                for k, j in enumerate(taps):
                    dw_s[j:j + 1, ls] = jnp.sum(parts[k], axis=0, keepdims=True)
        _acc_rows(dw_ref, i, dw_s[0:CONV_WIDTH, :])
        d = d_ref[...]
        _acc_rows(dbias_ref, i, jnp.sum(d, axis=0, keepdims=True))
        dc0 = dc0_s[...]
        a = a_ref[...]
        sb = _sigmoid(b_ref[...])
        dab_ref[:, :c] = (dc0 * sb).astype(BF16)
        dab_ref[:, c:] = (dc0 * a * sb * (1.0 - sb)).astype(BF16)

    in_specs = [_row_spec(t, c, A_COL), _row_spec(t, c, B_COL),
                _prev_rows_spec(t, hl, c, A_COL), _prev_rows_spec(t, hl, c, B_COL),
                _row_spec(t, c), _next_rows_spec(t, hl, c, 0, s),
                pl.BlockSpec((CONV_WIDTH, c), lambda i: (0, 0))]
    return pl.pallas_call(
        body, grid=(nt,), in_specs=in_specs,
        out_specs=[_row_spec(t, 2 * c), pl.BlockSpec((CONV_WIDTH, c), lambda i: (0, 0)), _vec_spec(c)],
        out_shape=[_sds((s, 2 * c), BF16), _sds((CONV_WIDTH, c), F32), _sds((1, c), F32)],
        scratch_shapes=[pltpu.VMEM((8, t + hl, c), F32), pltpu.VMEM((8, t + hl, c), F32), pltpu.VMEM((t, c), F32),
                        pltpu.VMEM((CONV_HALO, c), F32)], name="conv_dw_bwd",
        compiler_params=_params(("arbitrary",)))(p, p, p, p, dc1, dc1, dw_w)


GATE_COL = 4


def mix_out_fwd(p, b_gate, y_a, y_b, b_conv_out, g, l, res, norm_g):
    s = p.shape[0]
    c = D_MODEL
    t = min(MIX_TILE, s)
    lyr = g.shape[0]
    w = g.reshape(lyr, c, c)

    def body(gt_ref, bg_ref, ya_ref, yb_ref, bc_ref, w_ref, r_ref, ng_ref, h_ref, u_ref, mx_ref):
        gs = _sigmoid(gt_ref[...] + bg_ref[...])
        mixed = (gs[:, :c] * ya_ref[...] + gs[:, c:] * (yb_ref[...] + bc_ref[...])).astype(BF16)
        mx_ref[...] = mixed
        total = _nn(mixed, w_ref[...]) + r_ref[...]
        h_ref[...] = total
        r = lax.rsqrt(jnp.mean(total * total, axis=-1, keepdims=True) + RMS_EPS)
        u_ref[...] = (total * r * ng_ref[...]).astype(BF16)

    rows = _row_spec(t, c)
    return pl.pallas_call(
        body, grid=(s // t,),
        in_specs=[_row_spec(t, 2 * c, GATE_COL), _vec_spec(2 * c), rows, rows, _vec_spec(c),
                  pl.BlockSpec((None, c, c), lambda i: (l, 0, 0)), rows, _vec_spec(c)],
        out_specs=[rows, rows, rows], out_shape=[_sds((s, c), F32), _sds((s, c), BF16), _sds((s, c), BF16)],
        name="mix_out_fwd", compiler_params=_params(("parallel",)))(p, b_gate, y_a, y_b, b_conv_out, w, res, norm_g)


def gate_mix_bwd(p, b_gate, y_a, y_b, b_conv_out, dmix):
    s = p.shape[0]
    t = min(ROW_TILE, s)
    c = D_MODEL

    def body(gt_ref, bg_ref, ya_ref, yb_ref, bc_ref, d_ref, dya_ref, dyb_ref, dgt_ref, dbg_ref, dbc_ref):
        gs = _sigmoid(gt_ref[...] + bg_ref[...])
        ga, gb = gs[:, :c], gs[:, c:]
        d = d_ref[...]
        dya = ga * d
        dyb = gb * d
        dya_ref[...] = dya.astype(BF16)
        dyb_ref[...] = dyb.astype(BF16)
        dga = d * ya_ref[...] * ga * (1.0 - ga)
        dgb = d * (yb_ref[...] + bc_ref[...]) * gb * (1.0 - gb)
        dgt_ref[:, :c] = dga.astype(BF16)
        dgt_ref[:, c:] = dgb.astype(BF16)
        i = pl.program_id(0)
        _acc_rows(dbg_ref, i, jnp.concatenate([jnp.sum(dga, axis=0, keepdims=True),
                                               jnp.sum(dgb, axis=0, keepdims=True)], axis=-1))
        _acc_rows(dbc_ref, i, jnp.sum(dyb, axis=0, keepdims=True))

    return pl.pallas_call(
        body, grid=(s // t,),
        in_specs=[_row_spec(t, 2 * c, GATE_COL), _vec_spec(2 * c), _row_spec(t, c), _row_spec(t, c), _vec_spec(c),
                  _row_spec(t, c)],
        out_specs=[_row_spec(t, c), _row_spec(t, c), _row_spec(t, 2 * c), _vec_spec(2 * c), _vec_spec(c)],
        out_shape=[_sds((s, c), BF16), _sds((s, c), BF16), _sds((s, 2 * c), BF16), _sds((1, 2 * c), F32),
                   _sds((1, c), F32)],
        name="gate_mix_bwd", compiler_params=_params(("arbitrary",)))(p, b_gate, y_a, y_b, b_conv_out, dmix)


_X_SCALE = X_HEAD_DIM ** -0.5


def _softmax_rows(sc):
    m = jnp.max(sc, axis=-1, keepdims=True)
    e = jnp.exp(sc - m)
    return e / jnp.sum(e, axis=-1, keepdims=True)


def attn_fwd(qx, kv):
    s, d = qx.shape
    m = kv.shape[0]
    t = min(ATTN_TILE, s)
    hd = X_HEAD_DIM

    def body(q_ref, kv_ref, o_ref):
        for h in range(X_HEADS):
            sl = slice(h * hd, (h + 1) * hd)
            kh = kv_ref[:, sl].astype(BF16)
            vh = kv_ref[:, d + h * hd:d + (h + 1) * hd].astype(BF16)
            pr = _softmax_rows(_nt(q_ref[:, sl], kh) * _X_SCALE)
            o_ref[:, sl] = _nn(pr.astype(BF16), vh).astype(BF16)

    return pl.pallas_call(
        body, grid=(s // t,), in_specs=[_row_spec(t, d), pl.BlockSpec((m, 2 * d), lambda i: (0, 0))],
        out_specs=_row_spec(t, d), out_shape=_sds((s, d), BF16), name="attn_fwd",
        compiler_params=_params(("parallel",)))(qx, kv)


def attn_bwd(qx, kv, dox):
    s, d = qx.shape
    m = kv.shape[0]
    t = min(ATTN_TILE, s)
    hd = X_HEAD_DIM

    def body(q_ref, kv_ref, do_ref, dq_ref, dkv_ref):
        dks, dvs = [], []
        for h in range(X_HEADS):
            sl = slice(h * hd, (h + 1) * hd)
            qh = q_ref[:, sl]
            kh = kv_ref[:, sl].astype(BF16)
            vh = kv_ref[:, d + h * hd:d + (h + 1) * hd].astype(BF16)
            pr = _softmax_rows(_nt(qh, kh) * _X_SCALE)
            doh = do_ref[:, sl].astype(BF16)
            dpr = _nt(doh, vh)
            dvs.append(_tn(pr.astype(BF16), doh))
            ds = pr * (dpr - jnp.sum(dpr * pr, axis=-1, keepdims=True))
            dsb = (ds * _X_SCALE).astype(BF16)
            dq_ref[:, sl] = _nn(dsb, kh).astype(BF16)
            dks.append(_tn(dsb, qh))
        _acc_rows(dkv_ref, pl.program_id(0), jnp.concatenate(dks + dvs, axis=-1))

    return pl.pallas_call(
        body, grid=(s // t,),
        in_specs=[_row_spec(t, d), pl.BlockSpec((m, 2 * d), lambda i: (0, 0)), _row_spec(t, d)],
        out_specs=[_row_spec(t, d), pl.BlockSpec((m, 2 * d), lambda i: (0, 0))],
        out_shape=[_sds((s, d), BF16), _sds((m, 2 * d), F32)], name="attn_bwd",
        compiler_params=_params(("arbitrary",)))(qx, kv, dox)


def _offset_copies(ext, offsets, rows):
    for k, off in enumerate(offsets):
        ext[1 + k, pl.ds(0, rows), :] = ext[0, pl.ds(off, rows), :]


def _ffn_blocks(rows, lanes):
    return [(r0, slice(l0, l0 + FFN_LANES)) for r0 in range(0, rows, FFN_ROWS) for l0 in range(0, lanes, FFN_LANES)]


def ffn_act_fwd(up, dw_w, dw_b):
    s = up.shape[0]
    f = FFN_DIM
    t = min(ROW_TILE, s)
    hl = FFN_HALO

    def body(val_ref, gt_ref, gh_ref, w_ref, b_ref, o_ref, ext):
        i = pl.program_id(0)
        ext[0, pl.ds(0, hl), :] = jnp.where(i > 0, gh_ref[...], 0.0)
        ext[0, pl.ds(hl, t), :] = gt_ref[...]
        _offset_copies(ext, (hl - 2, hl - 1), t)
        for r0, ls in _ffn_blocks(t, f):
            gc = b_ref[:, ls] + w_ref[0:1, ls] * ext[1, pl.ds(r0, FFN_ROWS), ls] \
                + w_ref[1:2, ls] * ext[2, pl.ds(r0, FFN_ROWS), ls] + w_ref[2:3, ls] * ext[0, pl.ds(r0 + hl, FFN_ROWS), ls]
            o_ref[r0:r0 + FFN_ROWS, ls] = (gc * _sigmoid(gc) * val_ref[r0:r0 + FFN_ROWS, ls]).astype(BF16)

    return pl.pallas_call(
        body, grid=(s // t,),
        in_specs=[_row_spec(t, f, 0), _row_spec(t, f, 1), _prev_rows_spec(t, hl, f, 1),
                  pl.BlockSpec((3, f), lambda i: (0, 0)), _vec_spec(f)],
        out_specs=_row_spec(t, f), out_shape=_sds((s, f), BF16), scratch_shapes=[pltpu.VMEM((3, t + hl, f), F32)],
        name="ffn_act_fwd", compiler_params=_params(("parallel",)))(up, up, up, dw_w, dw_b)


def ffn_act_bwd(up, dw_w, dw_b, da):
    s = up.shape[0]
    f = FFN_DIM
    t = min(ROW_TILE, s)
    hl = FFN_HALO
    nt = s // t

    def body(val_ref, valn_ref, gt_ref, gp_ref, gn_ref, da_ref, dan_ref, w_ref, b_ref,
             dup_ref, dw_ref, db_ref, ext_g, ext_d, sums):
        i = pl.program_id(0)
        ext_g[0, pl.ds(0, hl), :] = jnp.where(i > 0, gp_ref[...], 0.0)
        ext_g[0, pl.ds(hl, t), :] = gt_ref[...]
        ext_g[0, pl.ds(hl + t, hl), :] = gn_ref[...]
        _offset_copies(ext_g, (hl - 2, hl - 1), t + hl)
        sums[...] = jnp.zeros_like(sums)

        def fold8(x):
            out = x[0:8]
            for k in range(8, x.shape[0], 8):
                out = out + x[k:k + 8]
            return out

        def gate_block(rows, off, ls, val, da_rows):
            taps = [ext_g[1, pl.ds(off, rows), ls], ext_g[2, pl.ds(off, rows), ls], ext_g[0, pl.ds(off + hl, rows), ls]]
            gc = b_ref[:, ls] + w_ref[0:1, ls] * taps[0] + w_ref[1:2, ls] * taps[1] + w_ref[2:3, ls] * taps[2]
            sg = _sigmoid(gc)
            return taps, gc * sg, da_rows * val * (sg * (1.0 + gc * (1.0 - sg)))

        for r0, ls in _ffn_blocks(t, f):
            rs = slice(r0, r0 + FFN_ROWS)
            da_rows = da_ref[rs, ls]
            taps, act, dgc = gate_block(FFN_ROWS, r0, ls, val_ref[rs, ls], da_rows)
            ext_d[0, rs, ls] = dgc
            dup_ref[rs, ls] = (da_rows * act).astype(BF16)
            for k in range(3):
                sums[8 * k:8 * k + 8, ls] += fold8(dgc * taps[k])
            sums[24:32, ls] += fold8(dgc)
        _, _, dgc_next = gate_block(hl, t, slice(None), valn_ref[...], dan_ref[...])
        ext_d[0, pl.ds(t, hl), :] = jnp.where(i < nt - 1, dgc_next, 0.0)
        _offset_copies(ext_d, (1, 2), t)
        for r0, ls in _ffn_blocks(t, f):
            gate_lanes = slice(f + ls.start, f + ls.stop)
            rs = pl.ds(r0, FFN_ROWS)
            dup_ref[r0:r0 + FFN_ROWS, gate_lanes] = (
                w_ref[2:3, ls] * ext_d[0, rs, ls] + w_ref[1:2, ls] * ext_d[1, rs, ls]
                + w_ref[0:1, ls] * ext_d[2, rs, ls]).astype(BF16)
        rows = [jnp.sum(sums[8 * k:8 * k + 8, :], axis=0, keepdims=True) for k in range(4)]
        _acc_rows(dw_ref, i, jnp.concatenate(rows[:3], axis=0))
        _acc_rows(db_ref, i, rows[3])

    in_specs = [_row_spec(t, f, 0), _next_rows_spec(t, hl, f, 0, s),
                _row_spec(t, f, 1), _prev_rows_spec(t, hl, f, 1), _next_rows_spec(t, hl, f, 1, s),
                _row_spec(t, f), _next_rows_spec(t, hl, f, 0, s),
                pl.BlockSpec((3, f), lambda i: (0, 0)), _vec_spec(f)]
    return pl.pallas_call(
        body, grid=(nt,), in_specs=in_specs,
        out_specs=[_row_spec(t, 2 * f), pl.BlockSpec((3, f), lambda i: (0, 0)), _vec_spec(f)],
        out_shape=[_sds((s, 2 * f), BF16), _sds((3, f), F32), _sds((1, f), F32)],
        scratch_shapes=[pltpu.VMEM((3, t + 2 * hl, f), F32), pltpu.VMEM((3, t + hl, f), F32),
                        pltpu.VMEM((32, f), F32)],
        name="ffn_act_bwd",
        compiler_params=_params(("arbitrary",)))(up, up, up, up, up, da, da, dw_w, dw_b)


def local_step(x, mem, positions, target, gw, sp):
    n_layers = gw["w_in"].shape[0]
    inv_freq = 1.0 / (ROPE_THETA ** (jnp.arange(0, RET_QK_DIM, 2, dtype=F32) / RET_QK_DIM))
    cos, sin = rope_tables(positions, inv_freq[None, :])
    tables = _decay_tables(min(RET_TILE, x.shape[0]))
    row = lambda name, l: sp[name][l][None, :]

    saved = []
    h = x
    u = rms_cast(x, row("norm_mix_g", 0))
    for l in range(n_layers):
        a = {"h0": h}
        a["u"] = u
        a["p"] = mm_fwd("mm_in", a["u"], gw["w_in"], l, True)
        a["o"], a["z"], a["states"] = retention_fwd(a["p"], cos, sin, row("ret_gn_g", l), tables)
        a["y_a"] = mm_fwd("mm_ret_out", a["z"], gw["w_ret_out"], l, False)
        a["c1"], a["c3"] = conv_fwd(a["p"], sp["conv_dw_w"][l], row("conv_dw_b", l), row("conv_ln_g", l),
                                    row("conv_ln_b", l))
        a["y_b"] = mm_fwd("mm_conv_out", a["c3"], gw["w_conv_out"], l, False)
        a["h1"], a["hx"], a["mixed"] = mix_out_fwd(a["p"], row("b_gate", l), a["y_a"], a["y_b"], row("b_conv_out", l),
                                                   gw["w_mix_out"], l, h, row("norm_xattn_g", l))
        a["qx"] = mm_fwd("mm_xq", a["hx"], gw["w_xq"], l, False, out_dtype=BF16)
        a["mem_n"] = rms_cast(mem, row("norm_mem_g", l))
        a["kv"] = mm_fwd("mm_xkv", a["mem_n"], gw["w_xkv"], l, True)
        a["ox"] = attn_fwd(a["qx"], a["kv"])
        a["h2"], a["hf"] = mm_fwd("mm_xo", a["ox"], gw["w_xo"], l, False, res=a["h1"], norm_g=row("norm_ffn_g", l))
        a["up"] = mm_fwd("mm_up", a["hf"], gw["w_up"], l, True)
        a["act"] = ffn_act_fwd(a["up"], sp["ffn_dw_w"][l], row("ffn_dw_b", l))
        if l + 1 < n_layers:
            h, u = mm_fwd("mm_down", a["act"], gw["w_down"], l, False, res=a["h2"], norm_g=row("norm_mix_g", l + 1))
        else:
            h = mm_fwd("mm_down", a["act"], gw["w_down"], l, False, res=a["h2"])
        saved.append(a)

    dh, d_final_g, loss = loss_head(h, sp["norm_final_g"][None, :], target)

    big = {}

    def dw(name, key, act, dy, col, l):
        big[key] = mm_dw(name, act, dy, col, l, n_layers, big.get(key))

    small = {n: [None] * n_layers for n in SMALL_REPL + SMALL_SHARDED if n != "norm_final_g"}
    for l in range(n_layers - 1, -1, -1):
        a = saved[l]
        d_act = mm_dx("mm_down_dx", dh, gw["w_down"], l, False)
        dw("mm_down_dw", "w_down", a["act"], dh, False, l)
        d_up, small["ffn_dw_w"][l], small["ffn_dw_b"][l] = ffn_act_bwd(a["up"], sp["ffn_dw_w"][l],
                                                                        row("ffn_dw_b", l), d_act)
        dh, small["norm_ffn_g"][l] = mm_dx("mm_up_dx", d_up, gw["w_up"], l, True,
                                           rms=(a["h2"], row("norm_ffn_g", l), dh))
        dw("mm_up_dw", "w_up", a["hf"], d_up, True, l)
        d_ox = mm_dx("mm_xo_dx", dh, gw["w_xo"], l, False)
        dw("mm_xo_dw", "w_xo", a["ox"], dh, False, l)
        d_qx, d_kv = attn_bwd(a["qx"], a["kv"], d_ox)
        dw("mm_xq_dw", "w_xq", a["hx"], d_qx, False, l)
        d_mem_n = mm_dx("mm_xkv_dx", d_kv, gw["w_xkv"], l, True)
        dw("mm_xkv_dw", "w_xkv", a["mem_n"], d_kv, True, l)
        _, small["norm_mem_g"][l] = rms_bwd(mem, row("norm_mem_g", l), d_mem_n)
        dh, small["norm_xattn_g"][l] = mm_dx("mm_xq_dx", d_qx, gw["w_xq"], l, False,
                                             rms=(a["h1"], row("norm_xattn_g", l), dh))
        d_mixed = mm_dx("mm_mix_out_dx", dh, gw["w_mix_out"], l, False)
        dw("mm_mix_out_dw", "w_mix_out", a["mixed"], dh, False, l)
        d_ya, d_yb, dp_gate, small["b_gate"][l], small["b_conv_out"][l] = gate_mix_bwd(
            a["p"], row("b_gate", l), a["y_a"], a["y_b"], row("b_conv_out", l), d_mixed)
        dw("mm_conv_out_dw", "w_conv_out", a["c3"], d_yb, False, l)
        d_c1, small["conv_ln_g"][l], small["conv_ln_b"][l] = mm_dx(
            "mm_conv_out_dx", d_yb, gw["w_conv_out"], l, False,
            ln=(a["c1"], row("conv_ln_g", l), row("conv_ln_b", l)))
        dp_conv, small["conv_dw_w"][l], small["conv_dw_b"][l] = conv_dw_bwd(a["p"], d_c1, sp["conv_dw_w"][l])
        d_z = mm_dx("mm_ret_out_dx", d_ya, gw["w_ret_out"], l, False)
        dw("mm_ret_out_dw", "w_ret_out", a["z"], d_ya, False, l)
        d_o, dp_gret, small["ret_gn_g"][l] = gn_gate_bwd(a["o"], a["p"], row("ret_gn_g", l), d_z)
        dp_q, dp_k, dp_v = retention_bwd(a["p"], cos, sin, a["states"], d_o, tables)
        dp = jnp.concatenate([dp_q, dp_k, dp_v, dp_gret, dp_conv, dp_gate], axis=1)
        dw("mm_in_dw", "w_in", a["u"], dp, True, l)
        dh, small["norm_mix_g"][l] = mm_dx("mm_in_dx", dp, gw["w_in"], l, True,
                                           rms=(a["h0"], row("norm_mix_g", l), dh))

    small = {n: jnp.stack([g.reshape(sp[n].shape[1:]) for g in v]) for n, v in small.items()}
    small["norm_final_g"] = d_final_g.reshape(-1)
    return loss, dh, big, small


_ANY = pl.BlockSpec(memory_space=pl.ANY)


def _place():
    x, y, c = lax.axis_index("x"), lax.axis_index("y"), lax.axis_index("c")
    return x, y, c


def _other_chips(x, y):
    return [(1 - x, y), (x, 1 - y), (1 - x, 1 - y)]


def place_shard(w, chip):
    lyr, a, b = w.shape
    t = _flat_tile(a, b)

    def body(chip_ref, w_ref, o_ref):
        o_ref[...] = w_ref[...].astype(BF16)

    grid_spec = pltpu.PrefetchScalarGridSpec(
        num_scalar_prefetch=1, grid=(lyr, a // t),
        in_specs=[pl.BlockSpec((None, t, b), lambda l, i, cr: (l, i, 0))],
        out_specs=pl.BlockSpec((None, None, t, b), lambda l, i, cr: (l, cr[0], i, 0)))
    return pl.pallas_call(body, grid_spec=grid_spec, out_shape=_sds((lyr, N_CHIPS, a, b), BF16), name="place_shard",
                          compiler_params=_params(("parallel", "parallel")))(chip, w)


def gather_weights(bufs):
    n = len(bufs)

    def body(*refs):
        outs = refs[n:2 * n]
        ici_send, ici_recv, pair_send, pair_recv = refs[2 * n:]
        x, y, c = _place()
        mine = 2 * x + y
        chips = _other_chips(x, y)

        def part(w, core, slot):
            lh = outs[w].shape[0] // 2
            return outs[w].at[pl.ds(core * lh, lh), slot]

        def over_ici(w, j, slot):
            px, py = chips[j]
            return pltpu.make_async_remote_copy(
                src_ref=part(w, c, slot), dst_ref=part(w, c, slot), send_sem=ici_send.at[w, j],
                recv_sem=ici_recv.at[w, j], device_id=(px, py, c), device_id_type=MESH)

        def to_pair(w, j, core):
            px, py = chips[j]
            return pltpu.make_async_remote_copy(
                src_ref=part(w, core, 2 * px + py), dst_ref=part(w, core, 2 * px + py), send_sem=pair_send.at[w, j],
                recv_sem=pair_recv.at[w, j], device_id=(x, y, 1 - c), device_id_type=MESH)

        for w in range(n):
            for j in range(3):
                over_ici(w, j, mine).start()
        for w in range(n):
            for j, (px, py) in enumerate(chips):
                over_ici(w, j, 2 * px + py).wait_recv()
                to_pair(w, j, c).start()
        for w in range(n):
            for j in range(3):
                to_pair(w, j, 1 - c).wait_recv()
                to_pair(w, j, c).wait_send()
                over_ici(w, j, mine).wait_send()

    return pl.pallas_call(
        body, in_specs=[_ANY] * n, out_specs=[_ANY] * n, out_shape=[_sds(b.shape, b.dtype) for b in bufs],
        input_output_aliases={i: i for i in range(n)},
        scratch_shapes=[pltpu.SemaphoreType.DMA((n, 3))] * 4,
        name="gather_weights", compiler_params=_params())(*bufs)


def pair_exchange(grads):
    n = len(grads)

    def body(*refs):
        ins, outs = refs[:n], refs[n:2 * n]
        send_sems, recv_sems = refs[2 * n:]
        x, y, c = _place()
        cps = []
        for w in range(n):
            lh = ins[w].shape[0] // 2
            cp = pltpu.make_async_remote_copy(
                src_ref=ins[w].at[pl.ds((1 - c) * lh, lh)], dst_ref=outs[w], send_sem=send_sems.at[w],
                recv_sem=recv_sems.at[w], device_id=(x, y, 1 - c), device_id_type=MESH)
            cp.start()
            cps.append(cp)
        for cp in cps:
            cp.wait_send()
            cp.wait_recv()

    out_shape = [_sds((g.shape[0] // 2,) + g.shape[1:], g.dtype) for g in grads]
    return pl.pallas_call(
        body, in_specs=[_ANY] * n, out_specs=[_ANY] * n, out_shape=out_shape,
        scratch_shapes=[pltpu.SemaphoreType.DMA((n,)), pltpu.SemaphoreType.DMA((n,))],
        name="pair_exchange", compiler_params=_params())(*grads)


def chip_exchange(parts):
    n = len(parts)

    def body(*refs):
        ins, outs = refs[:n], refs[n:2 * n]
        send_sems, recv_sems = refs[2 * n:]
        x, y, c = _place()
        mine = 2 * x + y
        chips = _other_chips(x, y)

        def copy(w, j, slot):
            px, py = chips[j]
            return pltpu.make_async_remote_copy(
                src_ref=ins[w].at[:, 2 * px + py], dst_ref=outs[w].at[slot], send_sem=send_sems.at[w, j],
                recv_sem=recv_sems.at[w, j], device_id=(px, py, c), device_id_type=MESH)

        for w in range(n):
            for j in range(3):
                copy(w, j, mine).start()
        for w in range(n):
            for j, (px, py) in enumerate(chips):
                copy(w, j, 2 * px + py).wait_recv()
                copy(w, j, mine).wait_send()

    out_shape = [_sds((N_CHIPS, g.shape[0]) + g.shape[2:], g.dtype) for g in parts]
    return pl.pallas_call(
        body, in_specs=[_ANY] * n, out_specs=[_ANY] * n, out_shape=out_shape,
        scratch_shapes=[pltpu.SemaphoreType.DMA((n, 3)), pltpu.SemaphoreType.DMA((n, 3))],
        name="chip_exchange", compiler_params=_params())(*parts)


def pair_share(halves):
    n = len(halves)

    def body(*refs):
        ins, outs = refs[:n], refs[n:2 * n]
        send_sems, recv_sems = refs[2 * n:]
        x, y, c = _place()
        cps = []
        for w in range(n):
            cp = pltpu.make_async_remote_copy(
                src_ref=ins[w], dst_ref=outs[w], send_sem=send_sems.at[w], recv_sem=recv_sems.at[w],
                device_id=(x, y, 1 - c), device_id_type=MESH)
            cp.start()
            cps.append(cp)
        for cp in cps:
            cp.wait_send()
            cp.wait_recv()

    return pl.pallas_call(
        body, in_specs=[_ANY] * n, out_specs=[_ANY] * n, out_shape=[_sds(g.shape, g.dtype) for g in halves],
        scratch_shapes=[pltpu.SemaphoreType.DMA((n,)), pltpu.SemaphoreType.DMA((n,))],
        name="pair_share", compiler_params=_params())(*halves)


def all_reduce_small(vec):
    r, lanes = vec.shape

    def body(v_ref, o_ref, buf, send_sems, recv_sems):
        x, y, c = _place()
        me = 4 * x + 2 * y + c
        buf[me] = v_ref[...]
        cps = []
        for k in range(1, N_DEV):
            peer = (me + k) % N_DEV
            cp = pltpu.make_async_remote_copy(
                src_ref=v_ref, dst_ref=buf.at[me], send_sem=send_sems.at[k - 1], recv_sem=recv_sems.at[k - 1],
                device_id=(peer // 4, (peer // 2) % 2, peer % 2), device_id_type=MESH)
            cp.start()
            cps.append(cp)
        for k in range(1, N_DEV):
            src = (me + N_DEV - k) % N_DEV
            cps[k - 1].wait_send()
            pltpu.make_async_remote_copy(
                src_ref=v_ref, dst_ref=buf.at[src], send_sem=send_sems.at[k - 1], recv_sem=recv_sems.at[k - 1],
                device_id=(src // 4, (src // 2) % 2, src % 2), device_id_type=MESH).wait_recv()
        acc = buf[0]
        for d in range(1, N_DEV):
            acc = acc + buf[d]
        o_ref[...] = acc

    vm = pl.BlockSpec(memory_space=pltpu.VMEM)
    return pl.pallas_call(
        body, in_specs=[vm], out_specs=vm, out_shape=_sds((r, lanes), F32),
        scratch_shapes=[pltpu.VMEM((N_DEV, r, lanes), F32), pltpu.SemaphoreType.DMA((N_DEV - 1,)),
                        pltpu.SemaphoreType.DMA((N_DEV - 1,))],
        name="all_reduce_small", compiler_params=_params())(vec)


ELEMENTWISE_BLOCK_BYTES = 1 << 20


def _flat_tile(rows, cols):
    for t in (512, 256, 128, 64, 32, 16, 8):
        if rows % t == 0 and t * cols * 4 <= ELEMENTWISE_BLOCK_BYTES:
            return t
    return rows


def add_pair(g, r, half):
    lyr, _, a, b = g.shape
    lh = lyr // 2
    rows = lh * 4 * a
    t = _flat_tile(rows, b)
    nb = rows // t
    g2 = g.reshape(lyr * 4 * a, b)
    r2 = r.reshape(rows, b)

    def body(half_ref, g_ref, r_ref, o_ref):
        o_ref[...] = (g_ref[...] + r_ref[...]).astype(BF16)

    grid_spec = pltpu.PrefetchScalarGridSpec(
        num_scalar_prefetch=1, grid=(nb,),
        in_specs=[pl.BlockSpec((t, b), lambda i, hr: (hr[0] * nb + i, 0)), pl.BlockSpec((t, b), lambda i, hr: (i, 0))],
        out_specs=pl.BlockSpec((t, b), lambda i, hr: (i, 0)))
    out = pl.pallas_call(body, grid_spec=grid_spec, out_shape=_sds((rows, b), BF16), name="add_pair",
                         compiler_params=_params(("parallel",)))(half, g2, r2)
    return out.reshape(lh, 4, a, b)


def sum_chips(own, parts, chip):
    _, lh, a, b = parts.shape
    t = _flat_tile(a, b)

    def body(chip_ref, own_ref, p_ref, o_ref):
        mine = chip_ref[0]

        def term(s):
            return jnp.where(mine == s, own_ref[...], p_ref[s]).astype(F32)

        o_ref[...] = ((term(0) + term(1)) + term(2)) + term(3)

    grid_spec = pltpu.PrefetchScalarGridSpec(
        num_scalar_prefetch=1, grid=(lh, a // t),
        in_specs=[pl.BlockSpec((None, None, t, b), lambda l, i, cr: (l, cr[0], i, 0)),
                  pl.BlockSpec((N_CHIPS, None, t, b), lambda l, i, cr: (0, l, i, 0))],
        out_specs=pl.BlockSpec((None, t, b), lambda l, i, cr: (l, i, 0)))
    return pl.pallas_call(body, grid_spec=grid_spec, out_shape=_sds((lh, a, b), F32), name="sum_chips",
                          compiler_params=_params(("parallel", "parallel")))(chip, own, parts)


def _adamw_math(w, g, m, v):
    mm = ADAM_B1 * m + (1.0 - ADAM_B1) * g
    vv = ADAM_B2 * v + (1.0 - ADAM_B2) * jnp.square(g)
    m_hat = mm / (1.0 - ADAM_B1 ** ADAM_STEP)
    v_hat = vv / (1.0 - ADAM_B2 ** ADAM_STEP)
    return -ADAM_LR * (m_hat / (jnp.sqrt(v_hat) + ADAM_EPS) + ADAM_WD * w), mm, vv


def adamw(w, g, m, v):
    shape = w.shape
    c = shape[-1]
    rows = int(np.prod(shape[:-1])) if len(shape) > 1 else 1
    t = _flat_tile(rows, c)
    flat = lambda z: z.reshape(rows, c)

    def body(w_ref, g_ref, m_ref, v_ref, d_ref, nm_ref, nv_ref):
        d_ref[...], nm_ref[...], nv_ref[...] = _adamw_math(w_ref[...], g_ref[...], m_ref[...], v_ref[...])

    spec = pl.BlockSpec((t, c), lambda i: (i, 0))
    outs = pl.pallas_call(body, grid=(rows // t,), in_specs=[spec] * 4, out_specs=[spec] * 3,
                          out_shape=[_sds((rows, c), F32)] * 3, name="adamw",
                          compiler_params=_params(("parallel",)))(flat(w), flat(g), flat(m), flat(v))
    return tuple(o.reshape(shape) for o in outs)


def adamw_halves(w, g_own, g_other, m, v, core):
    lyr, a, b = w.shape
    lh = lyr // 2
    t = _flat_tile(a, b)

    def body(core_ref, w_ref, go_ref, gs_ref, m_ref, v_ref, g_ref, d_ref, nm_ref, nv_ref):
        own = pl.program_id(0) // lh == core_ref[0]
        g = jnp.where(own, go_ref[...], gs_ref[...])
        g_ref[...] = g
        d_ref[...], nm_ref[...], nv_ref[...] = _adamw_math(w_ref[...], g, m_ref[...], v_ref[...])

    full = pl.BlockSpec((None, t, b), lambda l, i, cr: (l, i, 0))
    own_spec = pl.BlockSpec((None, t, b), lambda l, i, cr: (jnp.clip(l - cr[0] * lh, 0, lh - 1), i, 0))
    other_spec = pl.BlockSpec((None, t, b), lambda l, i, cr: (jnp.clip(l - (1 - cr[0]) * lh, 0, lh - 1), i, 0))
    grid_spec = pltpu.PrefetchScalarGridSpec(
        num_scalar_prefetch=1, grid=(lyr, a // t), in_specs=[full, own_spec, other_spec, full, full],
        out_specs=[full] * 4)
    return pl.pallas_call(body, grid_spec=grid_spec, out_shape=[_sds(w.shape, F32)] * 4, name="adamw_halves",
                          compiler_params=_params(("parallel", "parallel")))(core, w, g_own, g_other, m, v)


def _pack(parts):
    flat = jnp.concatenate([p.reshape(-1) for p in parts])
    pad = (-flat.shape[0]) % 1024
    return jnp.pad(flat, (0, pad)).reshape(-1, 128)


def _unpack(packed, shapes):
    flat = packed.reshape(-1)
    out, off = [], 0
    for shp in shapes:
        size = int(np.prod(shp))
        out.append(flat[off:off + size].reshape(shp))
        off += size
    return out


def kernel(x, mem, positions, norm_mix_g, w_in, b_gate, ret_gn_g, w_ret_out, conv_dw_w, conv_dw_b, conv_ln_g, conv_ln_b, w_conv_out, b_conv_out, w_mix_out, norm_xattn_g, norm_mem_g, w_xq, w_xkv, w_xo, norm_ffn_g, w_up, ffn_dw_w, ffn_dw_b, w_down, norm_final_g, loss_target, m_norm_mix_g, m_w_in, m_b_gate, m_ret_gn_g, m_w_ret_out, m_conv_dw_w, m_conv_dw_b, m_conv_ln_g, m_conv_ln_b, m_w_conv_out, m_b_conv_out, m_w_mix_out, m_norm_xattn_g, m_norm_mem_g, m_w_xq, m_w_xkv, m_w_xo, m_norm_ffn_g, m_w_up, m_ffn_dw_w, m_ffn_dw_b, m_w_down, m_norm_final_g, v_norm_mix_g, v_w_in, v_b_gate, v_ret_gn_g, v_w_ret_out, v_conv_dw_w, v_conv_dw_b, v_conv_ln_g, v_conv_ln_b, v_w_conv_out, v_b_conv_out, v_w_mix_out, v_norm_xattn_g, v_norm_mem_g, v_w_xq, v_w_xkv, v_w_xo, v_norm_ffn_g, v_w_up, v_ffn_dw_w, v_ffn_dw_b, v_w_down, v_norm_final_g):
    args = locals()
    w = {n: args[n] for n in WEIGHTS}
    m = {n: args["m_" + n] for n in WEIGHTS}
    v = {n: args["v_" + n] for n in WEIGHTS}
    chip = 2 * lax.axis_index("x") + lax.axis_index("y")
    core = lax.axis_index("c")

    chip_op = chip.reshape(1).astype(jnp.int32)
    core_op = core.reshape(1).astype(jnp.int32)
    gathered = gather_weights([place_shard(w[n], chip_op) for n in BIG])
    gw = dict(zip(BIG, gathered))

    sp = {n: w[n] for n in SMALL_REPL}
    placed = []
    for n in SMALL_SHARDED:
        cols = w[n].shape[-1]
        full = jnp.zeros(w[n].shape[:-1] + (N_CHIPS * cols,), F32)
        placed.append(lax.dynamic_update_slice_in_dim(full, w[n], chip * cols, axis=2))
    placed_shapes = [p.shape for p in placed]
    gathered_small = all_reduce_small(_pack([jnp.where(core == 0, p, 0.0) for p in placed]))
    for n, arr in zip(SMALL_SHARDED, _unpack(gathered_small, placed_shapes)):
        sp[n] = arr

    loss, grad_x, big, small = local_step(x[0], mem[0], positions.reshape(-1, 1), loss_target[0], gw, sp)

    names = [n for n in SMALL_REPL + SMALL_SHARDED]
    shapes = [small[n].shape for n in names] + [(128,)]
    reduced = _unpack(all_reduce_small(_pack([small[n] for n in names] + [loss.reshape(-1)])), shapes)
    grads = dict(zip(names, reduced[:-1]))
    loss_out = reduced[-1][0]
    for n in SMALL_SHARDED:
        cols = w[n].shape[-1]
        grads[n] = lax.dynamic_slice_in_dim(grads[n], chip * cols, cols, axis=2)

    blist = [big[n] for n in BIG]
    from_pair = pair_exchange(blist)
    pair_sum = [add_pair(g, r, core_op) for g, r in zip(blist, from_pair)]
    from_chips = chip_exchange(pair_sum)
    halves = [sum_chips(own, parts, chip_op) for own, parts in zip(pair_sum, from_chips)]
    other_halves = pair_share(halves)

    delta, new_m, new_v = {}, {}, {}
    for n, g_own, g_other in zip(BIG, halves, other_halves):
        grads[n], delta[n], new_m[n], new_v[n] = adamw_halves(w[n], g_own, g_other, m[n], v[n], core_op)
    for n in WEIGHTS:
        if n not in BIG:
            delta[n], new_m[n], new_v[n] = adamw(w[n], grads[n], m[n], v[n])
    return (loss_out, grad_x[None], *[grads[n] for n in WEIGHTS], *[delta[n] for n in WEIGHTS],
            *[new_m[n] for n in WEIGHTS], *[new_v[n] for n in WEIGHTS])
```

```python
import functools

import jax
import jax.numpy as jnp
import numpy as np
from jax import lax
from jax.experimental import pallas as pl
from jax.experimental.pallas import tpu as pltpu

F32 = jnp.float32
BF16 = jnp.bfloat16
MESH = pl.DeviceIdType.MESH

D_MODEL = 1024
CHUNK = 64
RET_HEADS = 4
RET_QK_DIM = 256
RET_V_DIM = 512
ROPE_THETA = 10000.0
CONV_WIDTH = 31
X_HEADS = 4
X_HEAD_DIM = 256
FFN_DIM = 2816
RMS_EPS = 1e-6
LN_EPS = 1e-5
ADAM_LR = 0.001
ADAM_B1 = 0.9
ADAM_B2 = 0.999
ADAM_EPS = 1e-08
ADAM_WD = 0.01
ADAM_STEP = 10

N_CHIPS = 4
N_DEV = 8
CONV_HALO = 32
FFN_HALO = 8
V7X_VMEM_LIMIT = 56 * 1024 * 1024
ROW_TILE = 256
STRIP_ROWS = 16
STRIP_LANES = 1024
DW_TAPS = 2
FFN_ROWS = 16
FFN_LANES = 256
MM_TILE_M = 1024
RET_TILE = 512
ATTN_TILE = 512
MIX_TILE = 512

BIG = ("w_in", "w_ret_out", "w_conv_out", "w_mix_out", "w_xq", "w_xkv", "w_xo", "w_up", "w_down")
COL_SHARDED = ("w_in", "w_xkv", "w_up")
SMALL_REPL = ("norm_mix_g", "b_gate", "ret_gn_g", "conv_dw_b", "conv_ln_g", "conv_ln_b", "b_conv_out",
              "norm_xattn_g", "norm_mem_g", "norm_ffn_g", "ffn_dw_b", "norm_final_g")
SMALL_SHARDED = ("conv_dw_w", "ffn_dw_w")
WEIGHTS = ('norm_mix_g', 'w_in', 'b_gate', 'ret_gn_g', 'w_ret_out', 'conv_dw_w', 'conv_dw_b', 'conv_ln_g',
           'conv_ln_b', 'w_conv_out', 'b_conv_out', 'w_mix_out', 'norm_xattn_g', 'norm_mem_g', 'w_xq', 'w_xkv',
           'w_xo', 'norm_ffn_g', 'w_up', 'ffn_dw_w', 'ffn_dw_b', 'w_down', 'norm_final_g')


def _params(sem=None):
    return pltpu.CompilerParams(dimension_semantics=sem, vmem_limit_bytes=V7X_VMEM_LIMIT)


def _sds(shape, dtype):
    return jax.ShapeDtypeStruct(tuple(shape), dtype)


def _sigmoid(x):
    return jax.nn.sigmoid(x)


def _dot(a, b, ca, cb):
    return lax.dot_general(a, b, (((ca,), (cb,)), ((), ())), preferred_element_type=F32)


def _nn(a, b):
    return _dot(a, b, 1, 0)


def _nt(a, b):
    return _dot(a, b, 1, 1)


def _tn(a, b):
    return _dot(a, b, 0, 0)


def _mm(name, dims, grid, in_specs, out_spec, out_sds, nk, operands, res=False, norm=False, rms_bwd=False,
        ln_bwd=False):
    n_in = 2 + res + norm + 3 * rms_bwd + 3 * ln_bwd

    def body(*refs):
        ins, outs = refs[:n_in], refs[n_in:]
        a_ref, b_ref = ins[:2]
        extra = list(ins[2:])
        r_ref = extra.pop(0) if res else None
        g_ref = extra.pop(0) if norm else None
        o_ref = outs[0]
        row_tile = pl.program_id(0)
        prod = _dot(a_ref[...].astype(BF16), b_ref[...].astype(BF16), *dims)

        def finish(total):
            if rms_bwd:
                h_ref, gain_ref, dres_ref = extra
                dx, dg = _rms_bwd_math(h_ref[...], gain_ref[...], total)
                o_ref[...] = dx + dres_ref[...]
                _acc_rows(outs[1], row_tile, dg)
                return
            if ln_bwd:
                x_ref, gain_ref, bias_ref = extra
                o_ref[...], dg, db = _ln_silu_bwd_math(x_ref[...], gain_ref[...], bias_ref[...], total)
                _acc_rows(outs[1], row_tile, dg)
                _acc_rows(outs[2], row_tile, db)
                return
            o_ref[...] = total.astype(o_ref.dtype)
            if norm:
                r = lax.rsqrt(jnp.mean(total * total, axis=-1, keepdims=True) + RMS_EPS)
                outs[1][...] = (total * r * g_ref[...]).astype(BF16)

        if nk == 1:
            finish(prod + r_ref[...] if res else prod)
        else:
            k = pl.program_id(2)

            @pl.when(k == 0)
            def _():
                o_ref[...] = (prod + r_ref[...]) if res else prod

            @pl.when((k > 0) & (k < nk - 1))
            def _():
                o_ref[...] += prod

            @pl.when(k == nk - 1)
            def _():
                finish(o_ref[...] + prod)

    assert nk == 1 or out_sds.dtype == F32
    out_specs, out_shape = [out_spec], [out_sds]
    if norm:
        out_specs.append(out_spec)
        out_shape.append(_sds(out_sds.shape, BF16))
    for _ in range(rms_bwd + 2 * ln_bwd):
        n = out_sds.shape[1]
        out_specs.append(pl.BlockSpec((1, n), lambda i, j, k: (0, 0)))
        out_shape.append(_sds((1, n), F32))
    sem = ("arbitrary",) * 3 if (rms_bwd or ln_bwd) else ("parallel", "parallel", "arbitrary")
    out = pl.pallas_call(body, grid=grid, in_specs=in_specs, out_specs=out_specs, out_shape=out_shape, name=name,
                         compiler_params=_params(sem))(*operands)
    return out if (norm or rms_bwd or ln_bwd) else out[0]


def _div_tile(n, want):
    best = None
    for t in range(128, min(n, want) + 1, 128):
        if n % t == 0:
            best = t
    assert best is not None, (n, want)
    return best


def mm_fwd(name, a, g, l, col, out_dtype=F32, res=None, norm_g=None):
    m, k_dim = a.shape
    tm = min(MM_TILE_M, m)
    if col:
        _, _, kk, b = g.shape
        assert kk == k_dim
        tn = _div_tile(b, 1408)
        nps = b // tn
        n = 4 * b
        grid = (m // tm, n // tn, 1)
        in_specs = [pl.BlockSpec((tm, k_dim), lambda i, j, k: (i, 0)),
                    pl.BlockSpec((None, None, k_dim, tn), lambda i, j, k: (l, j // nps, 0, j % nps))]
        nk = 1
        w = g
    else:
        lyr, _, a_rows, n = g.shape
        assert 4 * a_rows == k_dim
        w = g.reshape(lyr, k_dim, n)
        tk = _div_tile(k_dim, 1408)
        tn = n
        nk = k_dim // tk
        grid = (m // tm, 1, nk)
        in_specs = [pl.BlockSpec((tm, tk), lambda i, j, k: (i, k)),
                    pl.BlockSpec((None, tk, tn), lambda i, j, k: (l, k, j))]
    ops = [a, w]
    if res is not None:
        in_specs.append(pl.BlockSpec((tm, tn), lambda i, j, k: (i, j)))
        ops.append(res)
    if norm_g is not None:
        assert tn == n
        in_specs.append(pl.BlockSpec((1, n), lambda i, j, k: (0, 0)))
        ops.append(norm_g)
    return _mm(name, (1, 0), grid, in_specs, pl.BlockSpec((tm, tn), lambda i, j, k: (i, j)), _sds((m, n), out_dtype),
               nk, ops, res=res is not None, norm=norm_g is not None)


def mm_dx(name, dy, g, l, col, rms=None, ln=None):
    m, n = dy.shape
    tm = min(MM_TILE_M, m)
    if col:
        _, _, k_dim, b = g.shape
        assert 4 * b == n
        tk = _div_tile(b, 2560)
        nps = b // tk
        nk = n // tk
        grid = (m // tm, 1, nk)
        in_specs = [pl.BlockSpec((tm, tk), lambda i, j, k: (i, k)),
                    pl.BlockSpec((None, None, k_dim, tk), lambda i, j, k: (l, k // nps, 0, k % nps))]
        out_spec = pl.BlockSpec((tm, k_dim), lambda i, j, k: (i, 0))
        w = g
        tno = k_dim
    else:
        lyr, _, a_rows, nn_ = g.shape
        assert nn_ == n
        k_dim = 4 * a_rows
        w = g.reshape(lyr, k_dim, n)
        tno = _div_tile(k_dim, 1408)
        nk = 1
        grid = (m // tm, k_dim // tno, 1)
        in_specs = [pl.BlockSpec((tm, n), lambda i, j, k: (i, 0)),
                    pl.BlockSpec((None, tno, n), lambda i, j, k: (l, j, 0))]
        out_spec = pl.BlockSpec((tm, tno), lambda i, j, k: (i, j))
    ops = [dy, w]
    if rms is not None:
        assert tno == k_dim
        h, gain, dres = rms
        rows = pl.BlockSpec((tm, k_dim), lambda i, j, k: (i, 0))
        in_specs += [rows, pl.BlockSpec((1, k_dim), lambda i, j, k: (0, 0)), rows]
        ops += [h, gain, dres]
    if ln is not None:
        assert tno == k_dim
        vec = pl.BlockSpec((1, k_dim), lambda i, j, k: (0, 0))
        in_specs += [pl.BlockSpec((tm, k_dim), lambda i, j, k: (i, 0)), vec, vec]
        ops += list(ln)
    return _mm(name, (1, 1), grid, in_specs, out_spec, _sds((m, k_dim), F32), nk, ops, rms_bwd=rms is not None,
               ln_bwd=ln is not None)


def mm_dw(name, a, dy, col, l, n_layers, into=None):
    m, k_dim = a.shape
    _, n = dy.shape
    ts = min(MM_TILE_M, m)
    ns = m // ts
    tko = _div_tile(k_dim, 1408)
    if col:
        b = n // 4
        tn = _div_tile(b, 1408)
        nps = b // tn
        grid = (k_dim // tko, n // tn, ns)
        out_spec = pl.BlockSpec((None, None, tko, tn), lambda i, j, s: (l, j // nps, i, j % nps))
        shape = (n_layers, 4, k_dim, b)
    else:
        tn = n
        grid = (k_dim // tko, 1, ns)
        out_spec = pl.BlockSpec((None, tko, tn), lambda i, j, s: (l, i, j))
        shape = (n_layers, k_dim, n)
    in_specs = [pl.BlockSpec((ts, tko), lambda i, j, s: (s, i)),
                pl.BlockSpec((ts, tn), lambda i, j, s: (s, j))]
    ops = [a, dy]
    aliases = {}
    if into is not None:
        in_specs.append(_ANY)
        ops.append(into.reshape(shape))
        aliases = {2: 0}

    def body(a_ref, b_ref, *rest):
        o_ref = rest[-1]
        prod = _tn(a_ref[...].astype(BF16), b_ref[...].astype(BF16))
        if ns == 1:
            o_ref[...] = prod
        else:
            s = pl.program_id(2)

            @pl.when(s == 0)
            def _():
                o_ref[...] = prod

            @pl.when(s > 0)
            def _():
                o_ref[...] += prod

    out = pl.pallas_call(body, grid=grid, in_specs=in_specs, out_specs=out_spec, out_shape=_sds(shape, F32),
                         input_output_aliases=aliases, name=name,
                         compiler_params=_params(("parallel", "parallel", "arbitrary")))(*ops)
    return out.reshape(n_layers, 4, k_dim if col else k_dim // 4, shape[-1])


def _row_spec(t, c, col=0):
    return pl.BlockSpec((t, c), lambda i: (i, col))


def _vec_spec(c):
    return pl.BlockSpec((1, c), lambda i: (0, 0))


def _acc_rows(ref, i, val):
    @pl.when(i == 0)
    def _():
        ref[...] = val

    @pl.when(i > 0)
    def _():
        ref[...] += val


def rope_tables(positions, inv_freq):
    s = positions.shape[0]
    t = min(ROW_TILE, s)
    half = inv_freq.shape[1]

    def body(p_ref, f_ref, c_ref, s_ref):
        ang = p_ref[...].astype(F32) * f_ref[...]
        c_ref[...] = jnp.cos(ang)
        s_ref[...] = jnp.sin(ang)

    return pl.pallas_call(
        body, grid=(s // t,), in_specs=[_row_spec(t, 1), _vec_spec(half)],
        out_specs=[_row_spec(t, half), _row_spec(t, half)], out_shape=[_sds((s, half), F32)] * 2, name="rope_tables",
        compiler_params=_params(("parallel",)))(positions, inv_freq)


def rms_cast(h, g):
    s, d = h.shape
    t = min(ROW_TILE, s)

    def body(h_ref, g_ref, o_ref):
        x = h_ref[...]
        r = lax.rsqrt(jnp.mean(x * x, axis=-1, keepdims=True) + RMS_EPS)
        o_ref[...] = (x * r * g_ref[...]).astype(BF16)

    return pl.pallas_call(body, grid=(s // t,), in_specs=[_row_spec(t, d), _vec_spec(d)], out_specs=_row_spec(t, d),
                          out_shape=_sds((s, d), BF16), name="rms_cast", compiler_params=_params(("parallel",)))(h, g)


def _rms_bwd_math(x, g, du):
    r = lax.rsqrt(jnp.mean(x * x, axis=-1, keepdims=True) + RMS_EPS)
    gd = g * du
    dx = r * gd - x * (r * r * r) * jnp.mean(x * gd, axis=-1, keepdims=True)
    dg = jnp.sum(x * r * du, axis=0, keepdims=True)
    return dx, dg


def rms_bwd(h, g, du, dres=None):
    s, d = h.shape
    t = min(ROW_TILE, s)

    def body(*refs):
        if dres is None:
            h_ref, g_ref, du_ref, dh_ref, dg_ref = refs
        else:
            h_ref, g_ref, du_ref, dr_ref, dh_ref, dg_ref = refs
        dx, dg = _rms_bwd_math(h_ref[...], g_ref[...], du_ref[...])
        if dres is not None:
            dx = dx + dr_ref[...]
        dh_ref[...] = dx
        _acc_rows(dg_ref, pl.program_id(0), dg)

    in_specs = [_row_spec(t, d), _vec_spec(d), _row_spec(t, d)]
    ops = [h, g, du]
    if dres is not None:
        in_specs.append(_row_spec(t, d))
        ops.append(dres)
    return pl.pallas_call(body, grid=(s // t,), in_specs=in_specs, out_specs=[_row_spec(t, d), _vec_spec(d)],
                          out_shape=[_sds((s, d), F32), _sds((1, d), F32)], name="rms_bwd",
                          compiler_params=_params(("arbitrary",)))(*ops)


def loss_head(h, g, target):
    s, d = h.shape
    t = min(ROW_TILE, s)

    def body(h_ref, g_ref, t_ref, dh_ref, dg_ref, loss_ref):
        x = h_ref[...]
        gg = g_ref[...]
        r = lax.rsqrt(jnp.mean(x * x, axis=-1, keepdims=True) + RMS_EPS)
        err = x * r * gg - t_ref[...]
        part = 0.5 * jnp.sum(jnp.mean(err * err, axis=-1, keepdims=True), axis=0, keepdims=True)
        dy = err * (1.0 / d)
        dx, dg = _rms_bwd_math(x, gg, dy)
        dh_ref[...] = dx
        i = pl.program_id(0)
        _acc_rows(dg_ref, i, dg)
        _acc_rows(loss_ref, i, jnp.broadcast_to(part, (1, 128)))

    return pl.pallas_call(
        body, grid=(s // t,), in_specs=[_row_spec(t, d), _vec_spec(d), _row_spec(t, d)],
        out_specs=[_row_spec(t, d), _vec_spec(d), _vec_spec(128)],
        out_shape=[_sds((s, d), F32), _sds((1, d), F32), _sds((1, 128), F32)], name="loss_head",
        compiler_params=_params(("arbitrary",)))(h, g, target)


def _rot(x, cos, sin):
    half = x.shape[-1] // 2
    x1, x2 = x[:, :half], x[:, half:]
    return jnp.concatenate([x1 * cos - x2 * sin, x2 * cos + x1 * sin], axis=-1)


def _rot_t(dy, cos, sin):
    half = dy.shape[-1] // 2
    d1, d2 = dy[:, :half], dy[:, half:]
    return jnp.concatenate([d1 * cos + d2 * sin, d2 * cos - d1 * sin], axis=-1)


def _decay_tables(t):
    log_gamma = jnp.log(1.0 - jnp.power(2.0, -5.0 - jnp.arange(RET_HEADS, dtype=F32)))
    idx = jnp.arange(t, dtype=F32)
    dist = jnp.abs(idx[:, None] - idx[None, :])
    chunk = jnp.arange(t) // CHUNK
    seen = chunk[None, :] <= chunk[:, None]
    d_tile = jnp.where(seen[None], jnp.exp(log_gamma[:, None, None] * dist), 0.0)
    decay_q = jnp.exp(log_gamma[:, None] * (idx[None, :] + 1.0))[:, :, None]
    decay_k = jnp.exp(log_gamma[:, None] * (t - 1.0 - idx[None, :]))[:, :, None]
    decay_tile = jnp.exp(log_gamma * t)[:, None, None]
    return d_tile, decay_q, decay_k, decay_tile


_QK_SCALE = RET_QK_DIM ** -0.5


def _retention_specs(t, nt, order):
    dk, dv = RET_QK_DIM, RET_V_DIM
    hmap = lambda h, i: (h, 0, 0)
    qkv = [pl.BlockSpec((t, dk), lambda h, i: (order(i), h)),
           pl.BlockSpec((t, dk), lambda h, i: (order(i), RET_HEADS + h)),
           pl.BlockSpec((t, dv), lambda h, i: (order(i), 4 + h))]
    rope = [pl.BlockSpec((t, dk // 2), lambda h, i: (order(i), 0))] * 2
    tables = [pl.BlockSpec((None, t, t), hmap), pl.BlockSpec((None, t, 1), hmap), pl.BlockSpec((None, t, 1), hmap),
              pl.BlockSpec((None, 1, 1), hmap)]
    return qkv, rope, tables


def retention_fwd(p, cos, sin, gn_g, tables):
    s = p.shape[0]
    t = min(RET_TILE, s)
    nt = s // t
    dk, dv = RET_QK_DIM, RET_V_DIM

    def body(q_ref, k_ref, v_ref, cos_ref, sin_ref, di_ref, dq_ref, dkk_ref, dc_ref, gr_ref, gn_ref,
             o_ref, z_ref, st_ref, state):
        @pl.when(pl.program_id(1) == 0)
        def _():
            state[...] = jnp.zeros_like(state)

        cs, sn = cos_ref[...], sin_ref[...]
        qb = (_rot(q_ref[...], cs, sn) * _QK_SCALE).astype(BF16)
        kr = _rot(k_ref[...], cs, sn)
        vb = v_ref[...].astype(BF16)
        st = state[...].astype(BF16)
        st_ref[...] = st
        scores = _nt(qb, kr.astype(BF16)) * di_ref[...]
        o = _nn(scores.astype(BF16), vb) + _nn(qb, st) * dq_ref[...]
        state[...] = state[...] * dc_ref[...] + _tn((kr * dkk_ref[...]).astype(BF16), vb)
        o_ref[...] = o
        mu = jnp.mean(o, axis=-1, keepdims=True)
        oc = o - mu
        var = jnp.mean(oc * oc, axis=-1, keepdims=True)
        y = oc * lax.rsqrt(var + LN_EPS) * gn_ref[...]
        gr = gr_ref[...]
        z_ref[...] = (gr * _sigmoid(gr) * y).astype(BF16)

    qkv, rope, tabs = _retention_specs(t, nt, lambda i: i)
    in_specs = qkv + rope + tabs + [pl.BlockSpec((t, dv), lambda h, i: (i, 8 + h)),
                                    pl.BlockSpec((1, dv), lambda h, i: (0, h))]
    out_specs = [pl.BlockSpec((t, dv), lambda h, i: (i, h)),
                 pl.BlockSpec((t, dv), lambda h, i: (i, h)),
                 pl.BlockSpec((None, None, dk, dv), lambda h, i: (h, i, 0, 0))]
    out_shape = [_sds((s, RET_HEADS * dv), F32), _sds((s, RET_HEADS * dv), BF16), _sds((RET_HEADS, nt, dk, dv), BF16)]
    return pl.pallas_call(
        body, grid=(RET_HEADS, nt), in_specs=in_specs, out_specs=out_specs, out_shape=out_shape,
        scratch_shapes=[pltpu.VMEM((dk, dv), F32)], name="retention_fwd",
        compiler_params=_params(("parallel", "arbitrary")))(p, p, p, cos, sin, *tables, p, gn_g)


def gn_gate_bwd(o, p, gn_g, dz):
    s = o.shape[0]
    t = min(ROW_TILE, s)
    dv = RET_V_DIM
    w = RET_HEADS * dv

    def body(o_ref, gr_ref, gn_ref, dz_ref, do_ref, dgr_ref, dgn_ref):
        dgn_parts = []
        for h in range(RET_HEADS):
            sl = slice(h * dv, (h + 1) * dv)
            oo = o_ref[:, sl]
            gr = gr_ref[:, sl]
            dz = dz_ref[:, sl]
            gn = gn_ref[:, sl]
            mu = jnp.mean(oo, axis=-1, keepdims=True)
            oc = oo - mu
            rstd = lax.rsqrt(jnp.mean(oc * oc, axis=-1, keepdims=True) + LN_EPS)
            y = oc * rstd
            sg = _sigmoid(gr)
            act = gr * sg
            dyg = dz * act
            dgn_parts.append(jnp.sum(dyg * y, axis=0, keepdims=True))
            dy = dyg * gn
            do_ref[:, sl] = rstd * (dy - jnp.mean(dy, axis=-1, keepdims=True)
                                    - y * jnp.mean(dy * y, axis=-1, keepdims=True))
            dgr_ref[:, sl] = (dz * (y * gn) * (sg * (1.0 + gr * (1.0 - sg)))).astype(BF16)
        _acc_rows(dgn_ref, pl.program_id(0), jnp.concatenate(dgn_parts, axis=-1))

    return pl.pallas_call(
        body, grid=(s // t,), in_specs=[_row_spec(t, w), _row_spec(t, w, 2), _vec_spec(w), _row_spec(t, w)],
        out_specs=[_row_spec(t, w), _row_spec(t, w), _vec_spec(w)],
        out_shape=[_sds((s, w), F32), _sds((s, w), BF16), _sds((1, w), F32)], name="gn_gate_bwd",
        compiler_params=_params(("arbitrary",)))(o, p, gn_g, dz)


def retention_bwd(p, cos, sin, states, do, tables):
    s = p.shape[0]
    t = min(RET_TILE, s)
    nt = s // t
    dk, dv = RET_QK_DIM, RET_V_DIM

    def body(q_ref, k_ref, v_ref, cos_ref, sin_ref, di_ref, dq_ref, dkk_ref, dc_ref, st_ref, do_ref,
             gq_ref, gk_ref, gv_ref, dstate):
        @pl.when(pl.program_id(1) == 0)
        def _():
            dstate[...] = jnp.zeros_like(dstate)

        cs, sn = cos_ref[...], sin_ref[...]
        qb = (_rot(q_ref[...], cs, sn) * _QK_SCALE).astype(BF16)
        kr = _rot(k_ref[...], cs, sn)
        kb = kr.astype(BF16)
        vb = v_ref[...].astype(BF16)
        dmat, dkk = di_ref[...], dkk_ref[...]
        d_o = do_ref[...]
        dob = d_o.astype(BF16)
        dsb = dstate[...].astype(BF16)
        ab = (_nt(qb, kb) * dmat).astype(BF16)
        gv_ref[...] = (_tn(ab, dob) + _nn((kr * dkk).astype(BF16), dsb)).astype(BF16)
        dcb = (d_o * dq_ref[...]).astype(BF16)
        dpb = (_nt(dob, vb) * dmat).astype(BF16)
        dqq = _nt(dcb, st_ref[...]) + _nn(dpb, kb)
        dkv = _tn(dpb, qb) + _nt(vb, dsb) * dkk
        dstate[...] = dstate[...] * dc_ref[...] + _tn(qb, dcb)
        gq_ref[...] = _rot_t(dqq * _QK_SCALE, cs, sn).astype(BF16)
        gk_ref[...] = _rot_t(dkv, cs, sn).astype(BF16)

    rev = lambda i: nt - 1 - i
    qkv, rope, tabs = _retention_specs(t, nt, rev)
    in_specs = qkv + rope + tabs + [pl.BlockSpec((None, None, dk, dv), lambda h, i: (h, rev(i), 0, 0)),
                                    pl.BlockSpec((t, dv), lambda h, i: (rev(i), h))]
    out_specs = [pl.BlockSpec((t, dk), lambda h, i: (rev(i), h)),
                 pl.BlockSpec((t, dk), lambda h, i: (rev(i), h)),
                 pl.BlockSpec((t, dv), lambda h, i: (rev(i), h))]
    out_shape = [_sds((s, RET_HEADS * dk), BF16), _sds((s, RET_HEADS * dk), BF16), _sds((s, RET_HEADS * dv), BF16)]
    return pl.pallas_call(
        body, grid=(RET_HEADS, nt), in_specs=in_specs, out_specs=out_specs, out_shape=out_shape,
        scratch_shapes=[pltpu.VMEM((dk, dv), F32)], name="retention_bwd",
        compiler_params=_params(("parallel", "arbitrary")))(p, p, p, cos, sin, *tables, states, do)


A_COL, B_COL = 6, 7


def _prev_rows_spec(t, halo, width, col):
    per = t // halo
    return pl.BlockSpec((halo, width), lambda i: (jnp.maximum(i * per - 1, 0), col))


def _next_rows_spec(t, halo, width, col, n_rows):
    per = t // halo
    last = n_rows // halo - 1
    return pl.BlockSpec((halo, width), lambda i: (jnp.minimum((i + 1) * per, last), col))


def _shifted_copies(ext, rows):
    for b in range(1, 8):
        ext[b, pl.ds(0, rows - 8), :] = ext[0, pl.ds(b, rows - 8), :]


def _shifted(ext, start, lanes):
    return ext[start % 8, pl.ds(start - start % 8, STRIP_ROWS), lanes]


def conv_fwd(p, dw_w, dw_b, ln_g, ln_b):
    s = p.shape[0]
    t = min(ROW_TILE, s)
    c = D_MODEL
    hl = CONV_HALO

    def body(a_ref, b_ref, ah_ref, bh_ref, w_ref, wb_ref, g_ref, bb_ref, c1_ref, c3_ref, ext):
        i = pl.program_id(0)
        ext[0, pl.ds(0, hl), :] = jnp.where(i > 0, ah_ref[...] * _sigmoid(bh_ref[...]), 0.0)
        ext[0, pl.ds(hl, t), :] = a_ref[...] * _sigmoid(b_ref[...])
        _shifted_copies(ext, t + hl)
        assert STRIP_LANES == c
        first = hl - (CONV_WIDTH - 1)
        ls = slice(0, c)
        for r0 in range(0, t, STRIP_ROWS):
            accs = [jnp.broadcast_to(wb_ref[...], (STRIP_ROWS, c)), jnp.zeros((STRIP_ROWS, c), F32)]
            for j in range(CONV_WIDTH):
                accs[j % 2] = accs[j % 2] + w_ref[j:j + 1, :] * _shifted(ext, r0 + first + j, ls)
            acc = accs[0] + accs[1]
            c1_ref[r0:r0 + STRIP_ROWS, :] = acc
            mu = jnp.mean(acc, axis=-1, keepdims=True)
            xc = acc - mu
            var = jnp.mean(xc * xc, axis=-1, keepdims=True)
            c2 = xc * lax.rsqrt(var + LN_EPS) * g_ref[...] + bb_ref[...]
            c3_ref[r0:r0 + STRIP_ROWS, :] = (c2 * _sigmoid(c2)).astype(BF16)

    in_specs = [_row_spec(t, c, A_COL), _row_spec(t, c, B_COL),
                _prev_rows_spec(t, hl, c, A_COL), _prev_rows_spec(t, hl, c, B_COL),
                pl.BlockSpec((CONV_WIDTH, c), lambda i: (0, 0)), _vec_spec(c), _vec_spec(c), _vec_spec(c)]
    return pl.pallas_call(
        body, grid=(s // t,), in_specs=in_specs, out_specs=[_row_spec(t, c), _row_spec(t, c)],
        out_shape=[_sds((s, c), F32), _sds((s, c), BF16)], scratch_shapes=[pltpu.VMEM((8, t + hl, c), F32)],
        name="conv_fwd", compiler_params=_params(("parallel",)))(p, p, p, p, dw_w, dw_b, ln_g, ln_b)


def _ln_silu_bwd_math(x, g, b, d):
    mu = jnp.mean(x, axis=-1, keepdims=True)
    xc = x - mu
    rstd = lax.rsqrt(jnp.mean(xc * xc, axis=-1, keepdims=True) + LN_EPS)
    y = xc * rstd
    c2 = y * g + b
    sg = _sigmoid(c2)
    dc2 = d * (sg * (1.0 + c2 * (1.0 - sg)))
    dy = dc2 * g
    dx = rstd * (dy - jnp.mean(dy, axis=-1, keepdims=True) - y * jnp.mean(dy * y, axis=-1, keepdims=True))
    return dx, jnp.sum(dc2 * y, axis=0, keepdims=True), jnp.sum(dc2, axis=0, keepdims=True)


def conv_dw_bwd(p, dc1, dw_w):
    s = p.shape[0]
    t = min(ROW_TILE, s)
    c = D_MODEL
    hl = CONV_HALO
    nt = s // t

    def body(a_ref, b_ref, ah_ref, bh_ref, d_ref, dn_ref, w_ref, dab_ref, dw_ref, dbias_ref, ext_c, ext_d, dw_s):
        i = pl.program_id(0)
        ext_c[0, pl.ds(0, hl), :] = jnp.where(i > 0, ah_ref[...] * _sigmoid(bh_ref[...]), 0.0)
        ext_c[0, pl.ds(hl, t), :] = a_ref[...] * _sigmoid(b_ref[...])
        ext_d[0, pl.ds(0, t), :] = d_ref[...]
        ext_d[0, pl.ds(t, hl), :] = jnp.where(i < nt - 1, dn_ref[...], 0.0)
        _shifted_copies(ext_c, t + hl)
        _shifted_copies(ext_d, t + hl)
        first = hl - (CONV_WIDTH - 1)

        def fold8(x):
            rows = [x[k:k + 8] for k in range(0, STRIP_ROWS, 8)]
            while len(rows) > 1:
                rows = [rows[k] + rows[k + 1] for k in range(0, len(rows), 2)]
            return rows[0]

        for lane in range(0, c, STRIP_LANES):
            ls = slice(lane, lane + STRIP_LANES)
            for r0 in range(0, t, STRIP_ROWS):
                accs = [jnp.zeros((STRIP_ROWS, STRIP_LANES), F32) for _ in range(2)]
                for j in range(CONV_WIDTH):
                    accs[j % 2] = accs[j % 2] + w_ref[j:j + 1, ls] * _shifted(ext_d, r0 + CONV_WIDTH - 1 - j, ls)
                dc0 = accs[0] + accs[1]
                rs = slice(r0, r0 + STRIP_ROWS)
                sb = _sigmoid(b_ref[rs, ls])
                dab_ref[rs, ls] = (dc0 * sb).astype(BF16)
                dab_ref[rs, slice(c + lane, c + lane + STRIP_LANES)] = (
                    dc0 * a_ref[rs, ls] * sb * (1.0 - sb)).astype(BF16)
            for j0 in range(0, CONV_WIDTH, DW_TAPS):
                taps = range(j0, min(j0 + DW_TAPS, CONV_WIDTH))
                parts = [jnp.zeros((8, STRIP_LANES), F32) for _ in taps]
                for r0 in range(0, t, STRIP_ROWS):
                    d = ext_d[0, r0:r0 + STRIP_ROWS, ls]
                    for k, j in enumerate(taps):
                        parts[k] = parts[k] + fold8(d * _shifted(ext_c, r0 + first + j, ls))
                for k, j in enumerate(taps):
                    dw_s[j:j + 1, ls] = jnp.sum(parts[k], axis=0, keepdims=True)
        _acc_rows(dw_ref, i, dw_s[0:CONV_WIDTH, :])
        d = d_ref[...]
        _acc_rows(dbias_ref, i, jnp.sum(d, axis=0, keepdims=True))

    in_specs = [_row_spec(t, c, A_COL), _row_spec(t, c, B_COL),
                _prev_rows_spec(t, hl, c, A_COL), _prev_rows_spec(t, hl, c, B_COL),
                _row_spec(t, c), _next_rows_spec(t, hl, c, 0, s),
                pl.BlockSpec((CONV_WIDTH, c), lambda i: (0, 0))]
    return pl.pallas_call(
        body, grid=(nt,), in_specs=in_specs,
        out_specs=[_row_spec(t, 2 * c), pl.BlockSpec((CONV_WIDTH, c), lambda i: (0, 0)), _vec_spec(c)],
        out_shape=[_sds((s, 2 * c), BF16), _sds((CONV_WIDTH, c), F32), _sds((1, c), F32)],
        scratch_shapes=[pltpu.VMEM((8, t + hl, c), F32), pltpu.VMEM((8, t + hl, c), F32),
                        pltpu.VMEM((CONV_HALO, c), F32)], name="conv_dw_bwd",
        compiler_params=_params(("arbitrary",)))(p, p, p, p, dc1, dc1, dw_w)


GATE_COL = 4


def mix_out_fwd(p, b_gate, y_a, y_b, b_conv_out, g, l, res, norm_g):
    s = p.shape[0]
    c = D_MODEL
    t = min(MIX_TILE, s)
    lyr = g.shape[0]
    w = g.reshape(lyr, c, c)

    def body(gt_ref, bg_ref, ya_ref, yb_ref, bc_ref, w_ref, r_ref, ng_ref, h_ref, u_ref, mx_ref):
        gs = _sigmoid(gt_ref[...] + bg_ref[...])
        mixed = (gs[:, :c] * ya_ref[...] + gs[:, c:] * (yb_ref[...] + bc_ref[...])).astype(BF16)
        mx_ref[...] = mixed
        total = _nn(mixed, w_ref[...]) + r_ref[...]
        h_ref[...] = total
        r = lax.rsqrt(jnp.mean(total * total, axis=-1, keepdims=True) + RMS_EPS)
        u_ref[...] = (total * r * ng_ref[...]).astype(BF16)

    rows = _row_spec(t, c)
    return pl.pallas_call(
        body, grid=(s // t,),
        in_specs=[_row_spec(t, 2 * c, GATE_COL), _vec_spec(2 * c), rows, rows, _vec_spec(c),
                  pl.BlockSpec((None, c, c), lambda i: (l, 0, 0)), rows, _vec_spec(c)],
        out_specs=[rows, rows, rows], out_shape=[_sds((s, c), F32), _sds((s, c), BF16), _sds((s, c), BF16)],
        name="mix_out_fwd", compiler_params=_params(("parallel",)))(p, b_gate, y_a, y_b, b_conv_out, w, res, norm_g)


def gate_mix_bwd(p, b_gate, y_a, y_b, b_conv_out, dmix):
    s = p.shape[0]
    t = min(ROW_TILE, s)
    c = D_MODEL

    def body(gt_ref, bg_ref, ya_ref, yb_ref, bc_ref, d_ref, dya_ref, dyb_ref, dgt_ref, dbg_ref, dbc_ref):
        gs = _sigmoid(gt_ref[...] + bg_ref[...])
        ga, gb = gs[:, :c], gs[:, c:]
        d = d_ref[...]
        dya = ga * d
        dyb = gb * d
        dya_ref[...] = dya.astype(BF16)
        dyb_ref[...] = dyb.astype(BF16)
        dga = d * ya_ref[...] * ga * (1.0 - ga)
        dgb = d * (yb_ref[...] + bc_ref[...]) * gb * (1.0 - gb)
        dgt_ref[:, :c] = dga.astype(BF16)
        dgt_ref[:, c:] = dgb.astype(BF16)
        i = pl.program_id(0)
        _acc_rows(dbg_ref, i, jnp.concatenate([jnp.sum(dga, axis=0, keepdims=True),
                                               jnp.sum(dgb, axis=0, keepdims=True)], axis=-1))
        _acc_rows(dbc_ref, i, jnp.sum(dyb, axis=0, keepdims=True))

    return pl.pallas_call(
        body, grid=(s // t,),
        in_specs=[_row_spec(t, 2 * c, GATE_COL), _vec_spec(2 * c), _row_spec(t, c), _row_spec(t, c), _vec_spec(c),
                  _row_spec(t, c)],
        out_specs=[_row_spec(t, c), _row_spec(t, c), _row_spec(t, 2 * c), _vec_spec(2 * c), _vec_spec(c)],
        out_shape=[_sds((s, c), BF16), _sds((s, c), BF16), _sds((s, 2 * c), BF16), _sds((1, 2 * c), F32),
                   _sds((1, c), F32)],
        name="gate_mix_bwd", compiler_params=_params(("arbitrary",)))(p, b_gate, y_a, y_b, b_conv_out, dmix)


_X_SCALE = X_HEAD_DIM ** -0.5


def _softmax_rows(sc):
    m = jnp.max(sc, axis=-1, keepdims=True)
    e = jnp.exp(sc - m)
    return e / jnp.sum(e, axis=-1, keepdims=True)


def attn_fwd(qx, kv):
    s, d = qx.shape
    m = kv.shape[0]
    t = min(ATTN_TILE, s)
    hd = X_HEAD_DIM

    def body(q_ref, kv_ref, o_ref):
        for h in range(X_HEADS):
            sl = slice(h * hd, (h + 1) * hd)
            kh = kv_ref[:, sl].astype(BF16)
            vh = kv_ref[:, d + h * hd:d + (h + 1) * hd].astype(BF16)
            pr = _softmax_rows(_nt(q_ref[:, sl], kh) * _X_SCALE)
            o_ref[:, sl] = _nn(pr.astype(BF16), vh).astype(BF16)

    return pl.pallas_call(
        body, grid=(s // t,), in_specs=[_row_spec(t, d), pl.BlockSpec((m, 2 * d), lambda i: (0, 0))],
        out_specs=_row_spec(t, d), out_shape=_sds((s, d), BF16), name="attn_fwd",
        compiler_params=_params(("parallel",)))(qx, kv)


def attn_bwd(qx, kv, dox):
    s, d = qx.shape
    m = kv.shape[0]
    t = min(ATTN_TILE, s)
    hd = X_HEAD_DIM

    def body(q_ref, kv_ref, do_ref, dq_ref, dkv_ref):
        dks, dvs = [], []
        for h in range(X_HEADS):
            sl = slice(h * hd, (h + 1) * hd)
            qh = q_ref[:, sl]
            kh = kv_ref[:, sl].astype(BF16)
            vh = kv_ref[:, d + h * hd:d + (h + 1) * hd].astype(BF16)
            pr = _softmax_rows(_nt(qh, kh) * _X_SCALE)
            doh = do_ref[:, sl].astype(BF16)
            dpr = _nt(doh, vh)
            dvs.append(_tn(pr.astype(BF16), doh))
            ds = pr * (dpr - jnp.sum(dpr * pr, axis=-1, keepdims=True))
            dsb = (ds * _X_SCALE).astype(BF16)
            dq_ref[:, sl] = _nn(dsb, kh).astype(BF16)
            dks.append(_tn(dsb, qh))
        _acc_rows(dkv_ref, pl.program_id(0), jnp.concatenate(dks + dvs, axis=-1))

    return pl.pallas_call(
        body, grid=(s // t,),
        in_specs=[_row_spec(t, d), pl.BlockSpec((m, 2 * d), lambda i: (0, 0)), _row_spec(t, d)],
        out_specs=[_row_spec(t, d), pl.BlockSpec((m, 2 * d), lambda i: (0, 0))],
        out_shape=[_sds((s, d), BF16), _sds((m, 2 * d), F32)], name="attn_bwd",
        compiler_params=_params(("arbitrary",)))(qx, kv, dox)


def _offset_copies(ext, offsets, rows):
    for k, off in enumerate(offsets):
        ext[1 + k, pl.ds(0, rows), :] = ext[0, pl.ds(off, rows), :]


def _ffn_blocks(rows, lanes):
    return [(r0, slice(l0, l0 + FFN_LANES)) for r0 in range(0, rows, FFN_ROWS) for l0 in range(0, lanes, FFN_LANES)]


def ffn_act_fwd(up, dw_w, dw_b):
    s = up.shape[0]
    f = FFN_DIM
    t = min(ROW_TILE, s)
    hl = FFN_HALO

    def body(val_ref, gt_ref, gh_ref, w_ref, b_ref, o_ref, ext):
        i = pl.program_id(0)
        ext[0, pl.ds(0, hl), :] = jnp.where(i > 0, gh_ref[...], 0.0)
        ext[0, pl.ds(hl, t), :] = gt_ref[...]
        _offset_copies(ext, (hl - 2, hl - 1), t)
        for r0, ls in _ffn_blocks(t, f):
            gc = b_ref[:, ls] + w_ref[0:1, ls] * ext[1, pl.ds(r0, FFN_ROWS), ls] \
                + w_ref[1:2, ls] * ext[2, pl.ds(r0, FFN_ROWS), ls] + w_ref[2:3, ls] * ext[0, pl.ds(r0 + hl, FFN_ROWS), ls]
            o_ref[r0:r0 + FFN_ROWS, ls] = (gc * _sigmoid(gc) * val_ref[r0:r0 + FFN_ROWS, ls]).astype(BF16)

    return pl.pallas_call(
        body, grid=(s // t,),
        in_specs=[_row_spec(t, f, 0), _row_spec(t, f, 1), _prev_rows_spec(t, hl, f, 1),
                  pl.BlockSpec((3, f), lambda i: (0, 0)), _vec_spec(f)],
        out_specs=_row_spec(t, f), out_shape=_sds((s, f), BF16), scratch_shapes=[pltpu.VMEM((3, t + hl, f), F32)],
        name="ffn_act_fwd", compiler_params=_params(("parallel",)))(up, up, up, dw_w, dw_b)


def ffn_act_bwd(up, dw_w, dw_b, da):
    s = up.shape[0]
    f = FFN_DIM
    t = min(ROW_TILE, s)
    hl = FFN_HALO
    nt = s // t

    def body(val_ref, valn_ref, gt_ref, gp_ref, gn_ref, da_ref, dan_ref, w_ref, b_ref,
             dup_ref, dw_ref, db_ref, ext_g, ext_d, sums):
        i = pl.program_id(0)
        ext_g[0, pl.ds(0, hl), :] = jnp.where(i > 0, gp_ref[...], 0.0)
        ext_g[0, pl.ds(hl, t), :] = gt_ref[...]
        ext_g[0, pl.ds(hl + t, hl), :] = gn_ref[...]
        _offset_copies(ext_g, (hl - 2, hl - 1), t + hl)
        sums[...] = jnp.zeros_like(sums)

        def fold8(x):
            out = x[0:8]
            for k in range(8, x.shape[0], 8):
                out = out + x[k:k + 8]
            return out

        def gate_block(rows, off, ls, val, da_rows):
            taps = [ext_g[1, pl.ds(off, rows), ls], ext_g[2, pl.ds(off, rows), ls], ext_g[0, pl.ds(off + hl, rows), ls]]
            gc = b_ref[:, ls] + w_ref[0:1, ls] * taps[0] + w_ref[1:2, ls] * taps[1] + w_ref[2:3, ls] * taps[2]
            sg = _sigmoid(gc)
            return taps, gc * sg, da_rows * val * (sg * (1.0 + gc * (1.0 - sg)))

        for r0, ls in _ffn_blocks(t, f):
            rs = slice(r0, r0 + FFN_ROWS)
            da_rows = da_ref[rs, ls]
            taps, act, dgc = gate_block(FFN_ROWS, r0, ls, val_ref[rs, ls], da_rows)
            ext_d[0, rs, ls] = dgc
            dup_ref[rs, ls] = (da_rows * act).astype(BF16)
            for k in range(3):
                sums[8 * k:8 * k + 8, ls] += fold8(dgc * taps[k])
            sums[24:32, ls] += fold8(dgc)
        _, _, dgc_next = gate_block(hl, t, slice(None), valn_ref[...], dan_ref[...])
        ext_d[0, pl.ds(t, hl), :] = jnp.where(i < nt - 1, dgc_next, 0.0)
        _offset_copies(ext_d, (1, 2), t)
        for r0, ls in _ffn_blocks(t, f):
            gate_lanes = slice(f + ls.start, f + ls.stop)
            rs = pl.ds(r0, FFN_ROWS)
            dup_ref[r0:r0 + FFN_ROWS, gate_lanes] = (
                w_ref[2:3, ls] * ext_d[0, rs, ls] + w_ref[1:2, ls] * ext_d[1, rs, ls]
                + w_ref[0:1, ls] * ext_d[2, rs, ls]).astype(BF16)
        rows = [jnp.sum(sums[8 * k:8 * k + 8, :], axis=0, keepdims=True) for k in range(4)]
        _acc_rows(dw_ref, i, jnp.concatenate(rows[:3], axis=0))
        _acc_rows(db_ref, i, rows[3])

    in_specs = [_row_spec(t, f, 0), _next_rows_spec(t, hl, f, 0, s),
                _row_spec(t, f, 1), _prev_rows_spec(t, hl, f, 1), _next_rows_spec(t, hl, f, 1, s),
                _row_spec(t, f), _next_rows_spec(t, hl, f, 0, s),
                pl.BlockSpec((3, f), lambda i: (0, 0)), _vec_spec(f)]
    return pl.pallas_call(
        body, grid=(nt,), in_specs=in_specs,
        out_specs=[_row_spec(t, 2 * f), pl.BlockSpec((3, f), lambda i: (0, 0)), _vec_spec(f)],
        out_shape=[_sds((s, 2 * f), BF16), _sds((3, f), F32), _sds((1, f), F32)],
        scratch_shapes=[pltpu.VMEM((3, t + 2 * hl, f), F32), pltpu.VMEM((3, t + hl, f), F32),
                        pltpu.VMEM((32, f), F32)],
        name="ffn_act_bwd",
        compiler_params=_params(("arbitrary",)))(up, up, up, up, up, da, da, dw_w, dw_b)


def local_step(x, mem, positions, target, gw, sp):
    n_layers = gw["w_in"].shape[0]
    inv_freq = 1.0 / (ROPE_THETA ** (jnp.arange(0, RET_QK_DIM, 2, dtype=F32) / RET_QK_DIM))
    cos, sin = rope_tables(positions, inv_freq[None, :])
    tables = _decay_tables(min(RET_TILE, x.shape[0]))
    row = lambda name, l: sp[name][l][None, :]

    saved = []
    h = x
    u = rms_cast(x, row("norm_mix_g", 0))
    for l in range(n_layers):
        a = {"h0": h}
        a["u"] = u
        a["p"] = mm_fwd("mm_in", a["u"], gw["w_in"], l, True)
        a["o"], a["z"], a["states"] = retention_fwd(a["p"], cos, sin, row("ret_gn_g", l), tables)
        a["y_a"] = mm_fwd("mm_ret_out", a["z"], gw["w_ret_out"], l, False)
        a["c1"], a["c3"] = conv_fwd(a["p"], sp["conv_dw_w"][l], row("conv_dw_b", l), row("conv_ln_g", l),
                                    row("conv_ln_b", l))
        a["y_b"] = mm_fwd("mm_conv_out", a["c3"], gw["w_conv_out"], l, False)
        a["h1"], a["hx"], a["mixed"] = mix_out_fwd(a["p"], row("b_gate", l), a["y_a"], a["y_b"], row("b_conv_out", l),
                                                   gw["w_mix_out"], l, h, row("norm_xattn_g", l))
        a["qx"] = mm_fwd("mm_xq", a["hx"], gw["w_xq"], l, False, out_dtype=BF16)
        a["mem_n"] = rms_cast(mem, row("norm_mem_g", l))
        a["kv"] = mm_fwd("mm_xkv", a["mem_n"], gw["w_xkv"], l, True)
        a["ox"] = attn_fwd(a["qx"], a["kv"])
        a["h2"], a["hf"] = mm_fwd("mm_xo", a["ox"], gw["w_xo"], l, False, res=a["h1"], norm_g=row("norm_ffn_g", l))
        a["up"] = mm_fwd("mm_up", a["hf"], gw["w_up"], l, True)
        a["act"] = ffn_act_fwd(a["up"], sp["ffn_dw_w"][l], row("ffn_dw_b", l))
        if l + 1 < n_layers:
            h, u = mm_fwd("mm_down", a["act"], gw["w_down"], l, False, res=a["h2"], norm_g=row("norm_mix_g", l + 1))
        else:
            h = mm_fwd("mm_down", a["act"], gw["w_down"], l, False, res=a["h2"])
        saved.append(a)

    dh, d_final_g, loss = loss_head(h, sp["norm_final_g"][None, :], target)

    big = {}

    def dw(name, key, act, dy, col, l):
        big[key] = mm_dw(name, act, dy, col, l, n_layers, big.get(key))

    small = {n: [None] * n_layers for n in SMALL_REPL + SMALL_SHARDED if n != "norm_final_g"}
    for l in range(n_layers - 1, -1, -1):
        a = saved[l]
        d_act = mm_dx("mm_down_dx", dh, gw["w_down"], l, False)
        dw("mm_down_dw", "w_down", a["act"], dh, False, l)
        d_up, small["ffn_dw_w"][l], small["ffn_dw_b"][l] = ffn_act_bwd(a["up"], sp["ffn_dw_w"][l],
                                                                        row("ffn_dw_b", l), d_act)
        dh, small["norm_ffn_g"][l] = mm_dx("mm_up_dx", d_up, gw["w_up"], l, True,
                                           rms=(a["h2"], row("norm_ffn_g", l), dh))
        dw("mm_up_dw", "w_up", a["hf"], d_up, True, l)
        d_ox = mm_dx("mm_xo_dx", dh, gw["w_xo"], l, False)
        dw("mm_xo_dw", "w_xo", a["ox"], dh, False, l)
        d_qx, d_kv = attn_bwd(a["qx"], a["kv"], d_ox)
        dw("mm_xq_dw", "w_xq", a["hx"], d_qx, False, l)
        d_mem_n = mm_dx("mm_xkv_dx", d_kv, gw["w_xkv"], l, True)
        dw("mm_xkv_dw", "w_xkv", a["mem_n"], d_kv, True, l)
        _, small["norm_mem_g"][l] = rms_bwd(mem, row("norm_mem_g", l), d_mem_n)
        dh, small["norm_xattn_g"][l] = mm_dx("mm_xq_dx", d_qx, gw["w_xq"], l, False,
                                             rms=(a["h1"], row("norm_xattn_g", l), dh))
        d_mixed = mm_dx("mm_mix_out_dx", dh, gw["w_mix_out"], l, False)
        dw("mm_mix_out_dw", "w_mix_out", a["mixed"], dh, False, l)
        d_ya, d_yb, dp_gate, small["b_gate"][l], small["b_conv_out"][l] = gate_mix_bwd(
            a["p"], row("b_gate", l), a["y_a"], a["y_b"], row("b_conv_out", l), d_mixed)
        dw("mm_conv_out_dw", "w_conv_out", a["c3"], d_yb, False, l)
        d_c1, small["conv_ln_g"][l], small["conv_ln_b"][l] = mm_dx(
            "mm_conv_out_dx", d_yb, gw["w_conv_out"], l, False,
            ln=(a["c1"], row("conv_ln_g", l), row("conv_ln_b", l)))
        dp_conv, small["conv_dw_w"][l], small["conv_dw_b"][l] = conv_dw_bwd(a["p"], d_c1, sp["conv_dw_w"][l])
        d_z = mm_dx("mm_ret_out_dx", d_ya, gw["w_ret_out"], l, False)
        dw("mm_ret_out_dw", "w_ret_out", a["z"], d_ya, False, l)
        d_o, dp_gret, small["ret_gn_g"][l] = gn_gate_bwd(a["o"], a["p"], row("ret_gn_g", l), d_z)
        dp_q, dp_k, dp_v = retention_bwd(a["p"], cos, sin, a["states"], d_o, tables)
        dp = jnp.concatenate([dp_q, dp_k, dp_v, dp_gret, dp_conv, dp_gate], axis=1)
        dw("mm_in_dw", "w_in", a["u"], dp, True, l)
        dh, small["norm_mix_g"][l] = mm_dx("mm_in_dx", dp, gw["w_in"], l, True,
                                           rms=(a["h0"], row("norm_mix_g", l), dh))

    small = {n: jnp.stack([g.reshape(sp[n].shape[1:]) for g in v]) for n, v in small.items()}
    small["norm_final_g"] = d_final_g.reshape(-1)
    return loss, dh, big, small


_ANY = pl.BlockSpec(memory_space=pl.ANY)


def _place():
    x, y, c = lax.axis_index("x"), lax.axis_index("y"), lax.axis_index("c")
    return x, y, c


def _other_chips(x, y):
    return [(1 - x, y), (x, 1 - y), (1 - x, 1 - y)]


def place_shard(w, chip):
    lyr, a, b = w.shape
    t = _flat_tile(a, b)

    def body(chip_ref, w_ref, o_ref):
        o_ref[...] = w_ref[...].astype(BF16)

    grid_spec = pltpu.PrefetchScalarGridSpec(
        num_scalar_prefetch=1, grid=(lyr, a // t),
        in_specs=[pl.BlockSpec((None, t, b), lambda l, i, cr: (l, i, 0))],
        out_specs=pl.BlockSpec((None, None, t, b), lambda l, i, cr: (l, cr[0], i, 0)))
    return pl.pallas_call(body, grid_spec=grid_spec, out_shape=_sds((lyr, N_CHIPS, a, b), BF16), name="place_shard",
                          compiler_params=_params(("parallel", "parallel")))(chip, w)


def gather_weights(bufs):
    n = len(bufs)

    def body(*refs):
        outs = refs[n:2 * n]
        ici_send, ici_recv, pair_send, pair_recv = refs[2 * n:]
        x, y, c = _place()
        mine = 2 * x + y
        chips = _other_chips(x, y)

        def part(w, core, slot):
            lh = outs[w].shape[0] // 2
            return outs[w].at[pl.ds(core * lh, lh), slot]

        def over_ici(w, j, slot):
            px, py = chips[j]
            return pltpu.make_async_remote_copy(
                src_ref=part(w, c, slot), dst_ref=part(w, c, slot), send_sem=ici_send.at[w, j],
                recv_sem=ici_recv.at[w, j], device_id=(px, py, c), device_id_type=MESH)

        def to_pair(w, j, core):
            px, py = chips[j]
            return pltpu.make_async_remote_copy(
                src_ref=part(w, core, 2 * px + py), dst_ref=part(w, core, 2 * px + py), send_sem=pair_send.at[w, j],
                recv_sem=pair_recv.at[w, j], device_id=(x, y, 1 - c), device_id_type=MESH)

        for w in range(n):
            for j in range(3):
                over_ici(w, j, mine).start()
        for w in range(n):
            for j, (px, py) in enumerate(chips):
                over_ici(w, j, 2 * px + py).wait_recv()
                to_pair(w, j, c).start()
        for w in range(n):
            for j in range(3):
                to_pair(w, j, 1 - c).wait_recv()
                to_pair(w, j, c).wait_send()
                over_ici(w, j, mine).wait_send()

    return pl.pallas_call(
        body, in_specs=[_ANY] * n, out_specs=[_ANY] * n, out_shape=[_sds(b.shape, b.dtype) for b in bufs],
        input_output_aliases={i: i for i in range(n)},
        scratch_shapes=[pltpu.SemaphoreType.DMA((n, 3))] * 4,
        name="gather_weights", compiler_params=_params())(*bufs)


def pair_exchange(grads):
    n = len(grads)

    def body(*refs):
        ins, outs = refs[:n], refs[n:2 * n]
        send_sems, recv_sems = refs[2 * n:]
        x, y, c = _place()
        cps = []
        for w in range(n):
            lh = ins[w].shape[0] // 2
            cp = pltpu.make_async_remote_copy(
                src_ref=ins[w].at[pl.ds((1 - c) * lh, lh)], dst_ref=outs[w], send_sem=send_sems.at[w],
                recv_sem=recv_sems.at[w], device_id=(x, y, 1 - c), device_id_type=MESH)
            cp.start()
            cps.append(cp)
        for cp in cps:
            cp.wait_send()
            cp.wait_recv()

    out_shape = [_sds((g.shape[0] // 2,) + g.shape[1:], g.dtype) for g in grads]
    return pl.pallas_call(
        body, in_specs=[_ANY] * n, out_specs=[_ANY] * n, out_shape=out_shape,
        scratch_shapes=[pltpu.SemaphoreType.DMA((n,)), pltpu.SemaphoreType.DMA((n,))],
        name="pair_exchange", compiler_params=_params())(*grads)


def chip_exchange(parts):
    n = len(parts)

    def body(*refs):
        ins, outs = refs[:n], refs[n:2 * n]
        send_sems, recv_sems = refs[2 * n:]
        x, y, c = _place()
        mine = 2 * x + y
        chips = _other_chips(x, y)

        def copy(w, j, slot):
            px, py = chips[j]
            return pltpu.make_async_remote_copy(
                src_ref=ins[w].at[:, 2 * px + py], dst_ref=outs[w].at[slot], send_sem=send_sems.at[w, j],
                recv_sem=recv_sems.at[w, j], device_id=(px, py, c), device_id_type=MESH)

        for w in range(n):
            for j in range(3):
                copy(w, j, mine).start()
        for w in range(n):
            for j, (px, py) in enumerate(chips):
                copy(w, j, 2 * px + py).wait_recv()
                copy(w, j, mine).wait_send()

    out_shape = [_sds((N_CHIPS, g.shape[0]) + g.shape[2:], g.dtype) for g in parts]
    return pl.pallas_call(
        body, in_specs=[_ANY] * n, out_specs=[_ANY] * n, out_shape=out_shape,
        scratch_shapes=[pltpu.SemaphoreType.DMA((n, 3)), pltpu.SemaphoreType.DMA((n, 3))],
        name="chip_exchange", compiler_params=_params())(*parts)


def pair_share(halves):
    n = len(halves)

    def body(*refs):
        ins, outs = refs[:n], refs[n:2 * n]
        send_sems, recv_sems = refs[2 * n:]
        x, y, c = _place()
        cps = []
        for w in range(n):
            cp = pltpu.make_async_remote_copy(
                src_ref=ins[w], dst_ref=outs[w], send_sem=send_sems.at[w], recv_sem=recv_sems.at[w],
                device_id=(x, y, 1 - c), device_id_type=MESH)
            cp.start()
            cps.append(cp)
        for cp in cps:
            cp.wait_send()
            cp.wait_recv()

    return pl.pallas_call(
        body, in_specs=[_ANY] * n, out_specs=[_ANY] * n, out_shape=[_sds(g.shape, g.dtype) for g in halves],
        scratch_shapes=[pltpu.SemaphoreType.DMA((n,)), pltpu.SemaphoreType.DMA((n,))],
        name="pair_share", compiler_params=_params())(*halves)


def all_reduce_small(vec):
    r, lanes = vec.shape

    def body(v_ref, o_ref, buf, send_sems, recv_sems):
        x, y, c = _place()
        me = 4 * x + 2 * y + c
        buf[me] = v_ref[...]
        cps = []
        for k in range(1, N_DEV):
            peer = (me + k) % N_DEV
            cp = pltpu.make_async_remote_copy(
                src_ref=v_ref, dst_ref=buf.at[me], send_sem=send_sems.at[k - 1], recv_sem=recv_sems.at[k - 1],
                device_id=(peer // 4, (peer // 2) % 2, peer % 2), device_id_type=MESH)
            cp.start()
            cps.append(cp)
        for k in range(1, N_DEV):
            src = (me + N_DEV - k) % N_DEV
            cps[k - 1].wait_send()
            pltpu.make_async_remote_copy(
                src_ref=v_ref, dst_ref=buf.at[src], send_sem=send_sems.at[k - 1], recv_sem=recv_sems.at[k - 1],
                device_id=(src // 4, (src // 2) % 2, src % 2), device_id_type=MESH).wait_recv()
        acc = buf[0]
        for d in range(1, N_DEV):
            acc = acc + buf[d]
        o_ref[...] = acc

    vm = pl.BlockSpec(memory_space=pltpu.VMEM)
    return pl.pallas_call(
        body, in_specs=[vm], out_specs=vm, out_shape=_sds((r, lanes), F32),
        scratch_shapes=[pltpu.VMEM((N_DEV, r, lanes), F32), pltpu.SemaphoreType.DMA((N_DEV - 1,)),
                        pltpu.SemaphoreType.DMA((N_DEV - 1,))],
        name="all_reduce_small", compiler_params=_params())(vec)


ELEMENTWISE_BLOCK_BYTES = 1 << 20


def _flat_tile(rows, cols):
    for t in (512, 256, 128, 64, 32, 16, 8):
        if rows % t == 0 and t * cols * 4 <= ELEMENTWISE_BLOCK_BYTES:
            return t
    return rows


def add_pair(g, r, half):
    lyr, _, a, b = g.shape
    lh = lyr // 2
    rows = lh * 4 * a
    t = _flat_tile(rows, b)
    nb = rows // t
    g2 = g.reshape(lyr * 4 * a, b)
    r2 = r.reshape(rows, b)

    def body(half_ref, g_ref, r_ref, o_ref):
        o_ref[...] = (g_ref[...] + r_ref[...]).astype(BF16)

    grid_spec = pltpu.PrefetchScalarGridSpec(
        num_scalar_prefetch=1, grid=(nb,),
        in_specs=[pl.BlockSpec((t, b), lambda i, hr: (hr[0] * nb + i, 0)), pl.BlockSpec((t, b), lambda i, hr: (i, 0))],
        out_specs=pl.BlockSpec((t, b), lambda i, hr: (i, 0)))
    out = pl.pallas_call(body, grid_spec=grid_spec, out_shape=_sds((rows, b), BF16), name="add_pair",
                         compiler_params=_params(("parallel",)))(half, g2, r2)
    return out.reshape(lh, 4, a, b)


def sum_chips(own, parts, chip):
    _, lh, a, b = parts.shape
    t = _flat_tile(a, b)

    def body(chip_ref, own_ref, p_ref, o_ref):
        mine = chip_ref[0]

        def term(s):
            return jnp.where(mine == s, own_ref[...], p_ref[s]).astype(F32)

        o_ref[...] = ((term(0) + term(1)) + term(2)) + term(3)

    grid_spec = pltpu.PrefetchScalarGridSpec(
        num_scalar_prefetch=1, grid=(lh, a // t),
        in_specs=[pl.BlockSpec((None, None, t, b), lambda l, i, cr: (l, cr[0], i, 0)),
                  pl.BlockSpec((N_CHIPS, None, t, b), lambda l, i, cr: (0, l, i, 0))],
        out_specs=pl.BlockSpec((None, t, b), lambda l, i, cr: (l, i, 0)))
    return pl.pallas_call(body, grid_spec=grid_spec, out_shape=_sds((lh, a, b), F32), name="sum_chips",
                          compiler_params=_params(("parallel", "parallel")))(chip, own, parts)


def _adamw_math(w, g, m, v):
    mm = ADAM_B1 * m + (1.0 - ADAM_B1) * g
    vv = ADAM_B2 * v + (1.0 - ADAM_B2) * jnp.square(g)
    m_hat = mm / (1.0 - ADAM_B1 ** ADAM_STEP)
    v_hat = vv / (1.0 - ADAM_B2 ** ADAM_STEP)
    return -ADAM_LR * (m_hat / (jnp.sqrt(v_hat) + ADAM_EPS) + ADAM_WD * w), mm, vv


def adamw(w, g, m, v):
    shape = w.shape
    c = shape[-1]
    rows = int(np.prod(shape[:-1])) if len(shape) > 1 else 1
    t = _flat_tile(rows, c)
    flat = lambda z: z.reshape(rows, c)

    def body(w_ref, g_ref, m_ref, v_ref, d_ref, nm_ref, nv_ref):
        d_ref[...], nm_ref[...], nv_ref[...] = _adamw_math(w_ref[...], g_ref[...], m_ref[...], v_ref[...])

    spec = pl.BlockSpec((t, c), lambda i: (i, 0))
    outs = pl.pallas_call(body, grid=(rows // t,), in_specs=[spec] * 4, out_specs=[spec] * 3,
                          out_shape=[_sds((rows, c), F32)] * 3, name="adamw",
                          compiler_params=_params(("parallel",)))(flat(w), flat(g), flat(m), flat(v))
    return tuple(o.reshape(shape) for o in outs)


def adamw_halves(w, g_own, g_other, m, v, core):
    lyr, a, b = w.shape
    lh = lyr // 2
    t = _flat_tile(a, b)

    def body(core_ref, w_ref, go_ref, gs_ref, m_ref, v_ref, g_ref, d_ref, nm_ref, nv_ref):
        own = pl.program_id(0) // lh == core_ref[0]
        g = jnp.where(own, go_ref[...], gs_ref[...])
        g_ref[...] = g
        d_ref[...], nm_ref[...], nv_ref[...] = _adamw_math(w_ref[...], g, m_ref[...], v_ref[...])

    full = pl.BlockSpec((None, t, b), lambda l, i, cr: (l, i, 0))
    own_spec = pl.BlockSpec((None, t, b), lambda l, i, cr: (jnp.clip(l - cr[0] * lh, 0, lh - 1), i, 0))
    other_spec = pl.BlockSpec((None, t, b), lambda l, i, cr: (jnp.clip(l - (1 - cr[0]) * lh, 0, lh - 1), i, 0))
    grid_spec = pltpu.PrefetchScalarGridSpec(
        num_scalar_prefetch=1, grid=(lyr, a // t), in_specs=[full, own_spec, other_spec, full, full],
        out_specs=[full] * 4)
    return pl.pallas_call(body, grid_spec=grid_spec, out_shape=[_sds(w.shape, F32)] * 4, name="adamw_halves",
                          compiler_params=_params(("parallel", "parallel")))(core, w, g_own, g_other, m, v)


def _pack(parts):
    flat = jnp.concatenate([p.reshape(-1) for p in parts])
    pad = (-flat.shape[0]) % 1024
    return jnp.pad(flat, (0, pad)).reshape(-1, 128)


def _unpack(packed, shapes):
    flat = packed.reshape(-1)
    out, off = [], 0
    for shp in shapes:
        size = int(np.prod(shp))
        out.append(flat[off:off + size].reshape(shp))
        off += size
    return out


def kernel(x, mem, positions, norm_mix_g, w_in, b_gate, ret_gn_g, w_ret_out, conv_dw_w, conv_dw_b, conv_ln_g, conv_ln_b, w_conv_out, b_conv_out, w_mix_out, norm_xattn_g, norm_mem_g, w_xq, w_xkv, w_xo, norm_ffn_g, w_up, ffn_dw_w, ffn_dw_b, w_down, norm_final_g, loss_target, m_norm_mix_g, m_w_in, m_b_gate, m_ret_gn_g, m_w_ret_out, m_conv_dw_w, m_conv_dw_b, m_conv_ln_g, m_conv_ln_b, m_w_conv_out, m_b_conv_out, m_w_mix_out, m_norm_xattn_g, m_norm_mem_g, m_w_xq, m_w_xkv, m_w_xo, m_norm_ffn_g, m_w_up, m_ffn_dw_w, m_ffn_dw_b, m_w_down, m_norm_final_g, v_norm_mix_g, v_w_in, v_b_gate, v_ret_gn_g, v_w_ret_out, v_conv_dw_w, v_conv_dw_b, v_conv_ln_g, v_conv_ln_b, v_w_conv_out, v_b_conv_out, v_w_mix_out, v_norm_xattn_g, v_norm_mem_g, v_w_xq, v_w_xkv, v_w_xo, v_norm_ffn_g, v_w_up, v_ffn_dw_w, v_ffn_dw_b, v_w_down, v_norm_final_g):
    args = locals()
    w = {n: args[n] for n in WEIGHTS}
    m = {n: args["m_" + n] for n in WEIGHTS}
    v = {n: args["v_" + n] for n in WEIGHTS}
    chip = 2 * lax.axis_index("x") + lax.axis_index("y")
    core = lax.axis_index("c")

    chip_op = chip.reshape(1).astype(jnp.int32)
    core_op = core.reshape(1).astype(jnp.int32)
    gathered = gather_weights([place_shard(w[n], chip_op) for n in BIG])
    gw = dict(zip(BIG, gathered))

    sp = {n: w[n] for n in SMALL_REPL}
    placed = []
    for n in SMALL_SHARDED:
        cols = w[n].shape[-1]
        full = jnp.zeros(w[n].shape[:-1] + (N_CHIPS * cols,), F32)
        placed.append(lax.dynamic_update_slice_in_dim(full, w[n], chip * cols, axis=2))
    placed_shapes = [p.shape for p in placed]
    gathered_small = all_reduce_small(_pack([jnp.where(core == 0, p, 0.0) for p in placed]))
    for n, arr in zip(SMALL_SHARDED, _unpack(gathered_small, placed_shapes)):
        sp[n] = arr

    loss, grad_x, big, small = local_step(x[0], mem[0], positions.reshape(-1, 1), loss_target[0], gw, sp)

    names = [n for n in SMALL_REPL + SMALL_SHARDED]
    shapes = [small[n].shape for n in names] + [(128,)]
    reduced = _unpack(all_reduce_small(_pack([small[n] for n in names] + [loss.reshape(-1)])), shapes)
    grads = dict(zip(names, reduced[:-1]))
    loss_out = reduced[-1][0]
    for n in SMALL_SHARDED:
        cols = w[n].shape[-1]
        grads[n] = lax.dynamic_slice_in_dim(grads[n], chip * cols, cols, axis=2)

    blist = [big[n] for n in BIG]
    from_pair = pair_exchange(blist)
    pair_sum = [add_pair(g, r, core_op) for g, r in zip(blist, from_pair)]
    from_chips = chip_exchange(pair_sum)
    halves = [sum_chips(own, parts, chip_op) for own, parts in zip(pair_sum, from_chips)]
    other_halves = pair_share(halves)

    delta, new_m, new_v = {}, {}, {}
    for n, g_own, g_other in zip(BIG, halves, other_halves):
        grads[n], delta[n], new_m[n], new_v[n] = adamw_halves(w[n], g_own, g_other, m[n], v[n], core_op)
    for n in WEIGHTS:
        if n not in BIG:
            delta[n], new_m[n], new_v[n] = adamw(w[n], grads[n], m[n], v[n])
    return (loss_out, grad_x[None], *[grads[n] for n in WEIGHTS], *[delta[n] for n in WEIGHTS],
            *[new_m[n] for n in WEIGHTS], *[new_v[n] for n in WEIGHTS])
```

```python
import functools

import jax
import jax.numpy as jnp
import numpy as np
from jax import lax
from jax.experimental import pallas as pl
from jax.experimental.pallas import tpu as pltpu

F32 = jnp.float32
BF16 = jnp.bfloat16
MESH = pl.DeviceIdType.MESH

D_MODEL = 1024
CHUNK = 64
RET_HEADS = 4
RET_QK_DIM = 256
RET_V_DIM = 512
ROPE_THETA = 10000.0
CONV_WIDTH = 31
X_HEADS = 4
X_HEAD_DIM = 256
FFN_DIM = 2816
RMS_EPS = 1e-6
LN_EPS = 1e-5
ADAM_LR = 0.001
ADAM_B1 = 0.9
ADAM_B2 = 0.999
ADAM_EPS = 1e-08
ADAM_WD = 0.01
ADAM_STEP = 10

N_CHIPS = 4
N_DEV = 8
CONV_HALO = 32
FFN_HALO = 8
V7X_VMEM_LIMIT = 56 * 1024 * 1024
ROW_TILE = 256
STRIP_ROWS = 16
STRIP_LANES = 1024
DW_TAPS = 2
FFN_ROWS = 16
FFN_LANES = 256
MM_TILE_M = 1024
RET_TILE = 512
ATTN_TILE = 512
MIX_TILE = 512

BIG = ("w_in", "w_ret_out", "w_conv_out", "w_mix_out", "w_xq", "w_xkv", "w_xo", "w_up", "w_down")
COL_SHARDED = ("w_in", "w_xkv", "w_up")
SMALL_REPL = ("norm_mix_g", "b_gate", "ret_gn_g", "conv_dw_b", "conv_ln_g", "conv_ln_b", "b_conv_out",
              "norm_xattn_g", "norm_mem_g", "norm_ffn_g", "ffn_dw_b", "norm_final_g")
SMALL_SHARDED = ("conv_dw_w", "ffn_dw_w")
WEIGHTS = ('norm_mix_g', 'w_in', 'b_gate', 'ret_gn_g', 'w_ret_out', 'conv_dw_w', 'conv_dw_b', 'conv_ln_g',
           'conv_ln_b', 'w_conv_out', 'b_conv_out', 'w_mix_out', 'norm_xattn_g', 'norm_mem_g', 'w_xq', 'w_xkv',
           'w_xo', 'norm_ffn_g', 'w_up', 'ffn_dw_w', 'ffn_dw_b', 'w_down', 'norm_final_g')


def _params(sem=None):
    return pltpu.CompilerParams(dimension_semantics=sem, vmem_limit_bytes=V7X_VMEM_LIMIT)


def _sds(shape, dtype):
    return jax.ShapeDtypeStruct(tuple(shape), dtype)


def _sigmoid(x):
    return jax.nn.sigmoid(x)


def _dot(a, b, ca, cb):
    return lax.dot_general(a, b, (((ca,), (cb,)), ((), ())), preferred_element_type=F32)


def _nn(a, b):
    return _dot(a, b, 1, 0)


def _nt(a, b):
    return _dot(a, b, 1, 1)


def _tn(a, b):
    return _dot(a, b, 0, 0)


def _mm(name, dims, grid, in_specs, out_spec, out_sds, nk, operands, res=False, norm=False, rms_bwd=False,
        ln_bwd=False):
    n_in = 2 + res + norm + 3 * rms_bwd + 3 * ln_bwd

    def body(*refs):
        ins, outs = refs[:n_in], refs[n_in:]
        a_ref, b_ref = ins[:2]
        extra = list(ins[2:])
        r_ref = extra.pop(0) if res else None
        g_ref = extra.pop(0) if norm else None
        o_ref = outs[0]
        row_tile = pl.program_id(0)
        prod = _dot(a_ref[...].astype(BF16), b_ref[...].astype(BF16), *dims)

        def finish(total):
            if rms_bwd:
                h_ref, gain_ref, dres_ref = extra
                dx, dg = _rms_bwd_math(h_ref[...], gain_ref[...], total)
                o_ref[...] = dx + dres_ref[...]
                _acc_rows(outs[1], row_tile, dg)
                return
            if ln_bwd:
                x_ref, gain_ref, bias_ref = extra
                o_ref[...], dg, db = _ln_silu_bwd_math(x_ref[...], gain_ref[...], bias_ref[...], total)
                _acc_rows(outs[1], row_tile, dg)
                _acc_rows(outs[2], row_tile, db)
                return
            o_ref[...] = total.astype(o_ref.dtype)
            if norm:
                r = lax.rsqrt(jnp.mean(total * total, axis=-1, keepdims=True) + RMS_EPS)
                outs[1][...] = (total * r * g_ref[...]).astype(BF16)

        if nk == 1:
            finish(prod + r_ref[...] if res else prod)
        else:
            k = pl.program_id(2)

            @pl.when(k == 0)
            def _():
                o_ref[...] = (prod + r_ref[...]) if res else prod

            @pl.when((k > 0) & (k < nk - 1))
            def _():
                o_ref[...] += prod

            @pl.when(k == nk - 1)
            def _():
                finish(o_ref[...] + prod)

    assert nk == 1 or out_sds.dtype == F32
    out_specs, out_shape = [out_spec], [out_sds]
    if norm:
        out_specs.append(out_spec)
        out_shape.append(_sds(out_sds.shape, BF16))
    for _ in range(rms_bwd + 2 * ln_bwd):
        n = out_sds.shape[1]
        out_specs.append(pl.BlockSpec((1, n), lambda i, j, k: (0, 0)))
        out_shape.append(_sds((1, n), F32))
    sem = ("arbitrary",) * 3 if (rms_bwd or ln_bwd) else ("parallel", "parallel", "arbitrary")
    out = pl.pallas_call(body, grid=grid, in_specs=in_specs, out_specs=out_specs, out_shape=out_shape, name=name,
                         compiler_params=_params(sem))(*operands)
    return out if (norm or rms_bwd or ln_bwd) else out[0]


def _div_tile(n, want):
    best = None
    for t in range(128, min(n, want) + 1, 128):
        if n % t == 0:
            best = t
    assert best is not None, (n, want)
    return best


def mm_fwd(name, a, g, l, col, out_dtype=F32, res=None, norm_g=None):
    m, k_dim = a.shape
    tm = min(MM_TILE_M, m)
    if col:
        _, _, kk, b = g.shape
        assert kk == k_dim
        tn = _div_tile(b, 1408)
        nps = b // tn
        n = 4 * b
        grid = (m // tm, n // tn, 1)
        in_specs = [pl.BlockSpec((tm, k_dim), lambda i, j, k: (i, 0)),
                    pl.BlockSpec((None, None, k_dim, tn), lambda i, j, k: (l, j // nps, 0, j % nps))]
        nk = 1
        w = g
    else:
        lyr, _, a_rows, n = g.shape
        assert 4 * a_rows == k_dim
        w = g.reshape(lyr, k_dim, n)
        tk = _div_tile(k_dim, 1408)
        tn = n
        nk = k_dim // tk
        grid = (m // tm, 1, nk)
        in_specs = [pl.BlockSpec((tm, tk), lambda i, j, k: (i, k)),
                    pl.BlockSpec((None, tk, tn), lambda i, j, k: (l, k, j))]
    ops = [a, w]
    if res is not None:
        in_specs.append(pl.BlockSpec((tm, tn), lambda i, j, k: (i, j)))
        ops.append(res)
    if norm_g is not None:
        assert tn == n
        in_specs.append(pl.BlockSpec((1, n), lambda i, j, k: (0, 0)))
        ops.append(norm_g)
    return _mm(name, (1, 0), grid, in_specs, pl.BlockSpec((tm, tn), lambda i, j, k: (i, j)), _sds((m, n), out_dtype),
               nk, ops, res=res is not None, norm=norm_g is not None)


def mm_dx(name, dy, g, l, col, rms=None, ln=None):
    m, n = dy.shape
    tm = min(MM_TILE_M, m)
    if col:
        _, _, k_dim, b = g.shape
        assert 4 * b == n
        tk = _div_tile(b, 2560)
        nps = b // tk
        nk = n // tk
        grid = (m // tm, 1, nk)
        in_specs = [pl.BlockSpec((tm, tk), lambda i, j, k: (i, k)),
                    pl.BlockSpec((None, None, k_dim, tk), lambda i, j, k: (l, k // nps, 0, k % nps))]
        out_spec = pl.BlockSpec((tm, k_dim), lambda i, j, k: (i, 0))
        w = g
        tno = k_dim
    else:
        lyr, _, a_rows, nn_ = g.shape
        assert nn_ == n
        k_dim = 4 * a_rows
        w = g.reshape(lyr, k_dim, n)
        tno = _div_tile(k_dim, 1408)
        nk = 1
        grid = (m // tm, k_dim // tno, 1)
        in_specs = [pl.BlockSpec((tm, n), lambda i, j, k: (i, 0)),
                    pl.BlockSpec((None, tno, n), lambda i, j, k: (l, j, 0))]
        out_spec = pl.BlockSpec((tm, tno), lambda i, j, k: (i, j))
    ops = [dy, w]
    if rms is not None:
        assert tno == k_dim
        h, gain, dres = rms
        rows = pl.BlockSpec((tm, k_dim), lambda i, j, k: (i, 0))
        in_specs += [rows, pl.BlockSpec((1, k_dim), lambda i, j, k: (0, 0)), rows]
        ops += [h, gain, dres]
    if ln is not None:
        assert tno == k_dim
        vec = pl.BlockSpec((1, k_dim), lambda i, j, k: (0, 0))
        in_specs += [pl.BlockSpec((tm, k_dim), lambda i, j, k: (i, 0)), vec, vec]
        ops += list(ln)
    return _mm(name, (1, 1), grid, in_specs, out_spec, _sds((m, k_dim), F32), nk, ops, rms_bwd=rms is not None,
               ln_bwd=ln is not None)


def mm_dw(name, a, dy, col, l, n_layers, into=None):
    m, k_dim = a.shape
    _, n = dy.shape
    ts = min(2 * MM_TILE_M, m)
    ns = m // ts
    tko = _div_tile(k_dim, 1408)
    if col:
        b = n // 4
        tn = _div_tile(b, 1408)
        nps = b // tn
        grid = (k_dim // tko, n // tn, ns)
        out_spec = pl.BlockSpec((None, None, tko, tn), lambda i, j, s: (l, j // nps, i, j % nps))
        shape = (n_layers, 4, k_dim, b)
    else:
        tn = n
        grid = (k_dim // tko, 1, ns)
        out_spec = pl.BlockSpec((None, tko, tn), lambda i, j, s: (l, i, j))
        shape = (n_layers, k_dim, n)
    in_specs = [pl.BlockSpec((ts, tko), lambda i, j, s: (s, i)),
                pl.BlockSpec((ts, tn), lambda i, j, s: (s, j))]
    ops = [a, dy]
    aliases = {}
    if into is not None:
        in_specs.append(_ANY)
        ops.append(into.reshape(shape))
        aliases = {2: 0}

    def body(a_ref, b_ref, *rest):
        o_ref = rest[-1]
        prod = _tn(a_ref[...].astype(BF16), b_ref[...].astype(BF16))
        if ns == 1:
            o_ref[...] = prod
        else:
            s = pl.program_id(2)

            @pl.when(s == 0)
            def _():
                o_ref[...] = prod

            @pl.when(s > 0)
            def _():
                o_ref[...] += prod

    out = pl.pallas_call(body, grid=grid, in_specs=in_specs, out_specs=out_spec, out_shape=_sds(shape, F32),
                         input_output_aliases=aliases, name=name,
                         compiler_params=_params(("parallel", "parallel", "arbitrary")))(*ops)
    return out.reshape(n_layers, 4, k_dim if col else k_dim // 4, shape[-1])


def _row_spec(t, c, col=0):
    return pl.BlockSpec((t, c), lambda i: (i, col))


def _vec_spec(c):
    return pl.BlockSpec((1, c), lambda i: (0, 0))


def _acc_rows(ref, i, val):
    @pl.when(i == 0)
    def _():
        ref[...] = val

    @pl.when(i > 0)
    def _():
        ref[...] += val


def rope_tables(positions, inv_freq):
    s = positions.shape[0]
    t = min(ROW_TILE, s)
    half = inv_freq.shape[1]

    def body(p_ref, f_ref, c_ref, s_ref):
        ang = p_ref[...].astype(F32) * f_ref[...]
        c_ref[...] = jnp.cos(ang)
        s_ref[...] = jnp.sin(ang)

    return pl.pallas_call(
        body, grid=(s // t,), in_specs=[_row_spec(t, 1), _vec_spec(half)],
        out_specs=[_row_spec(t, half), _row_spec(t, half)], out_shape=[_sds((s, half), F32)] * 2, name="rope_tables",
        compiler_params=_params(("parallel",)))(positions, inv_freq)


def rms_cast(h, g):
    s, d = h.shape
    t = min(ROW_TILE, s)

    def body(h_ref, g_ref, o_ref):
        x = h_ref[...]
        r = lax.rsqrt(jnp.mean(x * x, axis=-1, keepdims=True) + RMS_EPS)
        o_ref[...] = (x * r * g_ref[...]).astype(BF16)

    return pl.pallas_call(body, grid=(s // t,), in_specs=[_row_spec(t, d), _vec_spec(d)], out_specs=_row_spec(t, d),
                          out_shape=_sds((s, d), BF16), name="rms_cast", compiler_params=_params(("parallel",)))(h, g)


def _rms_bwd_math(x, g, du):
    r = lax.rsqrt(jnp.mean(x * x, axis=-1, keepdims=True) + RMS_EPS)
    gd = g * du
    dx = r * gd - x * (r * r * r) * jnp.mean(x * gd, axis=-1, keepdims=True)
    dg = jnp.sum(x * r * du, axis=0, keepdims=True)
    return dx, dg


def rms_bwd(h, g, du, dres=None):
    s, d = h.shape
    t = min(ROW_TILE, s)

    def body(*refs):
        if dres is None:
            h_ref, g_ref, du_ref, dh_ref, dg_ref = refs
        else:
            h_ref, g_ref, du_ref, dr_ref, dh_ref, dg_ref = refs
        dx, dg = _rms_bwd_math(h_ref[...], g_ref[...], du_ref[...])
        if dres is not None:
            dx = dx + dr_ref[...]
        dh_ref[...] = dx
        _acc_rows(dg_ref, pl.program_id(0), dg)

    in_specs = [_row_spec(t, d), _vec_spec(d), _row_spec(t, d)]
    ops = [h, g, du]
    if dres is not None:
        in_specs.append(_row_spec(t, d))
        ops.append(dres)
    return pl.pallas_call(body, grid=(s // t,), in_specs=in_specs, out_specs=[_row_spec(t, d), _vec_spec(d)],
                          out_shape=[_sds((s, d), F32), _sds((1, d), F32)], name="rms_bwd",
                          compiler_params=_params(("arbitrary",)))(*ops)


def loss_head(h, g, target):
    s, d = h.shape
    t = min(ROW_TILE, s)

    def body(h_ref, g_ref, t_ref, dh_ref, dg_ref, loss_ref):
        x = h_ref[...]
        gg = g_ref[...]
        r = lax.rsqrt(jnp.mean(x * x, axis=-1, keepdims=True) + RMS_EPS)
        err = x * r * gg - t_ref[...]
        part = 0.5 * jnp.sum(jnp.mean(err * err, axis=-1, keepdims=True), axis=0, keepdims=True)
        dy = err * (1.0 / d)
        dx, dg = _rms_bwd_math(x, gg, dy)
        dh_ref[...] = dx
        i = pl.program_id(0)
        _acc_rows(dg_ref, i, dg)
        _acc_rows(loss_ref, i, jnp.broadcast_to(part, (1, 128)))

    return pl.pallas_call(
        body, grid=(s // t,), in_specs=[_row_spec(t, d), _vec_spec(d), _row_spec(t, d)],
        out_specs=[_row_spec(t, d), _vec_spec(d), _vec_spec(128)],
        out_shape=[_sds((s, d), F32), _sds((1, d), F32), _sds((1, 128), F32)], name="loss_head",
        compiler_params=_params(("arbitrary",)))(h, g, target)


def _rot(x, cos, sin):
    half = x.shape[-1] // 2
    x1, x2 = x[:, :half], x[:, half:]
    return jnp.concatenate([x1 * cos - x2 * sin, x2 * cos + x1 * sin], axis=-1)


def _rot_t(dy, cos, sin):
    half = dy.shape[-1] // 2
    d1, d2 = dy[:, :half], dy[:, half:]
    return jnp.concatenate([d1 * cos + d2 * sin, d2 * cos - d1 * sin], axis=-1)


def _decay_tables(t):
    log_gamma = jnp.log(1.0 - jnp.power(2.0, -5.0 - jnp.arange(RET_HEADS, dtype=F32)))
    idx = jnp.arange(t, dtype=F32)
    dist = jnp.abs(idx[:, None] - idx[None, :])
    chunk = jnp.arange(t) // CHUNK
    seen = chunk[None, :] <= chunk[:, None]
    d_tile = jnp.where(seen[None], jnp.exp(log_gamma[:, None, None] * dist), 0.0)
    decay_q = jnp.exp(log_gamma[:, None] * (idx[None, :] + 1.0))[:, :, None]
    decay_k = jnp.exp(log_gamma[:, None] * (t - 1.0 - idx[None, :]))[:, :, None]
    decay_tile = jnp.exp(log_gamma * t)[:, None, None]
    return d_tile, decay_q, decay_k, decay_tile


_QK_SCALE = RET_QK_DIM ** -0.5


def _retention_specs(t, nt, order):
    dk, dv = RET_QK_DIM, RET_V_DIM
    hmap = lambda h, i: (h, 0, 0)
    qkv = [pl.BlockSpec((t, dk), lambda h, i: (order(i), h)),
           pl.BlockSpec((t, dk), lambda h, i: (order(i), RET_HEADS + h)),
           pl.BlockSpec((t, dv), lambda h, i: (order(i), 4 + h))]
    rope = [pl.BlockSpec((t, dk // 2), lambda h, i: (order(i), 0))] * 2
    tables = [pl.BlockSpec((None, t, t), hmap), pl.BlockSpec((None, t, 1), hmap), pl.BlockSpec((None, t, 1), hmap),
              pl.BlockSpec((None, 1, 1), hmap)]
    return qkv, rope, tables


def retention_fwd(p, cos, sin, gn_g, tables):
    s = p.shape[0]
    t = min(RET_TILE, s)
    nt = s // t
    dk, dv = RET_QK_DIM, RET_V_DIM

    def body(q_ref, k_ref, v_ref, cos_ref, sin_ref, di_ref, dq_ref, dkk_ref, dc_ref, gr_ref, gn_ref,
             o_ref, z_ref, st_ref, state):
        @pl.when(pl.program_id(1) == 0)
        def _():
            state[...] = jnp.zeros_like(state)

        cs, sn = cos_ref[...], sin_ref[...]
        qb = (_rot(q_ref[...], cs, sn) * _QK_SCALE).astype(BF16)
        kr = _rot(k_ref[...], cs, sn)
        vb = v_ref[...].astype(BF16)
        st = state[...].astype(BF16)
        st_ref[...] = st
        scores = _nt(qb, kr.astype(BF16)) * di_ref[...]
        o = _nn(scores.astype(BF16), vb) + _nn(qb, st) * dq_ref[...]
        state[...] = state[...] * dc_ref[...] + _tn((kr * dkk_ref[...]).astype(BF16), vb)
        o_ref[...] = o
        mu = jnp.mean(o, axis=-1, keepdims=True)
        oc = o - mu
        var = jnp.mean(oc * oc, axis=-1, keepdims=True)
        y = oc * lax.rsqrt(var + LN_EPS) * gn_ref[...]
        gr = gr_ref[...]
        z_ref[...] = (gr * _sigmoid(gr) * y).astype(BF16)

    qkv, rope, tabs = _retention_specs(t, nt, lambda i: i)
    in_specs = qkv + rope + tabs + [pl.BlockSpec((t, dv), lambda h, i: (i, 8 + h)),
                                    pl.BlockSpec((1, dv), lambda h, i: (0, h))]
    out_specs = [pl.BlockSpec((t, dv), lambda h, i: (i, h)),
                 pl.BlockSpec((t, dv), lambda h, i: (i, h)),
                 pl.BlockSpec((None, None, dk, dv), lambda h, i: (h, i, 0, 0))]
    out_shape = [_sds((s, RET_HEADS * dv), F32), _sds((s, RET_HEADS * dv), BF16), _sds((RET_HEADS, nt, dk, dv), BF16)]
    return pl.pallas_call(
        body, grid=(RET_HEADS, nt), in_specs=in_specs, out_specs=out_specs, out_shape=out_shape,
        scratch_shapes=[pltpu.VMEM((dk, dv), F32)], name="retention_fwd",
        compiler_params=_params(("parallel", "arbitrary")))(p, p, p, cos, sin, *tables, p, gn_g)


def gn_gate_bwd(o, p, gn_g, dz):
    s = o.shape[0]
    t = min(ROW_TILE, s)
    dv = RET_V_DIM
    w = RET_HEADS * dv

    def body(o_ref, gr_ref, gn_ref, dz_ref, do_ref, dgr_ref, dgn_ref):
        dgn_parts = []
        for h in range(RET_HEADS):
            sl = slice(h * dv, (h + 1) * dv)
            oo = o_ref[:, sl]
            gr = gr_ref[:, sl]
            dz = dz_ref[:, sl]
            gn = gn_ref[:, sl]
            mu = jnp.mean(oo, axis=-1, keepdims=True)
            oc = oo - mu
            rstd = lax.rsqrt(jnp.mean(oc * oc, axis=-1, keepdims=True) + LN_EPS)
            y = oc * rstd
            sg = _sigmoid(gr)
            act = gr * sg
            dyg = dz * act
            dgn_parts.append(jnp.sum(dyg * y, axis=0, keepdims=True))
            dy = dyg * gn
            do_ref[:, sl] = rstd * (dy - jnp.mean(dy, axis=-1, keepdims=True)
                                    - y * jnp.mean(dy * y, axis=-1, keepdims=True))
            dgr_ref[:, sl] = (dz * (y * gn) * (sg * (1.0 + gr * (1.0 - sg)))).astype(BF16)
        _acc_rows(dgn_ref, pl.program_id(0), jnp.concatenate(dgn_parts, axis=-1))

    return pl.pallas_call(
        body, grid=(s // t,), in_specs=[_row_spec(t, w), _row_spec(t, w, 2), _vec_spec(w), _row_spec(t, w)],
        out_specs=[_row_spec(t, w), _row_spec(t, w), _vec_spec(w)],
        out_shape=[_sds((s, w), F32), _sds((s, w), BF16), _sds((1, w), F32)], name="gn_gate_bwd",
        compiler_params=_params(("arbitrary",)))(o, p, gn_g, dz)


def retention_bwd(p, cos, sin, states, do, tables):
    s = p.shape[0]
    t = min(RET_TILE, s)
    nt = s // t
    dk, dv = RET_QK_DIM, RET_V_DIM

    def body(q_ref, k_ref, v_ref, cos_ref, sin_ref, di_ref, dq_ref, dkk_ref, dc_ref, st_ref, do_ref,
             gq_ref, gk_ref, gv_ref, dstate):
        @pl.when(pl.program_id(1) == 0)
        def _():
            dstate[...] = jnp.zeros_like(dstate)

        cs, sn = cos_ref[...], sin_ref[...]
        qb = (_rot(q_ref[...], cs, sn) * _QK_SCALE).astype(BF16)
        kr = _rot(k_ref[...], cs, sn)
        kb = kr.astype(BF16)
        vb = v_ref[...].astype(BF16)
        dmat, dkk = di_ref[...], dkk_ref[...]
        d_o = do_ref[...]
        dob = d_o.astype(BF16)
        dsb = dstate[...].astype(BF16)
        ab = (_nt(qb, kb) * dmat).astype(BF16)
        gv_ref[...] = (_tn(ab, dob) + _nn((kr * dkk).astype(BF16), dsb)).astype(BF16)
        dcb = (d_o * dq_ref[...]).astype(BF16)
        dpb = (_nt(dob, vb) * dmat).astype(BF16)
        dqq = _nt(dcb, st_ref[...]) + _nn(dpb, kb)
        dkv = _tn(dpb, qb) + _nt(vb, dsb) * dkk
        dstate[...] = dstate[...] * dc_ref[...] + _tn(qb, dcb)
        gq_ref[...] = _rot_t(dqq * _QK_SCALE, cs, sn).astype(BF16)
        gk_ref[...] = _rot_t(dkv, cs, sn).astype(BF16)

    rev = lambda i: nt - 1 - i
    qkv, rope, tabs = _retention_specs(t, nt, rev)
    in_specs = qkv + rope + tabs + [pl.BlockSpec((None, None, dk, dv), lambda h, i: (h, rev(i), 0, 0)),
                                    pl.BlockSpec((t, dv), lambda h, i: (rev(i), h))]
    out_specs = [pl.BlockSpec((t, dk), lambda h, i: (rev(i), h)),
                 pl.BlockSpec((t, dk), lambda h, i: (rev(i), h)),
                 pl.BlockSpec((t, dv), lambda h, i: (rev(i), h))]
    out_shape = [_sds((s, RET_HEADS * dk), BF16), _sds((s, RET_HEADS * dk), BF16), _sds((s, RET_HEADS * dv), BF16)]
    return pl.pallas_call(
        body, grid=(RET_HEADS, nt), in_specs=in_specs, out_specs=out_specs, out_shape=out_shape,
        scratch_shapes=[pltpu.VMEM((dk, dv), F32)], name="retention_bwd",
        compiler_params=_params(("parallel", "arbitrary")))(p, p, p, cos, sin, *tables, states, do)


A_COL, B_COL = 6, 7


def _prev_rows_spec(t, halo, width, col):
    per = t // halo
    return pl.BlockSpec((halo, width), lambda i: (jnp.maximum(i * per - 1, 0), col))


def _next_rows_spec(t, halo, width, col, n_rows):
    per = t // halo
    last = n_rows // halo - 1
    return pl.BlockSpec((halo, width), lambda i: (jnp.minimum((i + 1) * per, last), col))


def _shifted_copies(ext, rows):
    for b in range(1, 8):
        ext[b, pl.ds(0, rows - 8), :] = ext[0, pl.ds(b, rows - 8), :]


def _shifted(ext, start, lanes):
    return ext[start % 8, pl.ds(start - start % 8, STRIP_ROWS), lanes]


def conv_fwd(p, dw_w, dw_b, ln_g, ln_b):
    s = p.shape[0]
    t = min(ROW_TILE, s)
    c = D_MODEL
    hl = CONV_HALO

    def body(a_ref, b_ref, ah_ref, bh_ref, w_ref, wb_ref, g_ref, bb_ref, c1_ref, c3_ref, ext):
        i = pl.program_id(0)
        ext[0, pl.ds(0, hl), :] = jnp.where(i > 0, ah_ref[...] * _sigmoid(bh_ref[...]), 0.0)
        ext[0, pl.ds(hl, t), :] = a_ref[...] * _sigmoid(b_ref[...])
        _shifted_copies(ext, t + hl)
        assert STRIP_LANES == c
        first = hl - (CONV_WIDTH - 1)
        ls = slice(0, c)
        for r0 in range(0, t, STRIP_ROWS):
            accs = [jnp.broadcast_to(wb_ref[...], (STRIP_ROWS, c)), jnp.zeros((STRIP_ROWS, c), F32)]
            for j in range(CONV_WIDTH):
                accs[j % 2] = accs[j % 2] + w_ref[j:j + 1, :] * _shifted(ext, r0 + first + j, ls)
            acc = accs[0] + accs[1]
            c1_ref[r0:r0 + STRIP_ROWS, :] = acc
            mu = jnp.mean(acc, axis=-1, keepdims=True)
            xc = acc - mu
            var = jnp.mean(xc * xc, axis=-1, keepdims=True)
            c2 = xc * lax.rsqrt(var + LN_EPS) * g_ref[...] + bb_ref[...]
            c3_ref[r0:r0 + STRIP_ROWS, :] = (c2 * _sigmoid(c2)).astype(BF16)

    in_specs = [_row_spec(t, c, A_COL), _row_spec(t, c, B_COL),
                _prev_rows_spec(t, hl, c, A_COL), _prev_rows_spec(t, hl, c, B_COL),
                pl.BlockSpec((CONV_WIDTH, c), lambda i: (0, 0)), _vec_spec(c), _vec_spec(c), _vec_spec(c)]
    return pl.pallas_call(
        body, grid=(s // t,), in_specs=in_specs, out_specs=[_row_spec(t, c), _row_spec(t, c)],
        out_shape=[_sds((s, c), F32), _sds((s, c), BF16)], scratch_shapes=[pltpu.VMEM((8, t + hl, c), F32)],
        name="conv_fwd", compiler_params=_params(("parallel",)))(p, p, p, p, dw_w, dw_b, ln_g, ln_b)


def _ln_silu_bwd_math(x, g, b, d):
    mu = jnp.mean(x, axis=-1, keepdims=True)
    xc = x - mu
    rstd = lax.rsqrt(jnp.mean(xc * xc, axis=-1, keepdims=True) + LN_EPS)
    y = xc * rstd
    c2 = y * g + b
    sg = _sigmoid(c2)
    dc2 = d * (sg * (1.0 + c2 * (1.0 - sg)))
    dy = dc2 * g
    dx = rstd * (dy - jnp.mean(dy, axis=-1, keepdims=True) - y * jnp.mean(dy * y, axis=-1, keepdims=True))
    return dx, jnp.sum(dc2 * y, axis=0, keepdims=True), jnp.sum(dc2, axis=0, keepdims=True)


def conv_dw_bwd(p, dc1, dw_w):
    s = p.shape[0]
    t = min(ROW_TILE, s)
    c = D_MODEL
    hl = CONV_HALO
    nt = s // t

    def body(a_ref, b_ref, ah_ref, bh_ref, d_ref, dn_ref, w_ref, dab_ref, dw_ref, dbias_ref, ext_c, ext_d, dw_s):
        i = pl.program_id(0)
        ext_c[0, pl.ds(0, hl), :] = jnp.where(i > 0, ah_ref[...] * _sigmoid(bh_ref[...]), 0.0)
        ext_c[0, pl.ds(hl, t), :] = a_ref[...] * _sigmoid(b_ref[...])
        ext_d[0, pl.ds(0, t), :] = d_ref[...]
        ext_d[0, pl.ds(t, hl), :] = jnp.where(i < nt - 1, dn_ref[...], 0.0)
        _shifted_copies(ext_c, t + hl)
        _shifted_copies(ext_d, t + hl)
        first = hl - (CONV_WIDTH - 1)

        def fold8(x):
            rows = [x[k:k + 8] for k in range(0, STRIP_ROWS, 8)]
            while len(rows) > 1:
                rows = [rows[k] + rows[k + 1] for k in range(0, len(rows), 2)]
            return rows[0]

        for lane in range(0, c, STRIP_LANES):
            ls = slice(lane, lane + STRIP_LANES)
            for r0 in range(0, t, STRIP_ROWS):
                accs = [jnp.zeros((STRIP_ROWS, STRIP_LANES), F32) for _ in range(2)]
                for j in range(CONV_WIDTH):
                    accs[j % 2] = accs[j % 2] + w_ref[j:j + 1, ls] * _shifted(ext_d, r0 + CONV_WIDTH - 1 - j, ls)
                dc0 = accs[0] + accs[1]
                rs = slice(r0, r0 + STRIP_ROWS)
                sb = _sigmoid(b_ref[rs, ls])
                dab_ref[rs, ls] = (dc0 * sb).astype(BF16)
                dab_ref[rs, slice(c + lane, c + lane + STRIP_LANES)] = (
                    dc0 * a_ref[rs, ls] * sb * (1.0 - sb)).astype(BF16)
            for j0 in range(0, CONV_WIDTH, DW_TAPS):
                taps = range(j0, min(j0 + DW_TAPS, CONV_WIDTH))
                parts = [jnp.zeros((8, STRIP_LANES), F32) for _ in taps]
                for r0 in range(0, t, STRIP_ROWS):
                    d = ext_d[0, r0:r0 + STRIP_ROWS, ls]
                    for k, j in enumerate(taps):
                        parts[k] = parts[k] + fold8(d * _shifted(ext_c, r0 + first + j, ls))
                for k, j in enumerate(taps):
                    dw_s[j:j + 1, ls] = jnp.sum(parts[k], axis=0, keepdims=True)
        _acc_rows(dw_ref, i, dw_s[0:CONV_WIDTH, :])
        d = d_ref[...]
        _acc_rows(dbias_ref, i, jnp.sum(d, axis=0, keepdims=True))

    in_specs = [_row_spec(t, c, A_COL), _row_spec(t, c, B_COL),
                _prev_rows_spec(t, hl, c, A_COL), _prev_rows_spec(t, hl, c, B_COL),
                _row_spec(t, c), _next_rows_spec(t, hl, c, 0, s),
                pl.BlockSpec((CONV_WIDTH, c), lambda i: (0, 0))]
    return pl.pallas_call(
        body, grid=(nt,), in_specs=in_specs,
        out_specs=[_row_spec(t, 2 * c), pl.BlockSpec((CONV_WIDTH, c), lambda i: (0, 0)), _vec_spec(c)],
        out_shape=[_sds((s, 2 * c), BF16), _sds((CONV_WIDTH, c), F32), _sds((1, c), F32)],
        scratch_shapes=[pltpu.VMEM((8, t + hl, c), F32), pltpu.VMEM((8, t + hl, c), F32),
                        pltpu.VMEM((CONV_HALO, c), F32)], name="conv_dw_bwd",
        compiler_params=_params(("arbitrary",)))(p, p, p, p, dc1, dc1, dw_w)


GATE_COL = 4


def mix_out_fwd(p, b_gate, y_a, y_b, b_conv_out, g, l, res, norm_g):
    s = p.shape[0]
    c = D_MODEL
    t = min(MIX_TILE, s)
    lyr = g.shape[0]
    w = g.reshape(lyr, c, c)

    def body(gt_ref, bg_ref, ya_ref, yb_ref, bc_ref, w_ref, r_ref, ng_ref, h_ref, u_ref, mx_ref):
        gs = _sigmoid(gt_ref[...] + bg_ref[...])
        mixed = (gs[:, :c] * ya_ref[...] + gs[:, c:] * (yb_ref[...] + bc_ref[...])).astype(BF16)
        mx_ref[...] = mixed
        total = _nn(mixed, w_ref[...]) + r_ref[...]
        h_ref[...] = total
        r = lax.rsqrt(jnp.mean(total * total, axis=-1, keepdims=True) + RMS_EPS)
        u_ref[...] = (total * r * ng_ref[...]).astype(BF16)

    rows = _row_spec(t, c)
    return pl.pallas_call(
        body, grid=(s // t,),
        in_specs=[_row_spec(t, 2 * c, GATE_COL), _vec_spec(2 * c), rows, rows, _vec_spec(c),
                  pl.BlockSpec((None, c, c), lambda i: (l, 0, 0)), rows, _vec_spec(c)],
        out_specs=[rows, rows, rows], out_shape=[_sds((s, c), F32), _sds((s, c), BF16), _sds((s, c), BF16)],
        name="mix_out_fwd", compiler_params=_params(("parallel",)))(p, b_gate, y_a, y_b, b_conv_out, w, res, norm_g)


def gate_mix_bwd(p, b_gate, y_a, y_b, b_conv_out, dmix):
    s = p.shape[0]
    t = min(ROW_TILE, s)
    c = D_MODEL

    def body(gt_ref, bg_ref, ya_ref, yb_ref, bc_ref, d_ref, dya_ref, dyb_ref, dgt_ref, dbg_ref, dbc_ref):
        gs = _sigmoid(gt_ref[...] + bg_ref[...])
        ga, gb = gs[:, :c], gs[:, c:]
        d = d_ref[...]
        dya = ga * d
        dyb = gb * d
        dya_ref[...] = dya.astype(BF16)
        dyb_ref[...] = dyb.astype(BF16)
        dga = d * ya_ref[...] * ga * (1.0 - ga)
        dgb = d * (yb_ref[...] + bc_ref[...]) * gb * (1.0 - gb)
        dgt_ref[:, :c] = dga.astype(BF16)
        dgt_ref[:, c:] = dgb.astype(BF16)
        i = pl.program_id(0)
        _acc_rows(dbg_ref, i, jnp.concatenate([jnp.sum(dga, axis=0, keepdims=True),
                                               jnp.sum(dgb, axis=0, keepdims=True)], axis=-1))
        _acc_rows(dbc_ref, i, jnp.sum(dyb, axis=0, keepdims=True))

    return pl.pallas_call(
        body, grid=(s // t,),
        in_specs=[_row_spec(t, 2 * c, GATE_COL), _vec_spec(2 * c), _row_spec(t, c), _row_spec(t, c), _vec_spec(c),
                  _row_spec(t, c)],
        out_specs=[_row_spec(t, c), _row_spec(t, c), _row_spec(t, 2 * c), _vec_spec(2 * c), _vec_spec(c)],
        out_shape=[_sds((s, c), BF16), _sds((s, c), BF16), _sds((s, 2 * c), BF16), _sds((1, 2 * c), F32),
                   _sds((1, c), F32)],
        name="gate_mix_bwd", compiler_params=_params(("arbitrary",)))(p, b_gate, y_a, y_b, b_conv_out, dmix)


_X_SCALE = X_HEAD_DIM ** -0.5


def _softmax_rows(sc):
    m = jnp.max(sc, axis=-1, keepdims=True)
    e = jnp.exp(sc - m)
    return e / jnp.sum(e, axis=-1, keepdims=True)


def attn_fwd(qx, kv):
    s, d = qx.shape
    m = kv.shape[0]
    t = min(ATTN_TILE, s)
    hd = X_HEAD_DIM

    def body(q_ref, kv_ref, o_ref):
        for h in range(X_HEADS):
            sl = slice(h * hd, (h + 1) * hd)
            kh = kv_ref[:, sl].astype(BF16)
            vh = kv_ref[:, d + h * hd:d + (h + 1) * hd].astype(BF16)
            pr = _softmax_rows(_nt(q_ref[:, sl], kh) * _X_SCALE)
            o_ref[:, sl] = _nn(pr.astype(BF16), vh).astype(BF16)

    return pl.pallas_call(
        body, grid=(s // t,), in_specs=[_row_spec(t, d), pl.BlockSpec((m, 2 * d), lambda i: (0, 0))],
        out_specs=_row_spec(t, d), out_shape=_sds((s, d), BF16), name="attn_fwd",
        compiler_params=_params(("parallel",)))(qx, kv)


def attn_bwd(qx, kv, dox):
    s, d = qx.shape
    m = kv.shape[0]
    t = min(ATTN_TILE, s)
    hd = X_HEAD_DIM

    def body(q_ref, kv_ref, do_ref, dq_ref, dkv_ref):
        dks, dvs = [], []
        for h in range(X_HEADS):
            sl = slice(h * hd, (h + 1) * hd)
            qh = q_ref[:, sl]
            kh = kv_ref[:, sl].astype(BF16)
            vh = kv_ref[:, d + h * hd:d + (h + 1) * hd].astype(BF16)
            pr = _softmax_rows(_nt(qh, kh) * _X_SCALE)
            doh = do_ref[:, sl].astype(BF16)
            dpr = _nt(doh, vh)
            dvs.append(_tn(pr.astype(BF16), doh))
            ds = pr * (dpr - jnp.sum(dpr * pr, axis=-1, keepdims=True))
            dsb = (ds * _X_SCALE).astype(BF16)
            dq_ref[:, sl] = _nn(dsb, kh).astype(BF16)
            dks.append(_tn(dsb, qh))
        _acc_rows(dkv_ref, pl.program_id(0), jnp.concatenate(dks + dvs, axis=-1))

    return pl.pallas_call(
        body, grid=(s // t,),
        in_specs=[_row_spec(t, d), pl.BlockSpec((m, 2 * d), lambda i: (0, 0)), _row_spec(t, d)],
        out_specs=[_row_spec(t, d), pl.BlockSpec((m, 2 * d), lambda i: (0, 0))],
        out_shape=[_sds((s, d), BF16), _sds((m, 2 * d), F32)], name="attn_bwd",
        compiler_params=_params(("arbitrary",)))(qx, kv, dox)


def _offset_copies(ext, offsets, rows):
    for k, off in enumerate(offsets):
        ext[1 + k, pl.ds(0, rows), :] = ext[0, pl.ds(off, rows), :]


def _ffn_blocks(rows, lanes):
    return [(r0, slice(l0, l0 + FFN_LANES)) for r0 in range(0, rows, FFN_ROWS) for l0 in range(0, lanes, FFN_LANES)]


def ffn_act_fwd(up, dw_w, dw_b):
    s = up.shape[0]
    f = FFN_DIM
    t = min(ROW_TILE, s)
    hl = FFN_HALO

    def body(val_ref, gt_ref, gh_ref, w_ref, b_ref, o_ref, ext):
        i = pl.program_id(0)
        ext[0, pl.ds(0, hl), :] = jnp.where(i > 0, gh_ref[...], 0.0)
        ext[0, pl.ds(hl, t), :] = gt_ref[...]
        _offset_copies(ext, (hl - 2, hl - 1), t)
        for r0, ls in _ffn_blocks(t, f):
            gc = b_ref[:, ls] + w_ref[0:1, ls] * ext[1, pl.ds(r0, FFN_ROWS), ls] \
                + w_ref[1:2, ls] * ext[2, pl.ds(r0, FFN_ROWS), ls] + w_ref[2:3, ls] * ext[0, pl.ds(r0 + hl, FFN_ROWS), ls]
            o_ref[r0:r0 + FFN_ROWS, ls] = (gc * _sigmoid(gc) * val_ref[r0:r0 + FFN_ROWS, ls]).astype(BF16)

    return pl.pallas_call(
        body, grid=(s // t,),
        in_specs=[_row_spec(t, f, 0), _row_spec(t, f, 1), _prev_rows_spec(t, hl, f, 1),
                  pl.BlockSpec((3, f), lambda i: (0, 0)), _vec_spec(f)],
        out_specs=_row_spec(t, f), out_shape=_sds((s, f), BF16), scratch_shapes=[pltpu.VMEM((3, t + hl, f), F32)],
        name="ffn_act_fwd", compiler_params=_params(("parallel",)))(up, up, up, dw_w, dw_b)


def ffn_act_bwd(up, dw_w, dw_b, da):
    s = up.shape[0]
    f = FFN_DIM
    t = min(ROW_TILE, s)
    hl = FFN_HALO
    nt = s // t

    def body(val_ref, valn_ref, gt_ref, gp_ref, gn_ref, da_ref, dan_ref, w_ref, b_ref,
             dup_ref, dw_ref, db_ref, ext_g, ext_d, sums):
        i = pl.program_id(0)
        ext_g[0, pl.ds(0, hl), :] = jnp.where(i > 0, gp_ref[...], 0.0)
        ext_g[0, pl.ds(hl, t), :] = gt_ref[...]
        ext_g[0, pl.ds(hl + t, hl), :] = gn_ref[...]
        _offset_copies(ext_g, (hl - 2, hl - 1), t + hl)
        sums[...] = jnp.zeros_like(sums)

        def fold8(x):
            out = x[0:8]
            for k in range(8, x.shape[0], 8):
                out = out + x[k:k + 8]
            return out

        def gate_block(rows, off, ls, val, da_rows):
            taps = [ext_g[1, pl.ds(off, rows), ls], ext_g[2, pl.ds(off, rows), ls], ext_g[0, pl.ds(off + hl, rows), ls]]
            gc = b_ref[:, ls] + w_ref[0:1, ls] * taps[0] + w_ref[1:2, ls] * taps[1] + w_ref[2:3, ls] * taps[2]
            sg = _sigmoid(gc)
            return taps, gc * sg, da_rows * val * (sg * (1.0 + gc * (1.0 - sg)))

        for r0, ls in _ffn_blocks(t, f):
            rs = slice(r0, r0 + FFN_ROWS)
            da_rows = da_ref[rs, ls]
            taps, act, dgc = gate_block(FFN_ROWS, r0, ls, val_ref[rs, ls], da_rows)
            ext_d[0, rs, ls] = dgc
            dup_ref[rs, ls] = (da_rows * act).astype(BF16)
            for k in range(3):
                sums[8 * k:8 * k + 8, ls] += fold8(dgc * taps[k])
            sums[24:32, ls] += fold8(dgc)
        _, _, dgc_next = gate_block(hl, t, slice(None), valn_ref[...], dan_ref[...])
        ext_d[0, pl.ds(t, hl), :] = jnp.where(i < nt - 1, dgc_next, 0.0)
        _offset_copies(ext_d, (1, 2), t)
        for r0, ls in _ffn_blocks(t, f):
            gate_lanes = slice(f + ls.start, f + ls.stop)
            rs = pl.ds(r0, FFN_ROWS)
            dup_ref[r0:r0 + FFN_ROWS, gate_lanes] = (
                w_ref[2:3, ls] * ext_d[0, rs, ls] + w_ref[1:2, ls] * ext_d[1, rs, ls]
                + w_ref[0:1, ls] * ext_d[2, rs, ls]).astype(BF16)
        rows = [jnp.sum(sums[8 * k:8 * k + 8, :], axis=0, keepdims=True) for k in range(4)]
        _acc_rows(dw_ref, i, jnp.concatenate(rows[:3], axis=0))
        _acc_rows(db_ref, i, rows[3])

    in_specs = [_row_spec(t, f, 0), _next_rows_spec(t, hl, f, 0, s),
                _row_spec(t, f, 1), _prev_rows_spec(t, hl, f, 1), _next_rows_spec(t, hl, f, 1, s),
                _row_spec(t, f), _next_rows_spec(t, hl, f, 0, s),
                pl.BlockSpec((3, f), lambda i: (0, 0)), _vec_spec(f)]
    return pl.pallas_call(
        body, grid=(nt,), in_specs=in_specs,
        out_specs=[_row_spec(t, 2 * f), pl.BlockSpec((3, f), lambda i: (0, 0)), _vec_spec(f)],
        out_shape=[_sds((s, 2 * f), BF16), _sds((3, f), F32), _sds((1, f), F32)],
        scratch_shapes=[pltpu.VMEM((3, t + 2 * hl, f), F32), pltpu.VMEM((3, t + hl, f), F32),
                        pltpu.VMEM((32, f), F32)],
        name="ffn_act_bwd",
        compiler_params=_params(("arbitrary",)))(up, up, up, up, up, da, da, dw_w, dw_b)


def local_step(x, mem, positions, target, gw, sp):
    n_layers = gw["w_in"].shape[0]
    inv_freq = 1.0 / (ROPE_THETA ** (jnp.arange(0, RET_QK_DIM, 2, dtype=F32) / RET_QK_DIM))
    cos, sin = rope_tables(positions, inv_freq[None, :])
    tables = _decay_tables(min(RET_TILE, x.shape[0]))
    row = lambda name, l: sp[name][l][None, :]

    saved = []
    h = x
    u = rms_cast(x, row("norm_mix_g", 0))
    for l in range(n_layers):
        a = {"h0": h}
        a["u"] = u
        a["p"] = mm_fwd("mm_in", a["u"], gw["w_in"], l, True)
        a["o"], a["z"], a["states"] = retention_fwd(a["p"], cos, sin, row("ret_gn_g", l), tables)
        a["y_a"] = mm_fwd("mm_ret_out", a["z"], gw["w_ret_out"], l, False)
        a["c1"], a["c3"] = conv_fwd(a["p"], sp["conv_dw_w"][l], row("conv_dw_b", l), row("conv_ln_g", l),
                                    row("conv_ln_b", l))
        a["y_b"] = mm_fwd("mm_conv_out", a["c3"], gw["w_conv_out"], l, False)
        a["h1"], a["hx"], a["mixed"] = mix_out_fwd(a["p"], row("b_gate", l), a["y_a"], a["y_b"], row("b_conv_out", l),
                                                   gw["w_mix_out"], l, h, row("norm_xattn_g", l))
        a["qx"] = mm_fwd("mm_xq", a["hx"], gw["w_xq"], l, False, out_dtype=BF16)
        a["mem_n"] = rms_cast(mem, row("norm_mem_g", l))
        a["kv"] = mm_fwd("mm_xkv", a["mem_n"], gw["w_xkv"], l, True)
        a["ox"] = attn_fwd(a["qx"], a["kv"])
        a["h2"], a["hf"] = mm_fwd("mm_xo", a["ox"], gw["w_xo"], l, False, res=a["h1"], norm_g=row("norm_ffn_g", l))
        a["up"] = mm_fwd("mm_up", a["hf"], gw["w_up"], l, True)
        a["act"] = ffn_act_fwd(a["up"], sp["ffn_dw_w"][l], row("ffn_dw_b", l))
        if l + 1 < n_layers:
            h, u = mm_fwd("mm_down", a["act"], gw["w_down"], l, False, res=a["h2"], norm_g=row("norm_mix_g", l + 1))
        else:
            h = mm_fwd("mm_down", a["act"], gw["w_down"], l, False, res=a["h2"])
        saved.append(a)

    dh, d_final_g, loss = loss_head(h, sp["norm_final_g"][None, :], target)

    big = {}

    def dw(name, key, act, dy, col, l):
        big[key] = mm_dw(name, act, dy, col, l, n_layers, big.get(key))

    small = {n: [None] * n_layers for n in SMALL_REPL + SMALL_SHARDED if n != "norm_final_g"}
    for l in range(n_layers - 1, -1, -1):
        a = saved[l]
        d_act = mm_dx("mm_down_dx", dh, gw["w_down"], l, False)
        dw("mm_down_dw", "w_down", a["act"], dh, False, l)
        d_up, small["ffn_dw_w"][l], small["ffn_dw_b"][l] = ffn_act_bwd(a["up"], sp["ffn_dw_w"][l],
                                                                        row("ffn_dw_b", l), d_act)
        dh, small["norm_ffn_g"][l] = mm_dx("mm_up_dx", d_up, gw["w_up"], l, True,
                                           rms=(a["h2"], row("norm_ffn_g", l), dh))
        dw("mm_up_dw", "w_up", a["hf"], d_up, True, l)
        d_ox = mm_dx("mm_xo_dx", dh, gw["w_xo"], l, False)
        dw("mm_xo_dw", "w_xo", a["ox"], dh, False, l)
        d_qx, d_kv = attn_bwd(a["qx"], a["kv"], d_ox)
        dw("mm_xq_dw", "w_xq", a["hx"], d_qx, False, l)
        d_mem_n = mm_dx("mm_xkv_dx", d_kv, gw["w_xkv"], l, True)
        dw("mm_xkv_dw", "w_xkv", a["mem_n"], d_kv, True, l)
        _, small["norm_mem_g"][l] = rms_bwd(mem, row("norm_mem_g", l), d_mem_n)
        dh, small["norm_xattn_g"][l] = mm_dx("mm_xq_dx", d_qx, gw["w_xq"], l, False,
                                             rms=(a["h1"], row("norm_xattn_g", l), dh))
        d_mixed = mm_dx("mm_mix_out_dx", dh, gw["w_mix_out"], l, False)
        dw("mm_mix_out_dw", "w_mix_out", a["mixed"], dh, False, l)
        d_ya, d_yb, dp_gate, small["b_gate"][l], small["b_conv_out"][l] = gate_mix_bwd(
            a["p"], row("b_gate", l), a["y_a"], a["y_b"], row("b_conv_out", l), d_mixed)
        dw("mm_conv_out_dw", "w_conv_out", a["c3"], d_yb, False, l)
        d_c1, small["conv_ln_g"][l], small["conv_ln_b"][l] = mm_dx(
            "mm_conv_out_dx", d_yb, gw["w_conv_out"], l, False,
            ln=(a["c1"], row("conv_ln_g", l), row("conv_ln_b", l)))
        dp_conv, small["conv_dw_w"][l], small["conv_dw_b"][l] = conv_dw_bwd(a["p"], d_c1, sp["conv_dw_w"][l])
        d_z = mm_dx("mm_ret_out_dx", d_ya, gw["w_ret_out"], l, False)
        dw("mm_ret_out_dw", "w_ret_out", a["z"], d_ya, False, l)
        d_o, dp_gret, small["ret_gn_g"][l] = gn_gate_bwd(a["o"], a["p"], row("ret_gn_g", l), d_z)
        dp_q, dp_k, dp_v = retention_bwd(a["p"], cos, sin, a["states"], d_o, tables)
        dp = jnp.concatenate([dp_q, dp_k, dp_v, dp_gret, dp_conv, dp_gate], axis=1)
        dw("mm_in_dw", "w_in", a["u"], dp, True, l)
        dh, small["norm_mix_g"][l] = mm_dx("mm_in_dx", dp, gw["w_in"], l, True,
                                           rms=(a["h0"], row("norm_mix_g", l), dh))

    small = {n: jnp.stack([g.reshape(sp[n].shape[1:]) for g in v]) for n, v in small.items()}
    small["norm_final_g"] = d_final_g.reshape(-1)
    return loss, dh, big, small


_ANY = pl.BlockSpec(memory_space=pl.ANY)


def _place():
    x, y, c = lax.axis_index("x"), lax.axis_index("y"), lax.axis_index("c")
    return x, y, c


def _other_chips(x, y):
    return [(1 - x, y), (x, 1 - y), (1 - x, 1 - y)]


def place_shard(w, chip):
    lyr, a, b = w.shape
    t = _flat_tile(a, b)

    def body(chip_ref, w_ref, o_ref):
        o_ref[...] = w_ref[...].astype(BF16)

    grid_spec = pltpu.PrefetchScalarGridSpec(
        num_scalar_prefetch=1, grid=(lyr, a // t),
        in_specs=[pl.BlockSpec((None, t, b), lambda l, i, cr: (l, i, 0))],
        out_specs=pl.BlockSpec((None, None, t, b), lambda l, i, cr: (l, cr[0], i, 0)))
    return pl.pallas_call(body, grid_spec=grid_spec, out_shape=_sds((lyr, N_CHIPS, a, b), BF16), name="place_shard",
                          compiler_params=_params(("parallel", "parallel")))(chip, w)


def gather_weights(bufs):
    n = len(bufs)

    def body(*refs):
        outs = refs[n:2 * n]
        ici_send, ici_recv, pair_send, pair_recv = refs[2 * n:]
        x, y, c = _place()
        mine = 2 * x + y
        chips = _other_chips(x, y)

        def part(w, core, slot):
            lh = outs[w].shape[0] // 2
            return outs[w].at[pl.ds(core * lh, lh), slot]

        def over_ici(w, j, slot):
            px, py = chips[j]
            return pltpu.make_async_remote_copy(
                src_ref=part(w, c, slot), dst_ref=part(w, c, slot), send_sem=ici_send.at[w, j],
                recv_sem=ici_recv.at[w, j], device_id=(px, py, c), device_id_type=MESH)

        def to_pair(w, j, core):
            px, py = chips[j]
            return pltpu.make_async_remote_copy(
                src_ref=part(w, core, 2 * px + py), dst_ref=part(w, core, 2 * px + py), send_sem=pair_send.at[w, j],
                recv_sem=pair_recv.at[w, j], device_id=(x, y, 1 - c), device_id_type=MESH)

        for w in range(n):
            for j in range(3):
                over_ici(w, j, mine).start()
        for w in range(n):
            for j, (px, py) in enumerate(chips):
                over_ici(w, j, 2 * px + py).wait_recv()
                to_pair(w, j, c).start()
        for w in range(n):
            for j in range(3):
                to_pair(w, j, 1 - c).wait_recv()
                to_pair(w, j, c).wait_send()
                over_ici(w, j, mine).wait_send()

    return pl.pallas_call(
        body, in_specs=[_ANY] * n, out_specs=[_ANY] * n, out_shape=[_sds(b.shape, b.dtype) for b in bufs],
        input_output_aliases={i: i for i in range(n)},
        scratch_shapes=[pltpu.SemaphoreType.DMA((n, 3))] * 4,
        name="gather_weights", compiler_params=_params())(*bufs)


def pair_exchange(grads):
    n = len(grads)

    def body(*refs):
        ins, outs = refs[:n], refs[n:2 * n]
        send_sems, recv_sems = refs[2 * n:]
        x, y, c = _place()
        cps = []
        for w in range(n):
            lh = ins[w].shape[0] // 2
            cp = pltpu.make_async_remote_copy(
                src_ref=ins[w].at[pl.ds((1 - c) * lh, lh)], dst_ref=outs[w], send_sem=send_sems.at[w],
                recv_sem=recv_sems.at[w], device_id=(x, y, 1 - c), device_id_type=MESH)
            cp.start()
            cps.append(cp)
        for cp in cps:
            cp.wait_send()
            cp.wait_recv()

    out_shape = [_sds((g.shape[0] // 2,) + g.shape[1:], g.dtype) for g in grads]
    return pl.pallas_call(
        body, in_specs=[_ANY] * n, out_specs=[_ANY] * n, out_shape=out_shape,
        scratch_shapes=[pltpu.SemaphoreType.DMA((n,)), pltpu.SemaphoreType.DMA((n,))],
        name="pair_exchange", compiler_params=_params())(*grads)


def chip_exchange(parts):
    n = len(parts)

    def body(*refs):
        ins, outs = refs[:n], refs[n:2 * n]
        send_sems, recv_sems = refs[2 * n:]
        x, y, c = _place()
        mine = 2 * x + y
        chips = _other_chips(x, y)

        def copy(w, j, slot):
            px, py = chips[j]
            return pltpu.make_async_remote_copy(
                src_ref=ins[w].at[:, 2 * px + py], dst_ref=outs[w].at[slot], send_sem=send_sems.at[w, j],
                recv_sem=recv_sems.at[w, j], device_id=(px, py, c), device_id_type=MESH)

        for w in range(n):
            for j in range(3):
                copy(w, j, mine).start()
        for w in range(n):
            for j, (px, py) in enumerate(chips):
                copy(w, j, 2 * px + py).wait_recv()
                copy(w, j, mine).wait_send()

    out_shape = [_sds((N_CHIPS, g.shape[0]) + g.shape[2:], g.dtype) for g in parts]
    return pl.pallas_call(
        body, in_specs=[_ANY] * n, out_specs=[_ANY] * n, out_shape=out_shape,
        scratch_shapes=[pltpu.SemaphoreType.DMA((n, 3)), pltpu.SemaphoreType.DMA((n, 3))],
        name="chip_exchange", compiler_params=_params())(*parts)


def pair_share(halves):
    n = len(halves)

    def body(*refs):
        ins, outs = refs[:n], refs[n:2 * n]
        send_sems, recv_sems = refs[2 * n:]
        x, y, c = _place()
        cps = []
        for w in range(n):
            cp = pltpu.make_async_remote_copy(
                src_ref=ins[w], dst_ref=outs[w], send_sem=send_sems.at[w], recv_sem=recv_sems.at[w],
                device_id=(x, y, 1 - c), device_id_type=MESH)
            cp.start()
            cps.append(cp)
        for cp in cps:
            cp.wait_send()
            cp.wait_recv()

    return pl.pallas_call(
        body, in_specs=[_ANY] * n, out_specs=[_ANY] * n, out_shape=[_sds(g.shape, g.dtype) for g in halves],
        scratch_shapes=[pltpu.SemaphoreType.DMA((n,)), pltpu.SemaphoreType.DMA((n,))],
        name="pair_share", compiler_params=_params())(*halves)


def all_reduce_small(vec):
    r, lanes = vec.shape

    def body(v_ref, o_ref, buf, send_sems, recv_sems):
        x, y, c = _place()
        me = 4 * x + 2 * y + c
        buf[me] = v_ref[...]
        cps = []
        for k in range(1, N_DEV):
            peer = (me + k) % N_DEV
            cp = pltpu.make_async_remote_copy(
                src_ref=v_ref, dst_ref=buf.at[me], send_sem=send_sems.at[k - 1], recv_sem=recv_sems.at[k - 1],
                device_id=(peer // 4, (peer // 2) % 2, peer % 2), device_id_type=MESH)
            cp.start()
            cps.append(cp)
        for k in range(1, N_DEV):
            src = (me + N_DEV - k) % N_DEV
            cps[k - 1].wait_send()
            pltpu.make_async_remote_copy(
                src_ref=v_ref, dst_ref=buf.at[src], send_sem=send_sems.at[k - 1], recv_sem=recv_sems.at[k - 1],
                device_id=(src // 4, (src // 2) % 2, src % 2), device_id_type=MESH).wait_recv()
        acc = buf[0]
        for d in range(1, N_DEV):
            acc = acc + buf[d]
        o_ref[...] = acc

    vm = pl.BlockSpec(memory_space=pltpu.VMEM)
    return pl.pallas_call(
        body, in_specs=[vm], out_specs=vm, out_shape=_sds((r, lanes), F32),
        scratch_shapes=[pltpu.VMEM((N_DEV, r, lanes), F32), pltpu.SemaphoreType.DMA((N_DEV - 1,)),
                        pltpu.SemaphoreType.DMA((N_DEV - 1,))],
        name="all_reduce_small", compiler_params=_params())(vec)


ELEMENTWISE_BLOCK_BYTES = 1 << 20


def _flat_tile(rows, cols):
    for t in (512, 256, 128, 64, 32, 16, 8):
        if rows % t == 0 and t * cols * 4 <= ELEMENTWISE_BLOCK_BYTES:
            return t
    return rows


def add_pair(g, r, half):
    lyr, _, a, b = g.shape
    lh = lyr // 2
    rows = lh * 4 * a
    t = _flat_tile(rows, b)
    nb = rows // t
    g2 = g.reshape(lyr * 4 * a, b)
    r2 = r.reshape(rows, b)

    def body(half_ref, g_ref, r_ref, o_ref):
        o_ref[...] = (g_ref[...] + r_ref[...]).astype(BF16)

    grid_spec = pltpu.PrefetchScalarGridSpec(
        num_scalar_prefetch=1, grid=(nb,),
        in_specs=[pl.BlockSpec((t, b), lambda i, hr: (hr[0] * nb + i, 0)), pl.BlockSpec((t, b), lambda i, hr: (i, 0))],
        out_specs=pl.BlockSpec((t, b), lambda i, hr: (i, 0)))
    out = pl.pallas_call(body, grid_spec=grid_spec, out_shape=_sds((rows, b), BF16), name="add_pair",
                         compiler_params=_params(("parallel",)))(half, g2, r2)
    return out.reshape(lh, 4, a, b)


def sum_chips(own, parts, chip):
    _, lh, a, b = parts.shape
    t = _flat_tile(a, b)

    def body(chip_ref, own_ref, p_ref, o_ref):
        mine = chip_ref[0]

        def term(s):
            return jnp.where(mine == s, own_ref[...], p_ref[s]).astype(F32)

        o_ref[...] = ((term(0) + term(1)) + term(2)) + term(3)

    grid_spec = pltpu.PrefetchScalarGridSpec(
        num_scalar_prefetch=1, grid=(lh, a // t),
        in_specs=[pl.BlockSpec((None, None, t, b), lambda l, i, cr: (l, cr[0], i, 0)),
                  pl.BlockSpec((N_CHIPS, None, t, b), lambda l, i, cr: (0, l, i, 0))],
        out_specs=pl.BlockSpec((None, t, b), lambda l, i, cr: (l, i, 0)))
    return pl.pallas_call(body, grid_spec=grid_spec, out_shape=_sds((lh, a, b), F32), name="sum_chips",
                          compiler_params=_params(("parallel", "parallel")))(chip, own, parts)


def _adamw_math(w, g, m, v):
    mm = ADAM_B1 * m + (1.0 - ADAM_B1) * g
    vv = ADAM_B2 * v + (1.0 - ADAM_B2) * jnp.square(g)
    m_hat = mm / (1.0 - ADAM_B1 ** ADAM_STEP)
    v_hat = vv / (1.0 - ADAM_B2 ** ADAM_STEP)
    return -ADAM_LR * (m_hat / (jnp.sqrt(v_hat) + ADAM_EPS) + ADAM_WD * w), mm, vv


def adamw(w, g, m, v):
    shape = w.shape
    c = shape[-1]
    rows = int(np.prod(shape[:-1])) if len(shape) > 1 else 1
    t = _flat_tile(rows, c)
    flat = lambda z: z.reshape(rows, c)

    def body(w_ref, g_ref, m_ref, v_ref, d_ref, nm_ref, nv_ref):
        d_ref[...], nm_ref[...], nv_ref[...] = _adamw_math(w_ref[...], g_ref[...], m_ref[...], v_ref[...])

    spec = pl.BlockSpec((t, c), lambda i: (i, 0))
    outs = pl.pallas_call(body, grid=(rows // t,), in_specs=[spec] * 4, out_specs=[spec] * 3,
                          out_shape=[_sds((rows, c), F32)] * 3, name="adamw",
                          compiler_params=_params(("parallel",)))(flat(w), flat(g), flat(m), flat(v))
    return tuple(o.reshape(shape) for o in outs)


def adamw_halves(w, g_own, g_other, m, v, core):
    lyr, a, b = w.shape
    lh = lyr // 2
    t = _flat_tile(a, b)

    def body(core_ref, w_ref, go_ref, gs_ref, m_ref, v_ref, g_ref, d_ref, nm_ref, nv_ref):
        own = pl.program_id(0) // lh == core_ref[0]
        g = jnp.where(own, go_ref[...], gs_ref[...])
        g_ref[...] = g
        d_ref[...], nm_ref[...], nv_ref[...] = _adamw_math(w_ref[...], g, m_ref[...], v_ref[...])

    full = pl.BlockSpec((None, t, b), lambda l, i, cr: (l, i, 0))
    own_spec = pl.BlockSpec((None, t, b), lambda l, i, cr: (jnp.clip(l - cr[0] * lh, 0, lh - 1), i, 0))
    other_spec = pl.BlockSpec((None, t, b), lambda l, i, cr: (jnp.clip(l - (1 - cr[0]) * lh, 0, lh - 1), i, 0))
    grid_spec = pltpu.PrefetchScalarGridSpec(
        num_scalar_prefetch=1, grid=(lyr, a // t), in_specs=[full, own_spec, other_spec, full, full],
        out_specs=[full] * 4)
    return pl.pallas_call(body, grid_spec=grid_spec, out_shape=[_sds(w.shape, F32)] * 4, name="adamw_halves",
                          compiler_params=_params(("parallel", "parallel")))(core, w, g_own, g_other, m, v)


def _pack(parts):
    flat = jnp.concatenate([p.reshape(-1) for p in parts])
    pad = (-flat.shape[0]) % 1024
    return jnp.pad(flat, (0, pad)).reshape(-1, 128)


def _unpack(packed, shapes):
    flat = packed.reshape(-1)
    out, off = [], 0
    for shp in shapes:
        size = int(np.prod(shp))
        out.append(flat[off:off + size].reshape(shp))
        off += size
    return out


def kernel(x, mem, positions, norm_mix_g, w_in, b_gate, ret_gn_g, w_ret_out, conv_dw_w, conv_dw_b, conv_ln_g, conv_ln_b, w_conv_out, b_conv_out, w_mix_out, norm_xattn_g, norm_mem_g, w_xq, w_xkv, w_xo, norm_ffn_g, w_up, ffn_dw_w, ffn_dw_b, w_down, norm_final_g, loss_target, m_norm_mix_g, m_w_in, m_b_gate, m_ret_gn_g, m_w_ret_out, m_conv_dw_w, m_conv_dw_b, m_conv_ln_g, m_conv_ln_b, m_w_conv_out, m_b_conv_out, m_w_mix_out, m_norm_xattn_g, m_norm_mem_g, m_w_xq, m_w_xkv, m_w_xo, m_norm_ffn_g, m_w_up, m_ffn_dw_w, m_ffn_dw_b, m_w_down, m_norm_final_g, v_norm_mix_g, v_w_in, v_b_gate, v_ret_gn_g, v_w_ret_out, v_conv_dw_w, v_conv_dw_b, v_conv_ln_g, v_conv_ln_b, v_w_conv_out, v_b_conv_out, v_w_mix_out, v_norm_xattn_g, v_norm_mem_g, v_w_xq, v_w_xkv, v_w_xo, v_norm_ffn_g, v_w_up, v_ffn_dw_w, v_ffn_dw_b, v_w_down, v_norm_final_g):
    args = locals()
    w = {n: args[n] for n in WEIGHTS}
    m = {n: args["m_" + n] for n in WEIGHTS}
    v = {n: args["v_" + n] for n in WEIGHTS}
    chip = 2 * lax.axis_index("x") + lax.axis_index("y")
    core = lax.axis_index("c")

    chip_op = chip.reshape(1).astype(jnp.int32)
    core_op = core.reshape(1).astype(jnp.int32)
    gathered = gather_weights([place_shard(w[n], chip_op) for n in BIG])
    gw = dict(zip(BIG, gathered))

    sp = {n: w[n] for n in SMALL_REPL}
    placed = []
    for n in SMALL_SHARDED:
        cols = w[n].shape[-1]
        full = jnp.zeros(w[n].shape[:-1] + (N_CHIPS * cols,), F32)
        placed.append(lax.dynamic_update_slice_in_dim(full, w[n], chip * cols, axis=2))
    placed_shapes = [p.shape for p in placed]
    gathered_small = all_reduce_small(_pack([jnp.where(core == 0, p, 0.0) for p in placed]))
    for n, arr in zip(SMALL_SHARDED, _unpack(gathered_small, placed_shapes)):
        sp[n] = arr

    loss, grad_x, big, small = local_step(x[0], mem[0], positions.reshape(-1, 1), loss_target[0], gw, sp)

    names = [n for n in SMALL_REPL + SMALL_SHARDED]
    shapes = [small[n].shape for n in names] + [(128,)]
    reduced = _unpack(all_reduce_small(_pack([small[n] for n in names] + [loss.reshape(-1)])), shapes)
    grads = dict(zip(names, reduced[:-1]))
    loss_out = reduced[-1][0]
    for n in SMALL_SHARDED:
        cols = w[n].shape[-1]
        grads[n] = lax.dynamic_slice_in_dim(grads[n], chip * cols, cols, axis=2)

    blist = [big[n] for n in BIG]
    from_pair = pair_exchange(blist)
    pair_sum = [add_pair(g, r, core_op) for g, r in zip(blist, from_pair)]
    from_chips = chip_exchange(pair_sum)
    halves = [sum_chips(own, parts, chip_op) for own, parts in zip(pair_sum, from_chips)]
    other_halves = pair_share(halves)

    delta, new_m, new_v = {}, {}, {}
    for n, g_own, g_other in zip(BIG, halves, other_halves):
        grads[n], delta[n], new_m[n], new_v[n] = adamw_halves(w[n], g_own, g_other, m[n], v[n], core_op)
    for n in WEIGHTS:
        if n not in BIG:
            delta[n], new_m[n], new_v[n] = adamw(w[n], grads[n], m[n], v[n])
    return (loss_out, grad_x[None], *[grads[n] for n in WEIGHTS], *[delta[n] for n in WEIGHTS],
            *[new_m[n] for n in WEIGHTS], *[new_v[n] for n in WEIGHTS])
```
